```python
import math
import jax, jax.numpy as jnp
from jax import lax
import numpy as np

D_MODEL = 2048
BATCH = 16
SEQ = 256
DEPTH = 2
DEC_BATCH = 8
DEC_SEQ = 2048
PAST_LEN = 256

GRID_W = 64
RG_WIDTH = 1024
RG_HEADS = 8
RG_HEAD_DIM = RG_WIDTH // RG_HEADS
RG_CONV = 4
RG_C = 8.0
CV_WIDTH = 1024
CV_KERNEL = 31
S5_WIDTH = 1024
S5_GROUP = 16
S5_GROUPS = S5_WIDTH // S5_GROUP
S5_STATE = 64
N_BRANCH = 3
N_IN = RG_WIDTH + 2 * CV_WIDTH + S5_WIDTH + N_BRANCH * D_MODEL
N_EXPERTS = 64
TOP_K = 8
N_ROUTE_GROUPS = 8
TOPK_GROUPS = 4
EXPERT_FF = 512
SHARED_FF = 512
ROUTE_SCALE = 2.5
EPS = 1e-6

kernel_name = 'hybrid_rglru_conformer_s5_moe_diffusion_step'

F32 = jnp.float32


def rmsnorm(x, g):
    xf = x.astype(F32)
    y = xf * lax.rsqrt(jnp.mean(xf * xf, axis=-1, keepdims=True) + EPS)
    return (y * g.astype(F32)).astype(x.dtype)


def layernorm(x, g, b):
    xf = x.astype(F32)
    xc = xf - jnp.mean(xf, axis=-1, keepdims=True)
    y = xc * lax.rsqrt(jnp.mean(xc * xc, axis=-1, keepdims=True) + EPS)
    return (y * g.astype(F32) + b.astype(F32)).astype(x.dtype)


def depthwise_conv(x, w, b, pad_lo, pad_hi):
    y = lax.conv_general_dilated(x, w[:, None, :].astype(x.dtype), window_strides=(1,), padding=[(pad_lo, pad_hi)], dimension_numbers=('NWC', 'WIO', 'NWC'), feature_group_count=x.shape[-1])
    return y + b.astype(x.dtype)


def linear_scan(a, b, h0, reverse):
    if h0 is not None:
        first = -1 if reverse else 0
        b = b.at[:, first].add(a[:, first] * h0)
    def combine(left, right):
        a_l, b_l = left
        a_r, b_r = right
        return a_l * a_r, a_r * b_l + b_r
    _, h = lax.associative_scan(combine, (a, b), reverse=reverse, axis=1)
    return h


def raster_to_columns(x):
    n, l, c = x.shape
    rows = l // GRID_W
    return x.reshape(n, rows, GRID_W, c).transpose(0, 2, 1, 3).reshape(n, l, c)


def columns_to_raster(x):
    n, l, c = x.shape
    rows = l // GRID_W
    return x.reshape(n, GRID_W, rows, c).transpose(0, 2, 1, 3).reshape(n, l, c)


def rglru_branch(u, pl, h0):
    bn, l, _ = u.shape
    xc = depthwise_conv(u, pl['rg_conv_w'], pl['rg_conv_b'], RG_CONV // 2, RG_CONV - 1 - RG_CONV // 2)
    xh = xc.reshape(bn, l, RG_HEADS, RG_HEAD_DIM)
    def block_diag_gate(w, bias):
        z = jnp.einsum('blhi,dhij->dblhj', xh, w).reshape(2, bn, l, RG_WIDTH)
        return jax.nn.sigmoid((z + bias[:, None, None, :]).astype(F32))
    r = block_diag_gate(pl['rg_wa'], pl['rg_ba'])
    i = block_diag_gate(pl['rg_wx'], pl['rg_bx'])
    log_a = -RG_C * jax.nn.softplus(-pl['rg_lambda'].astype(F32))[:, None, None, :] * r
    a = jnp.exp(log_a)
    b = jnp.sqrt(-jnp.expm1(2.0 * log_a)) * i * xc.astype(F32)[None]
    h_f = linear_scan(a[0], b[0], None if h0 is None else h0[0], False)
    h_b = linear_scan(a[1], b[1], None if h0 is None else h0[1], True)
    y = (h_f + h_b).astype(u.dtype)
    return y @ pl['rg_out'], (h_f[:, -1], h_b[:, 0])


def conv_branch(v, pl, grid):
    bn, l, _ = v.shape
    z = v[..., :CV_WIDTH] * jax.nn.sigmoid(v[..., CV_WIDTH:])
    if grid:
        z = z.reshape(bn * (l // GRID_W), GRID_W, CV_WIDTH)
    z = depthwise_conv(z, pl['cv_dw'], pl['cv_db'], CV_KERNEL // 2, CV_KERNEL // 2)
    z = z.reshape(bn, l, CV_WIDTH)
    z = jax.nn.silu(layernorm(z, pl['cv_ln_g'], pl['cv_ln_b']))
    return z @ pl['cv_out']


def s5_direction(ug, lam_re, lam_im, log_dt, b_re, b_im, c_re, c_im, h0, reverse):
    lam = lax.complex(lam_re.astype(F32), lam_im.astype(F32))
    dt = jnp.exp(log_dt.astype(F32))[:, None]
    a_bar = jnp.exp(lam * dt)
    b_bar = ((a_bar - 1.0) / lam)[..., None] * lax.complex(b_re.astype(F32), b_im.astype(F32))
    bu = jnp.einsum('blgs,gps->blgp', ug, b_bar)
    h = linear_scan(jnp.broadcast_to(a_bar, bu.shape), bu, h0, reverse)
    c_mat = lax.complex(c_re.astype(F32), c_im.astype(F32))
    y = jnp.real(jnp.einsum('blgp,gsp->blgs', h, c_mat))
    final = h[:, 0] if reverse else h[:, -1]
    return y, final


def s5_branch(u, pl, h0, grid):
    bn, l, _ = u.shape
    us = raster_to_columns(u) if grid else u
    usf = us.astype(F32)
    ug = usf.reshape(bn, l, S5_GROUPS, S5_GROUP)
    y = pl['s5_d'].astype(F32) * usf
    finals = []
    for d, rev in enumerate((False, True)):
        yd, fd = s5_direction(ug, pl['s5_lambda_re'][d], pl['s5_lambda_im'][d], pl['s5_log_dt'][d], pl['s5_b_re'][d], pl['s5_b_im'][d], pl['s5_c_re'][d], pl['s5_c_im'][d], None if h0 is None else h0[d], rev)
        y = y + yd.reshape(bn, l, S5_WIDTH)
        finals.append(fd)
    if grid:
        y = columns_to_raster(y)
    z = y.astype(u.dtype) @ pl['s5_glu']
    return z[..., :D_MODEL] * jax.nn.sigmoid(z[..., D_MODEL:]), finals


def mixer(h, pl, h0_rg, h0_s5, grid):
    proj = h @ pl['w_in']
    o1 = RG_WIDTH
    o2 = o1 + 2 * CV_WIDTH
    o3 = o2 + S5_WIDTH
    br_rg, fin_rg = rglru_branch(proj[..., :o1], pl, h0_rg)
    br_cv = conv_branch(proj[..., o1:o2], pl, grid)
    br_s5, fin_s5 = s5_branch(proj[..., o2:o3], pl, h0_s5, grid)
    g_rg, g_cv, g_s5 = jnp.split(jax.nn.sigmoid(proj[..., o3:] + pl['b_gate']), N_BRANCH, axis=-1)
    merged = g_rg * br_rg + g_cv * br_cv + g_s5 * br_s5
    return merged @ pl['w_out'], fin_rg, fin_s5


def moe(h, pl):
    bn, l, d = h.shape
    t = h.reshape(bn * l, d)
    n_tok = t.shape[0]
    s = jax.nn.sigmoid((t @ pl['router_w']).astype(F32))
    choice = s + pl['router_b'].astype(F32)
    grp = choice.reshape(n_tok, N_ROUTE_GROUPS, N_EXPERTS // N_ROUTE_GROUPS)
    grp_score = jnp.sum(lax.top_k(grp, 2)[0], axis=-1)
    _, gidx = lax.top_k(grp_score, TOPK_GROUPS)
    gmask = jnp.any(gidx[..., None] == jnp.arange(N_ROUTE_GROUPS), axis=-2)
    emask = jnp.repeat(gmask, N_EXPERTS // N_ROUTE_GROUPS, axis=-1)
    _, eidx = lax.top_k(jnp.where(emask, choice, -jnp.inf), TOP_K)
    w = jnp.take_along_axis(s, eidx, axis=-1)
    w = ROUTE_SCALE * w / jnp.sum(w, axis=-1, keepdims=True)
    combine = jnp.sum(jnp.where(eidx[..., None] == jnp.arange(N_EXPERTS), w[..., None], 0.0), axis=1)
    w1, w3, w2 = pl['exp_w1'], pl['exp_w3'], pl['exp_w2']
    def expert(e, acc):
        y = (jax.nn.silu(t @ w1[e]) * (t @ w3[e])) @ w2[e]
        return acc + combine[:, e, None] * y.astype(F32)
    routed = lax.fori_loop(0, N_EXPERTS, expert, jnp.zeros((n_tok, d), F32))
    shared = (jax.nn.silu(t @ pl['sh_w1']) * (t @ pl['sh_w3'])) @ pl['sh_w2']
    return (routed + shared.astype(F32)).astype(h.dtype).reshape(bn, l, d)


def layer(x, cond, pl, h0_rg, h0_s5, grid):
    mod = (jax.nn.silu(cond) @ pl['ada_w'] + pl['ada_b'])[:, None, :]
    shift1, scale1, gate1, shift2, scale2, gate2 = jnp.split(mod, 6, axis=-1)
    hn = rmsnorm(x, pl['norm_mix']) * (1.0 + scale1) + shift1
    mo, fin_rg, fin_s5 = mixer(hn, pl, h0_rg, h0_s5, grid)
    x = x + gate1 * mo
    hn = rmsnorm(x, pl['norm_ffn']) * (1.0 + scale2) + shift2
    x = x + gate2 * moe(hn, pl)
    return x, fin_rg, fin_s5


def setup_inputs(seed: int = 0) -> dict:
    key = jax.random.key(seed)
    ks = iter(jax.random.split(key, 64))
    def nrm(shape, scale):
        return jax.random.normal(next(ks), shape, F32) * scale
    def gain(shape):
        return 1.0 + nrm(shape, 0.01)
    dd = (DEPTH, 2)
    x_prompt = nrm((BATCH, SEQ, D_MODEL), 1.0)
    x_sample = nrm((DEC_BATCH, DEC_SEQ, D_MODEL), 1.0)
    state_rglru = nrm((DEC_BATCH, DEPTH, 2, RG_WIDTH), 0.5)
    state_s5 = nrm((DEC_BATCH, DEPTH, 2, S5_GROUPS, S5_STATE, 2), 0.1)
    c = nrm((DEC_BATCH, D_MODEL), 1.0)
    c_ctx = nrm((D_MODEL,), 1.0)
    a0 = jax.random.uniform(next(ks), (DEPTH, 2, RG_WIDTH), F32, 0.9, 0.999)
    p = a0 ** (1.0 / RG_C)
    rg_lambda = jnp.log(p) - jnp.log1p(-p)
    lam_im0 = math.pi * jnp.arange(S5_STATE, dtype=F32)
    return {
        'x_prompt': x_prompt, 'x_sample': x_sample,
        'state_rglru': state_rglru, 'state_s5': state_s5,
        'c': c, 'c_ctx': c_ctx,
        'norm_mix': gain((DEPTH, D_MODEL)), 'norm_ffn': gain((DEPTH, D_MODEL)),
        'ada_w': nrm((DEPTH, D_MODEL, 6 * D_MODEL), 0.5 * D_MODEL ** -0.5),
        'ada_b': nrm((DEPTH, 6 * D_MODEL), 0.01),
        'w_in': nrm((DEPTH, D_MODEL, N_IN), D_MODEL ** -0.5),
        'b_gate': nrm((DEPTH, N_BRANCH * D_MODEL), 0.01),
        'rg_conv_w': nrm((DEPTH, RG_CONV, RG_WIDTH), RG_CONV ** -0.5),
        'rg_conv_b': nrm((DEPTH, RG_WIDTH), 0.01),
        'rg_wa': nrm(dd + (RG_HEADS, RG_HEAD_DIM, RG_HEAD_DIM), RG_HEAD_DIM ** -0.5),
        'rg_ba': nrm(dd + (RG_WIDTH,), 0.01),
        'rg_wx': nrm(dd + (RG_HEADS, RG_HEAD_DIM, RG_HEAD_DIM), RG_HEAD_DIM ** -0.5),
        'rg_bx': nrm(dd + (RG_WIDTH,), 0.01),
        'rg_lambda': rg_lambda,
        'rg_out': nrm((DEPTH, RG_WIDTH, D_MODEL), RG_WIDTH ** -0.5),
        'cv_dw': nrm((DEPTH, CV_KERNEL, CV_WIDTH), CV_KERNEL ** -0.5),
        'cv_db': nrm((DEPTH, CV_WIDTH), 0.01),
        'cv_ln_g': gain((DEPTH, CV_WIDTH)),
        'cv_ln_b': nrm((DEPTH, CV_WIDTH), 0.01),
        'cv_out': nrm((DEPTH, CV_WIDTH, D_MODEL), CV_WIDTH ** -0.5),
        's5_lambda_re': -0.5 + nrm(dd + (S5_GROUPS, S5_STATE), 0.01),
        's5_lambda_im': lam_im0 + nrm(dd + (S5_GROUPS, S5_STATE), 0.01),
        's5_log_dt': jax.random.uniform(next(ks), dd + (S5_GROUPS,), F32, math.log(1e-3), math.log(1e-1)),
        's5_b_re': nrm(dd + (S5_GROUPS, S5_STATE, S5_GROUP), (2 * S5_GROUP) ** -0.5),
        's5_b_im': nrm(dd + (S5_GROUPS, S5_STATE, S5_GROUP), (2 * S5_GROUP) ** -0.5),
        's5_c_re': nrm(dd + (S5_GROUPS, S5_GROUP, S5_STATE), (2 * S5_STATE) ** -0.5),
        's5_c_im': nrm(dd + (S5_GROUPS, S5_GROUP, S5_STATE), (2 * S5_STATE) ** -0.5),
        's5_d': nrm((DEPTH, S5_WIDTH), 1.0),
        's5_glu': nrm((DEPTH, S5_WIDTH, 2 * D_MODEL), S5_WIDTH ** -0.5),
        'w_out': nrm((DEPTH, D_MODEL, D_MODEL), D_MODEL ** -0.5),
        'router_w': nrm((DEPTH, D_MODEL, N_EXPERTS), D_MODEL ** -0.5),
        'router_b': nrm((DEPTH, N_EXPERTS), 0.01),
        'exp_w1': nrm((DEPTH, N_EXPERTS, D_MODEL, EXPERT_FF), D_MODEL ** -0.5),
        'exp_w3': nrm((DEPTH, N_EXPERTS, D_MODEL, EXPERT_FF), D_MODEL ** -0.5),
        'exp_w2': nrm((DEPTH, N_EXPERTS, EXPERT_FF, D_MODEL), EXPERT_FF ** -0.5),
        'sh_w1': nrm((DEPTH, D_MODEL, SHARED_FF), D_MODEL ** -0.5),
        'sh_w3': nrm((DEPTH, D_MODEL, SHARED_FF), D_MODEL ** -0.5),
        'sh_w2': nrm((DEPTH, SHARED_FF, D_MODEL), SHARED_FF ** -0.5),
        'norm_final': gain((D_MODEL,)),
    }


def reference(x_prompt, x_sample, state_rglru, state_s5, c, c_ctx, norm_mix, norm_ffn, ada_w, ada_b, w_in, b_gate, rg_conv_w, rg_conv_b, rg_wa, rg_ba, rg_wx, rg_bx, rg_lambda, rg_out, cv_dw, cv_db, cv_ln_g, cv_ln_b, cv_out, s5_lambda_re, s5_lambda_im, s5_log_dt, s5_b_re, s5_b_im, s5_c_re, s5_c_im, s5_d, s5_glu, w_out, router_w, router_b, exp_w1, exp_w3, exp_w2, sh_w1, sh_w3, sh_w2, norm_final):
    stacked = dict(norm_mix=norm_mix, norm_ffn=norm_ffn, ada_w=ada_w, ada_b=ada_b, w_in=w_in, b_gate=b_gate, rg_conv_w=rg_conv_w, rg_conv_b=rg_conv_b, rg_wa=rg_wa, rg_ba=rg_ba, rg_wx=rg_wx, rg_bx=rg_bx, rg_lambda=rg_lambda, rg_out=rg_out, cv_dw=cv_dw, cv_db=cv_db, cv_ln_g=cv_ln_g, cv_ln_b=cv_ln_b, cv_out=cv_out, s5_lambda_re=s5_lambda_re, s5_lambda_im=s5_lambda_im, s5_log_dt=s5_log_dt, s5_b_re=s5_b_re, s5_b_im=s5_b_im, s5_c_re=s5_c_re, s5_c_im=s5_c_im, s5_d=s5_d, s5_glu=s5_glu, w_out=w_out, router_w=router_w, router_b=router_b, exp_w1=exp_w1, exp_w3=exp_w3, exp_w2=exp_w2, sh_w1=sh_w1, sh_w3=sh_w3, sh_w2=sh_w2)
    x = x_prompt
    rg_states, s5_states = [], []
    for l in range(DEPTH):
        pl = {k: v[l] for k, v in stacked.items()}
        x, fin_rg, fin_s5 = layer(x, c_ctx[None, :], pl, None, None, False)
        rg_states.append(jnp.stack(fin_rg, axis=1))
        s5c = jnp.stack(fin_s5, axis=1)
        s5_states.append(jnp.stack([jnp.real(s5c), jnp.imag(s5c)], axis=-1))
    y_prompt = rmsnorm(x, norm_final)
    new_state_rglru = jnp.stack(rg_states, axis=1).astype(x_prompt.dtype)
    new_state_s5 = jnp.stack(s5_states, axis=1).astype(x_prompt.dtype)
    x = x_sample
    for l in range(DEPTH):
        pl = {k: v[l] for k, v in stacked.items()}
        h0_rg = (state_rglru[:, l, 0].astype(F32), state_rglru[:, l, 1].astype(F32))
        st = state_s5[:, l].astype(F32)
        s5c = lax.complex(st[..., 0], st[..., 1])
        x, _, _ = layer(x, c, pl, h0_rg, (s5c[:, 0], s5c[:, 1]), True)
    y_sample = rmsnorm(x, norm_final)
    return (y_prompt, y_sample, new_state_rglru, new_state_s5)
```

```python
import functools
import math

import jax
import jax.numpy as jnp
from jax import lax
from jax.experimental import pallas as pl
from jax.experimental.pallas import tpu as pltpu

F32 = jnp.float32
MXU_DTYPE = jnp.bfloat16

SUBLANES = 8
LANES = 128
VMEM_LIMIT_BYTES = 56 * 1024 * 1024

GRID_W = 64
RG_C = 8.0
S5_GROUP = 16
TOP_K = 8
N_ROUTE_GROUPS = 8
TOPK_GROUPS = 4
ROUTE_SCALE = 2.5
EPS = 1e-6

CHUNK_TILES = 64
CONV_HALO_TILES = 16


def _params(n_grid_dims):
    return pltpu.CompilerParams(
        dimension_semantics=("arbitrary",) * n_grid_dims,
        vmem_limit_bytes=VMEM_LIMIT_BYTES,
    )


def _mm(a, b):
    return jnp.dot(a.astype(MXU_DTYPE), b.astype(MXU_DTYPE), preferred_element_type=F32)


def _sigmoid(x):
    return 1.0 / (1.0 + jnp.exp(-x))


def _silu(x):
    return x * _sigmoid(x)


def _tile_bcast_mul_add(y, scale, shift):
    rows, d = y.shape
    y3 = y.reshape(rows // SUBLANES, SUBLANES, d)
    return (y3 * scale[None] + shift[None]).reshape(rows, d)


def _gated_residual(x, gate, y):
    rows, d = y.shape
    return (y.reshape(rows // SUBLANES, SUBLANES, d) * gate[None]).reshape(rows, d) + x


def _norm_mod(x, g, scale, shift):
    ms = jnp.mean(x * x, axis=-1, keepdims=True)
    y = x * lax.rsqrt(ms + EPS) * g
    return _tile_bcast_mul_add(y, 1.0 + scale, shift)


def _ada_kernel(c_ref, w_ref, b_ref, o_ref):
    o_ref[0] = _mm(_silu(c_ref[...]), w_ref[0]) + b_ref[0]


def _ada(cond, ada_w, ada_b):
    depth, d, n = ada_w.shape
    tn = math.gcd(n, 1024)
    rows = cond.shape[0]
    return pl.pallas_call(
        _ada_kernel,
        grid=(depth, n // tn),
        in_specs=[
            pl.BlockSpec((rows, d), lambda l, j: (0, 0)),
            pl.BlockSpec((1, d, tn), lambda l, j: (l, 0, j)),
            pl.BlockSpec((1, 1, tn), lambda l, j: (l, 0, j)),
        ],
        out_specs=pl.BlockSpec((1, rows, tn), lambda l, j: (l, 0, j)),
        out_shape=jax.ShapeDtypeStruct((depth, rows, n), F32),
        compiler_params=_params(2),
        name="ada",
    )(cond, ada_w, ada_b.reshape(depth, 1, n))


def _win_kernel(x_ref, g_ref, sc_ref, sh_ref, wa_ref, wb_ref, bg_ref,
                urg_ref, z_ref, us5_ref, gt_ref, hn_ref, *, n_rg, n_cv, n_s5):
    j = pl.program_id(1)

    @pl.when(j == 0)
    def _():
        hn_ref[...] = _norm_mod(x_ref[...], g_ref[...], sc_ref[0], sh_ref[0]).astype(hn_ref.dtype)

    hn = hn_ref[...]
    a = jnp.dot(hn, wa_ref[...], preferred_element_type=F32)

    @pl.when(j < n_rg)
    def _():
        urg_ref[...] = a

    @pl.when((j >= n_rg) & (j < n_rg + n_cv))
    def _():
        b = jnp.dot(hn, wb_ref[...], preferred_element_type=F32)
        z_ref[...] = a * _sigmoid(b)

    @pl.when((j >= n_rg + n_cv) & (j < n_rg + n_cv + n_s5))
    def _():
        us5_ref[...] = a

    @pl.when(j >= n_rg + n_cv + n_s5)
    def _():
        gt_ref[...] = _sigmoid(a + bg_ref[...]).astype(gt_ref.dtype)


def _win(x, g, scale, shift, w_in, b_gate, *, rg, cv, s5, n_sample_rows, tm, tn):
    t, d = x.shape
    n_gate = b_gate.shape[-1]
    n_rg, n_cv, n_s5, n_g = rg // tn, cv // tn, s5 // tn, n_gate // tn
    ns = n_sample_rows // tm
    unit = lambda i, j: ((i >= ns).astype(jnp.int32), 0, 0)
    clip = lambda v, n: jnp.clip(v, 0, n - 1)
    kern = functools.partial(_win_kernel, n_rg=n_rg, n_cv=n_cv, n_s5=n_s5)
    return pl.pallas_call(
        kern,
        grid=(t // tm, n_rg + n_cv + n_s5 + n_g),
        in_specs=[
            pl.BlockSpec((tm, d), lambda i, j: (i, 0)),
            pl.BlockSpec((1, d), lambda i, j: (0, 0)),
            pl.BlockSpec((1, SUBLANES, d), unit),
            pl.BlockSpec((1, SUBLANES, d), unit),
            pl.BlockSpec((d, tn), lambda i, j: (0, jnp.where(j < n_rg + n_cv, j, j + n_cv))),
            pl.BlockSpec((d, tn), lambda i, j: (0, n_rg + n_cv + clip(j - n_rg, n_cv))),
            pl.BlockSpec((1, tn), lambda i, j: (0, clip(j - (n_rg + n_cv + n_s5), n_g))),
        ],
        out_specs=[
            pl.BlockSpec((tm, tn), lambda i, j: (i, clip(j, n_rg))),
            pl.BlockSpec((tm, tn), lambda i, j: (i, clip(j - n_rg, n_cv))),
            pl.BlockSpec((tm, tn), lambda i, j: (i, clip(j - n_rg - n_cv, n_s5))),
            pl.BlockSpec((tm, tn), lambda i, j: (i, clip(j - n_rg - n_cv - n_s5, n_g))),
        ],
        out_shape=[
            jax.ShapeDtypeStruct((t, rg), F32),
            jax.ShapeDtypeStruct((t, cv), F32),
            jax.ShapeDtypeStruct((t, s5), F32),
            jax.ShapeDtypeStruct((t, n_gate), MXU_DTYPE),
        ],
        scratch_shapes=[pltpu.VMEM((tm, d), MXU_DTYPE)],
        compiler_params=_params(2),
        name="win",
    )(x, g, scale, shift, w_in, w_in, b_gate)


def _rg_kernel(u_ref, cw_ref, cb_ref, wa_ref, wx_ref, ba_ref, bx_ref, lam_ref, h0_ref,
               y_ref, fin_ref, ext_ref, a_ref, b_ref, *, units, ch, kconv):
    rows = ch * SUBLANES
    pad_lo = (kconv // 2) * SUBLANES
    pad_hi = (kconv - 1 - kconv // 2) * SUBLANES
    t_rows = u_ref.shape[0]
    for d in (0, 1):
        neg_lam = -lam_ref[d:d + 1, :]
        softplus = jnp.maximum(neg_lam, 0.0) + jnp.log1p(jnp.exp(-jnp.abs(neg_lam)))
        coef = -RG_C * softplus
        wa = wa_ref[d, 0]
        wx = wx_ref[d, 0]
        ba = ba_ref[d:d + 1, :]
        bx = bx_ref[d:d + 1, :]
        for ui, (t0, nt, _, has_h0) in enumerate(units):
            nch = nt // ch

            def chunk_body(ci, h, d=d, t0=t0, nch=nch, coef=coef, wa=wa, wx=wx, ba=ba, bx=bx):
                c = ci if d == 0 else nch - 1 - ci
                r0 = pl.multiple_of((t0 + c * ch) * SUBLANES, SUBLANES)
                lo_start = pl.multiple_of(jnp.maximum(r0 - pad_lo, 0), SUBLANES)
                hi_start = pl.multiple_of(jnp.minimum(r0 + rows, t_rows - pad_hi), SUBLANES)
                ext_ref[0:pad_lo, :] = jnp.where(c > 0, u_ref[pl.ds(lo_start, pad_lo), :], 0.0)
                ext_ref[pad_lo:pad_lo + rows, :] = u_ref[pl.ds(r0, rows), :]
                ext_ref[pad_lo + rows:pad_lo + rows + pad_hi, :] = jnp.where(
                    c < nch - 1, u_ref[pl.ds(hi_start, pad_hi), :], 0.0)
                xc = cb_ref[...] + cw_ref[0:1, :] * ext_ref[0:rows, :]
                for k in range(1, kconv):
                    xc = xc + cw_ref[k:k + 1, :] * ext_ref[k * SUBLANES:k * SUBLANES + rows, :]
                r = _sigmoid(_mm(xc, wa) + ba)
                i = _sigmoid(_mm(xc, wx) + bx)
                a = jnp.exp(coef * r)
                a_ref[...] = a
                b_ref[...] = jnp.sqrt(1.0 - a * a) * i * xc

                def step(s, h):
                    tt = s if d == 0 else ch - 1 - s
                    o = pl.multiple_of(tt * SUBLANES, SUBLANES)
                    h = a_ref[pl.ds(o, SUBLANES), :] * h + b_ref[pl.ds(o, SUBLANES), :]
                    dst = pl.ds(pl.multiple_of(r0 + o, SUBLANES), SUBLANES)
                    if d == 0:
                        y_ref[dst, :] = h
                    else:
                        y_ref[dst, :] = y_ref[dst, :] + h
                    return h

                return lax.fori_loop(0, ch, step, h, unroll=8)

            h_init = h0_ref[d] if has_h0 else jnp.zeros((SUBLANES, LANES), F32)
            fin_ref[ui, d] = lax.fori_loop(0, nch, chunk_body, h_init)


def _rglru(u, conv_w, conv_b, wa, wx, ba, bx, lam, h0, *, units):
    t, rg = u.shape
    heads = rg // LANES
    kconv = conv_w.shape[0]
    n_units = len(units)
    rows = CHUNK_TILES * SUBLANES
    kern = functools.partial(_rg_kernel, units=units, ch=CHUNK_TILES, kconv=kconv)
    col = lambda h: (0, h)
    return pl.pallas_call(
        kern,
        grid=(heads,),
        in_specs=[
            pl.BlockSpec((t, LANES), col),
            pl.BlockSpec((kconv, LANES), col),
            pl.BlockSpec((1, LANES), col),
            pl.BlockSpec((2, 1, LANES, LANES), lambda h: (0, h, 0, 0)),
            pl.BlockSpec((2, 1, LANES, LANES), lambda h: (0, h, 0, 0)),
            pl.BlockSpec((2, LANES), col),
            pl.BlockSpec((2, LANES), col),
            pl.BlockSpec((2, LANES), col),
            pl.BlockSpec((2, SUBLANES, LANES), lambda h: (0, 0, h)),
        ],
        out_specs=[
            pl.BlockSpec((t, LANES), col),
            pl.BlockSpec((n_units, 2, SUBLANES, LANES), lambda h: (0, 0, 0, h)),
        ],
        out_shape=[
            jax.ShapeDtypeStruct((t, rg), F32),
            jax.ShapeDtypeStruct((n_units, 2, SUBLANES, rg), F32),
        ],
        scratch_shapes=[
            pltpu.VMEM((rows + (kconv - 1) * SUBLANES, LANES), F32),
            pltpu.VMEM((rows, LANES), F32),
            pltpu.VMEM((rows, LANES), F32),
        ],
        compiler_params=_params(1),
        name="rglru",
    )(u, conv_w, conv_b, wa, wx, ba, bx, lam, h0)


def _cv_kernel(zp_ref, zc_ref, zn_ref, w_ref, db_ref, lg_ref, lb_ref, o_ref, ext_ref, acc_ref,
               *, n_sample_chunks, ctx_unit_chunks, kc):
    i = pl.program_id(0)
    rows, c = zc_ref.shape
    hr = CONV_HALO_TILES * SUBLANES
    group = SUBLANES
    is_ctx = i >= n_sample_chunks
    cpos = (i - n_sample_chunks) % ctx_unit_chunks
    lo_ok = is_ctx & (cpos > 0)
    hi_ok = is_ctx & (cpos < ctx_unit_chunks - 1)
    ext_ref[0:hr, :] = jnp.where(lo_ok, zp_ref[rows - hr:rows, :], 0.0)
    ext_ref[hr:hr + rows, :] = zc_ref[...]
    ext_ref[hr + rows:hr + rows + hr, :] = jnp.where(hi_ok, zn_ref[0:hr, :], 0.0)
    first_tap_tile = CONV_HALO_TILES - kc // 2

    def lane_body(lb, carry):
        l0 = pl.multiple_of(lb * LANES, LANES)

        def grp_body(g, carry):
            acc = jnp.zeros((group, SUBLANES, LANES), F32)
            for k in range(kc):
                src = pl.multiple_of((g * group + k + first_tap_tile) * SUBLANES, SUBLANES)
                e = ext_ref[pl.ds(src, group * SUBLANES), pl.ds(l0, LANES)]
                acc = acc + e.reshape(group, SUBLANES, LANES) * w_ref[k, :, pl.ds(l0, LANES)][None]
            dst = pl.multiple_of(g * group * SUBLANES, SUBLANES)
            acc_ref[pl.ds(dst, group * SUBLANES), pl.ds(l0, LANES)] = acc.reshape(group * SUBLANES, LANES)
            return carry

        return lax.fori_loop(0, rows // (group * SUBLANES), grp_body, carry)

    lax.fori_loop(0, c // LANES, lane_body, 0)
    z = acc_ref[...] + db_ref[...]
    mu = jnp.mean(z, axis=-1, keepdims=True)
    zc = z - mu
    var = jnp.mean(zc * zc, axis=-1, keepdims=True)
    y = zc * lax.rsqrt(var + EPS) * lg_ref[...] + lb_ref[...]
    o_ref[...] = _silu(y).astype(o_ref.dtype)


def _conv_branch(z, dw_tiles, db, ln_g, ln_b, *, n_sample_chunks, ctx_unit_chunks):
    t, c = z.shape
    kc = dw_tiles.shape[0]
    rows = CHUNK_TILES * SUBLANES
    n = t // rows
    kern = functools.partial(_cv_kernel, n_sample_chunks=n_sample_chunks,
                             ctx_unit_chunks=ctx_unit_chunks, kc=kc)
    vec = pl.BlockSpec((1, c), lambda i: (0, 0))
    return pl.pallas_call(
        kern,
        grid=(n,),
        in_specs=[
            pl.BlockSpec((rows, c), lambda i: (jnp.maximum(i - 1, 0), 0)),
            pl.BlockSpec((rows, c), lambda i: (i, 0)),
            pl.BlockSpec((rows, c), lambda i: (jnp.minimum(i + 1, n - 1), 0)),
            pl.BlockSpec((kc, SUBLANES, c), lambda i: (0, 0, 0)),
            vec, vec, vec,
        ],
        out_specs=pl.BlockSpec((rows, c), lambda i: (i, 0)),
        out_shape=jax.ShapeDtypeStruct((t, c), MXU_DTYPE),
        scratch_shapes=[
            pltpu.VMEM((rows + 2 * CONV_HALO_TILES * SUBLANES, c), F32),
            pltpu.VMEM((rows, c), F32),
        ],
        compiler_params=_params(1),
        name="conv",
    )(z, z, z, dw_tiles, db, ln_g, ln_b)


def _s5_kernel(u_ref, dsk_ref, are_ref, aim_ref, bm_ref, cm_ref, h0_ref,
               y_ref, fin_ref, lhs_ref, hs_ref, yc_ref, *, units, tc, grid_rows):
    ns = are_ref.shape[-1]

    def tile_of(unit, c, s):
        t0, _, is_grid, _ = unit
        if is_grid:
            return t0 + (s % grid_rows) * GRID_W + c * (tc // grid_rows) + s // grid_rows
        return t0 + c * tc + s

    for d in (0, 1):
        a_re = jnp.broadcast_to(are_ref[d, 0], (SUBLANES, ns))
        a_im = jnp.broadcast_to(aim_ref[d, 0], (SUBLANES, ns))
        for ui, unit in enumerate(units):
            nch = unit[1] // tc

            def chunk_body(ci, h, d=d, unit=unit, nch=nch, a_re=a_re, a_im=a_im):
                c = ci if d == 0 else nch - 1 - ci
                for s in range(tc):
                    r = pl.multiple_of(tile_of(unit, c, s) * SUBLANES, SUBLANES)
                    lhs_ref[s * SUBLANES:(s + 1) * SUBLANES, :] = u_ref[pl.ds(r, SUBLANES), :]
                hs_ref[...] = _mm(lhs_ref[...], bm_ref[d, 0])

                def step(si, h):
                    s = si if d == 0 else tc - 1 - si
                    o = pl.ds(pl.multiple_of(s * SUBLANES, SUBLANES), SUBLANES)
                    h_re, h_im = h
                    n_re = a_re * h_re - a_im * h_im + hs_ref[o, 0:ns]
                    n_im = a_re * h_im + a_im * h_re + hs_ref[o, ns:2 * ns]
                    hs_ref[o, 0:ns] = n_re
                    hs_ref[o, ns:2 * ns] = n_im
                    return n_re, n_im

                h = lax.fori_loop(0, tc, step, h, unroll=4)
                yc_ref[...] = _mm(hs_ref[...], cm_ref[d, 0])
                for s in range(tc):
                    r = pl.multiple_of(tile_of(unit, c, s) * SUBLANES, SUBLANES)
                    sl = slice(s * SUBLANES, (s + 1) * SUBLANES)
                    if d == 0:
                        y_ref[pl.ds(r, SUBLANES), :] = dsk_ref[...] * lhs_ref[sl, :] + yc_ref[sl, :]
                    else:
                        y_ref[pl.ds(r, SUBLANES), :] = y_ref[pl.ds(r, SUBLANES), :] + yc_ref[sl, :]
                return h

            if unit[3]:
                h_init = (h0_ref[d, 0, :, 0:ns], h0_ref[d, 0, :, ns:2 * ns])
            else:
                h_init = (jnp.zeros((SUBLANES, ns), F32), jnp.zeros((SUBLANES, ns), F32))
            f_re, f_im = lax.fori_loop(0, nch, chunk_body, h_init)
            fin_ref[ui, d, 0, :, 0:ns] = f_re
            fin_ref[ui, d, 0, :, ns:2 * ns] = f_im


def _s5(u, dskip, a_re, a_im, bmat, cmat, h0, *, units, tc, grid_rows):
    t, width = u.shape
    nb = width // LANES
    ns = a_re.shape[-1]
    n_units = len(units)
    kern = functools.partial(_s5_kernel, units=units, tc=tc, grid_rows=grid_rows)
    blk4 = lambda *shape: pl.BlockSpec((2, 1) + shape, lambda j: (0, j, 0, 0))
    return pl.pallas_call(
        kern,
        grid=(nb,),
        in_specs=[
            pl.BlockSpec((t, LANES), lambda j: (0, j)),
            pl.BlockSpec((1, LANES), lambda j: (0, j)),
            blk4(1, ns), blk4(1, ns),
            blk4(LANES, 2 * ns), blk4(2 * ns, LANES),
            blk4(SUBLANES, 2 * ns),
        ],
        out_specs=[
            pl.BlockSpec((t, LANES), lambda j: (0, j)),
            pl.BlockSpec((n_units, 2, 1, SUBLANES, 2 * ns), lambda j: (0, 0, j, 0, 0)),
        ],
        out_shape=[
            jax.ShapeDtypeStruct((t, width), F32),
            jax.ShapeDtypeStruct((n_units, 2, nb, SUBLANES, 2 * ns), F32),
        ],
        scratch_shapes=[
            pltpu.VMEM((tc * SUBLANES, LANES), F32),
            pltpu.VMEM((tc * SUBLANES, 2 * ns), F32),
            pltpu.VMEM((tc * SUBLANES, LANES), F32),
        ],
        compiler_params=_params(1),
        name="s5",
    )(u, dskip, a_re, a_im, bmat, cmat, h0)


def _merge_kernel(yrg_ref, zc_ref, ys5_ref, g0_ref, g1_ref, g2_ref, wrg_ref, wcv_ref, wsv_ref,
                  wsg_ref, wo_ref, x_ref, gate_ref, o_ref, acc_ref, rgb_ref, s5b_ref):
    j = pl.program_id(1)

    @pl.when(j == 0)
    def _():
        acc_ref[...] = jnp.zeros_like(acc_ref)
        rgb_ref[...] = yrg_ref[...].astype(rgb_ref.dtype)
        s5b_ref[...] = ys5_ref[...].astype(s5b_ref.dtype)

    dot = lambda a, b: jnp.dot(a, b, preferred_element_type=F32)
    br_rg = dot(rgb_ref[...], wrg_ref[...])
    br_cv = dot(zc_ref[...], wcv_ref[...])
    s5b = s5b_ref[...]
    br_s5 = dot(s5b, wsv_ref[...]) * _sigmoid(dot(s5b, wsg_ref[...]))
    merged = (g0_ref[...].astype(F32) * br_rg + g1_ref[...].astype(F32) * br_cv
              + g2_ref[...].astype(F32) * br_s5)
    acc_ref[...] += dot(merged.astype(MXU_DTYPE), wo_ref[...])

    @pl.when(j == pl.num_programs(1) - 1)
    def _():
        o_ref[...] = _gated_residual(x_ref[...], gate_ref[0], acc_ref[...])


def _merge(y_rg, zc, y_s5, gates, rg_out, cv_out, s5_glu, w_out, x, gate, *, n_sample_rows, tm, tk):
    t, d = x.shape
    rg, cv, s5 = y_rg.shape[1], zc.shape[1], y_s5.shape[1]
    nk = d // tk
    ns = n_sample_rows // tm
    row = lambda w: pl.BlockSpec((tm, w), lambda i, j: (i, 0))
    gspec = lambda k: pl.BlockSpec((tm, tk), lambda i, j: (i, k * nk + j))
    wcol = lambda rows, off: pl.BlockSpec((rows, tk), lambda i, j: (0, off + j))
    return pl.pallas_call(
        _merge_kernel,
        grid=(t // tm, nk),
        in_specs=[
            row(rg), row(cv), row(s5),
            gspec(0), gspec(1), gspec(2),
            wcol(rg, 0), wcol(cv, 0), wcol(s5, 0), wcol(s5, nk),
            pl.BlockSpec((tk, d), lambda i, j: (j, 0)),
            row(d),
            pl.BlockSpec((1, SUBLANES, d), lambda i, j: ((i >= ns).astype(jnp.int32), 0, 0)),
        ],
        out_specs=row(d),
        out_shape=jax.ShapeDtypeStruct((t, d), F32),
        scratch_shapes=[
            pltpu.VMEM((tm, d), F32),
            pltpu.VMEM((tm, rg), MXU_DTYPE),
            pltpu.VMEM((tm, s5), MXU_DTYPE),
        ],
        compiler_params=_params(2),
        name="merge",
    )(y_rg, zc, y_s5, gates, gates, gates, rg_out, cv_out, s5_glu, s5_glu, w_out, x, gate)


def _router_kernel(x_ref, g_ref, sc_ref, sh_ref, rw_ref, rb_ref, hn_ref, comb_ref):
    hn = _norm_mod(x_ref[...], g_ref[...], sc_ref[0], sh_ref[0])
    hn_ref[...] = hn.astype(hn_ref.dtype)
    logits = lax.dot_general(rw_ref[...], hn, (((1,), (1,)), ((), ())),
                             precision=lax.Precision.HIGHEST, preferred_element_type=F32)
    s = _sigmoid(logits)
    choice = s + rb_ref[...]
    n_exp, tm = choice.shape
    gsize = n_exp // N_ROUTE_GROUPS
    neg_inf = jnp.float32(-jnp.inf)
    c3 = choice.reshape(N_ROUTE_GROUPS, gsize, tm)
    sub = lax.broadcasted_iota(jnp.int32, c3.shape, 1)
    m1 = jnp.max(c3, axis=1, keepdims=True)
    i1 = jnp.min(jnp.where(c3 == m1, sub, gsize), axis=1, keepdims=True)
    m2 = jnp.max(jnp.where(sub == i1, neg_inf, c3), axis=1, keepdims=True)
    gscore = jnp.broadcast_to(m1 + m2, c3.shape)
    gidx = lax.broadcasted_iota(jnp.int32, c3.shape, 0)
    beaten = jnp.zeros(c3.shape, jnp.int32)
    for gp in range(N_ROUTE_GROUPS):
        other = gscore[gp:gp + 1]
        wins = (other > gscore) | ((other == gscore) & (gidx > gp))
        beaten = beaten + wins.astype(jnp.int32)
    masked = jnp.where(beaten < TOPK_GROUPS, c3, neg_inf).reshape(n_exp, tm)
    eidx = lax.broadcasted_iota(jnp.int32, masked.shape, 0)
    beaten = jnp.zeros(masked.shape, jnp.int32)
    for ep in range(n_exp):
        other = masked[ep:ep + 1, :]
        wins = (other > masked) | ((other == masked) & (eidx > ep))
        beaten = beaten + wins.astype(jnp.int32)
    w = jnp.where(beaten < TOP_K, s, 0.0)
    comb = ROUTE_SCALE * w / jnp.sum(w, axis=0, keepdims=True)
    pad = jnp.zeros((comb_ref.shape[1] - n_exp, tm), F32)
    comb_ref[...] = jnp.concatenate([comb, pad], axis=0).T


def _router(x, g, scale, shift, router_wt, router_b, *, n_sample_rows, tm):
    t, d = x.shape
    n_exp = router_wt.shape[0]
    ns = n_sample_rows // tm
    unit = lambda i: ((i >= ns).astype(jnp.int32), 0, 0)
    return pl.pallas_call(
        _router_kernel,
        grid=(t // tm,),
        in_specs=[
            pl.BlockSpec((tm, d), lambda i: (i, 0)),
            pl.BlockSpec((1, d), lambda i: (0, 0)),
            pl.BlockSpec((1, SUBLANES, d), unit),
            pl.BlockSpec((1, SUBLANES, d), unit),
            pl.BlockSpec((n_exp, d), lambda i: (0, 0)),
            pl.BlockSpec((n_exp, 1), lambda i: (0, 0)),
        ],
        out_specs=[
            pl.BlockSpec((tm, d), lambda i: (i, 0)),
            pl.BlockSpec((tm, LANES), lambda i: (i, 0)),
        ],
        out_shape=[
            jax.ShapeDtypeStruct((t, d), MXU_DTYPE),
            jax.ShapeDtypeStruct((t, LANES), F32),
        ],
        compiler_params=_params(1),
        name="router",
    )(x, g, scale, shift, router_wt, router_b)


def _moe_kernel(hn_ref, comb_ref, w1_ref, w3_ref, w2_ref, x_ref, gate_ref, o_ref, acc_ref, *, n_exp):
    e = pl.program_id(1)

    @pl.when(e == 0)
    def _():
        acc_ref[...] = jnp.zeros_like(acc_ref)

    hn = hn_ref[...]
    h1 = jnp.dot(hn, w1_ref[0], preferred_element_type=F32)
    h3 = jnp.dot(hn, w3_ref[0], preferred_element_type=F32)
    comb = comb_ref[...]
    lane = lax.broadcasted_iota(jnp.int32, comb.shape, 1)
    c = jnp.sum(jnp.where(lane == e, comb, 0.0), axis=-1, keepdims=True)
    c = jnp.where(e == n_exp, 1.0, c)
    hh = _silu(h1) * h3 * c
    acc_ref[...] += jnp.dot(hh.astype(MXU_DTYPE), w2_ref[0], preferred_element_type=F32)

    @pl.when(e == n_exp)
    def _():
        o_ref[...] = _gated_residual(x_ref[...], gate_ref[0], acc_ref[...])


def _moe(hn, comb, w1, w3, w2, x, gate, *, n_sample_rows, tm):
    t, d = x.shape
    n_all, _, f = w1.shape
    ns = n_sample_rows // tm
    return pl.pallas_call(
        functools.partial(_moe_kernel, n_exp=n_all - 1),
        grid=(t // tm, n_all),
        in_specs=[
            pl.BlockSpec((tm, d), lambda i, e: (i, 0)),
            pl.BlockSpec((tm, LANES), lambda i, e: (i, 0)),
            pl.BlockSpec((1, d, f), lambda i, e: (e, 0, 0)),
            pl.BlockSpec((1, d, f), lambda i, e: (e, 0, 0)),
            pl.BlockSpec((1, f, d), lambda i, e: (e, 0, 0)),
            pl.BlockSpec((tm, d), lambda i, e: (i, 0)),
            pl.BlockSpec((1, SUBLANES, d), lambda i, e: ((i >= ns).astype(jnp.int32), 0, 0)),
        ],
        out_specs=pl.BlockSpec((tm, d), lambda i, e: (i, 0)),
        out_shape=jax.ShapeDtypeStruct((t, d), F32),
        scratch_shapes=[pltpu.VMEM((tm, d), F32)],
        compiler_params=_params(2),
        name="moe",
    )(hn, comb, w1, w3, w2, x, gate)


def _final_norm_kernel(x_ref, g_ref, o_ref):
    x = x_ref[...]
    ms = jnp.mean(x * x, axis=-1, keepdims=True)
    o_ref[...] = x * lax.rsqrt(ms + EPS) * g_ref[...]


def _final_norm(x, g, *, tm):
    t, d = x.shape
    return pl.pallas_call(
        _final_norm_kernel,
        grid=(t // tm,),
        in_specs=[pl.BlockSpec((tm, d), lambda i: (i, 0)), pl.BlockSpec((1, d), lambda i: (0, 0))],
        out_specs=pl.BlockSpec((tm, d), lambda i: (i, 0)),
        out_shape=jax.ShapeDtypeStruct((t, d), F32),
        compiler_params=_params(1),
        name="final_norm",
    )(x, g)


def _s5_discretise(lam_re, lam_im, log_dt, b_re, b_im, c_re, c_im):
    two, g, p = lam_re.shape
    s = b_re.shape[-1]
    gb = LANES // s
    nb = g // gb
    dt = jnp.exp(log_dt)[..., None]
    mag = jnp.exp(lam_re * dt)
    a_re = mag * jnp.cos(lam_im * dt)
    a_im = mag * jnp.sin(lam_im * dt)
    den = lam_re * lam_re + lam_im * lam_im
    q_re = ((a_re - 1.0) * lam_re + a_im * lam_im) / den
    q_im = (a_im * lam_re - (a_re - 1.0) * lam_im) / den
    bb_re = q_re[..., None] * b_re - q_im[..., None] * b_im
    bb_im = q_re[..., None] * b_im + q_im[..., None] * b_re
    eye = jnp.eye(gb, dtype=F32)

    def bdiag_b(m):
        m = m.reshape(two, nb, gb, p, s)
        return jnp.einsum("dbgps,gh->dbgshp", m, eye).reshape(two, nb, gb * s, gb * p)

    def bdiag_c(m):
        m = m.reshape(two, nb, gb, s, p)
        return jnp.einsum("dbgsp,gh->dbgphs", m, eye).reshape(two, nb, gb * p, gb * s)

    bmat = jnp.concatenate([bdiag_b(bb_re), bdiag_b(bb_im)], axis=-1)
    cmat = jnp.concatenate([bdiag_c(c_re), -bdiag_c(c_im)], axis=-2)
    blk = lambda a: a.reshape(two, nb, 1, gb * p)
    return blk(a_re), blk(a_im), bmat.astype(MXU_DTYPE), cmat.astype(MXU_DTYPE)


def kernel(x_prompt, x_sample, state_rglru, state_s5, c, c_ctx, norm_mix, norm_ffn, ada_w, ada_b, w_in, b_gate, rg_conv_w, rg_conv_b, rg_wa, rg_ba, rg_wx, rg_bx, rg_lambda, rg_out, cv_dw, cv_db, cv_ln_g, cv_ln_b, cv_out, s5_lambda_re, s5_lambda_im, s5_log_dt, s5_b_re, s5_b_im, s5_c_re, s5_c_im, s5_d, s5_glu, w_out, router_w, router_b, exp_w1, exp_w3, exp_w2, sh_w1, sh_w3, sh_w2, norm_final):
    batch, lc, d = x_prompt.shape
    dec_batch, ls, _ = x_sample.shape
    depth = w_in.shape[0]
    rg = rg_out.shape[1]
    cv = cv_out.shape[1]
    s5 = s5_glu.shape[1]
    n_groups_s5, n_state = s5_lambda_re.shape[2], s5_lambda_re.shape[3]
    assert dec_batch == SUBLANES and batch % SUBLANES == 0
    assert ls % GRID_W == 0 and lc % CHUNK_TILES == 0
    assert rg_wa.shape[-1] == LANES
    n_ctx_units = batch // SUBLANES
    grid_rows = ls // GRID_W
    n_sample_rows = ls * SUBLANES
    units = ((0, ls, True, True),) + tuple(
        (ls + k * lc, lc, False, False) for k in range(n_ctx_units))
    tm = 512
    tn = min(512, rg, cv, s5)
    tk = min(512, d)
    tc = min(128, lc)
    assert tc % grid_rows == 0 and (GRID_W * grid_rows) % tc == 0

    xs = jnp.transpose(x_sample, (1, 0, 2)).reshape(ls * SUBLANES, d)
    xc = jnp.transpose(x_prompt.reshape(n_ctx_units, SUBLANES, lc, d), (0, 2, 1, 3))
    x = jnp.concatenate([xs, xc.reshape(n_ctx_units * lc * SUBLANES, d)], axis=0)

    cond = jnp.concatenate([c, c_ctx[None], jnp.zeros((SUBLANES - 1, d), F32)], axis=0)
    mod = _ada(cond, ada_w, ada_b)
    mod = jnp.stack([mod[:, :SUBLANES],
                     jnp.broadcast_to(mod[:, SUBLANES:SUBLANES + 1], (depth, SUBLANES, 6 * d))], axis=1)
    mod = mod.reshape(depth, 2, SUBLANES, 6, d)

    cast = lambda a: a.astype(MXU_DTYPE)
    rg_fin, s5_fin = [], []
    for l in range(depth):
        shift1, scale1, gate1, shift2, scale2, gate2 = (mod[l, :, :, k] for k in range(6))
        u_rg, z_cv, u_s5, gates = _win(
            x, norm_mix[l][None], scale1, shift1, cast(w_in[l]), b_gate[l][None],
            rg=rg, cv=cv, s5=s5, n_sample_rows=n_sample_rows, tm=tm, tn=tn)

        h0_rg = jnp.transpose(state_rglru[:, l], (1, 0, 2))
        heads = rg // LANES
        y_rg, fin_rg = _rglru(
            u_rg, rg_conv_w[l], rg_conv_b[l][None],
            cast(rg_wa[l]), cast(rg_wx[l]), rg_ba[l], rg_bx[l], rg_lambda[l], h0_rg, units=units)
        rg_fin.append(fin_rg)

        dw_tiles = jnp.broadcast_to(cv_dw[l][:, None, :], (cv_dw.shape[1], SUBLANES, cv))
        zc = _conv_branch(z_cv, dw_tiles, cv_db[l][None], cv_ln_g[l][None], cv_ln_b[l][None],
                          n_sample_chunks=ls // CHUNK_TILES, ctx_unit_chunks=lc // CHUNK_TILES)

        a_re, a_im, bmat, cmat = _s5_discretise(
            s5_lambda_re[l], s5_lambda_im[l], s5_log_dt[l], s5_b_re[l], s5_b_im[l],
            s5_c_re[l], s5_c_im[l])
        nb = s5 // LANES
        st = jnp.transpose(state_s5[:, l], (1, 4, 0, 2, 3))
        st = st.reshape(2, 2, SUBLANES, nb, (n_groups_s5 // nb) * n_state)
        h0_s5 = jnp.transpose(st, (0, 3, 2, 1, 4)).reshape(2, nb, SUBLANES, -1)
        y_s5, fin_s5 = _s5(u_s5, s5_d[l][None], a_re, a_im, bmat, cmat, h0_s5,
                           units=units, tc=tc, grid_rows=grid_rows)
        s5_fin.append(fin_s5)

        x = _merge(y_rg, zc, y_s5, gates, cast(rg_out[l]), cast(cv_out[l]), cast(s5_glu[l]),
                   cast(w_out[l]), x, gate1, n_sample_rows=n_sample_rows, tm=tm, tk=tk)

        hn, comb = _router(x, norm_ffn[l][None], scale2, shift2, router_w[l].T, router_b[l][:, None],
                           n_sample_rows=n_sample_rows, tm=tm)
        w1 = cast(jnp.concatenate([exp_w1[l], sh_w1[l][None]], axis=0))
        w3 = cast(jnp.concatenate([exp_w3[l], sh_w3[l][None]], axis=0))
        w2 = cast(jnp.concatenate([exp_w2[l], sh_w2[l][None]], axis=0))
        x = _moe(hn, comb, w1, w3, w2, x, gate2, n_sample_rows=n_sample_rows, tm=tm)

    y = _final_norm(x, norm_final[None], tm=tm)
    y_sample = jnp.transpose(y[:n_sample_rows].reshape(ls, SUBLANES, d), (1, 0, 2))
    y_prompt = jnp.transpose(y[n_sample_rows:].reshape(n_ctx_units, lc, SUBLANES, d),
                             (0, 2, 1, 3)).reshape(batch, lc, d)

    fr = jnp.stack(rg_fin, axis=0)[:, 1:]
    new_state_rglru = jnp.transpose(fr, (1, 3, 0, 2, 4)).reshape(batch, depth, 2, rg)
    fs = jnp.stack(s5_fin, axis=0)[:, 1:]
    nb = s5 // LANES
    fs = fs.reshape(depth, n_ctx_units, 2, nb, SUBLANES, 2, n_groups_s5 // nb, n_state)
    new_state_s5 = jnp.transpose(fs, (1, 4, 0, 2, 3, 6, 7, 5)).reshape(
        batch, depth, 2, n_groups_s5, n_state, 2)
    return y_prompt, y_sample, new_state_rglru, new_state_s5
```

```python
import functools
import math

import jax
import jax.numpy as jnp
from jax import lax
from jax.experimental import pallas as pl
from jax.experimental.pallas import tpu as pltpu

F32 = jnp.float32
MXU_DTYPE = jnp.bfloat16

SUBLANES = 8
LANES = 128
VMEM_LIMIT_BYTES = 56 * 1024 * 1024

GRID_W = 64
RG_C = 8.0
S5_GROUP = 16
TOP_K = 8
N_ROUTE_GROUPS = 8
TOPK_GROUPS = 4
ROUTE_SCALE = 2.5
EPS = 1e-6

CHUNK_TILES = 64
CONV_HALO_TILES = 16


def _params(n_grid_dims):
    return pltpu.CompilerParams(
        dimension_semantics=("arbitrary",) * n_grid_dims,
        vmem_limit_bytes=VMEM_LIMIT_BYTES,
    )


def _mm(a, b):
    return jnp.dot(a.astype(MXU_DTYPE), b.astype(MXU_DTYPE), preferred_element_type=F32)


def _sigmoid(x):
    return 1.0 / (1.0 + jnp.exp(-x))


def _silu(x):
    return x * _sigmoid(x)


def _tile_bcast_mul_add(y, scale, shift):
    rows, d = y.shape
    y3 = y.reshape(rows // SUBLANES, SUBLANES, d)
    return (y3 * scale[None] + shift[None]).reshape(rows, d)


def _gated_residual(x, gate, y):
    rows, d = y.shape
    return (y.reshape(rows // SUBLANES, SUBLANES, d) * gate[None]).reshape(rows, d) + x


def _norm_mod(x, g, scale, shift):
    ms = jnp.mean(x * x, axis=-1, keepdims=True)
    y = x * lax.rsqrt(ms + EPS) * g
    return _tile_bcast_mul_add(y, 1.0 + scale, shift)


def _ada_kernel(c_ref, w_ref, b_ref, o_ref):
    o_ref[0] = _mm(_silu(c_ref[...]), w_ref[0]) + b_ref[0]


def _ada(cond, ada_w, ada_b):
    depth, d, n = ada_w.shape
    tn = math.gcd(n, 1024)
    rows = cond.shape[0]
    return pl.pallas_call(
        _ada_kernel,
        grid=(depth, n // tn),
        in_specs=[
            pl.BlockSpec((rows, d), lambda l, j: (0, 0)),
            pl.BlockSpec((1, d, tn), lambda l, j: (l, 0, j)),
            pl.BlockSpec((1, 1, tn), lambda l, j: (l, 0, j)),
        ],
        out_specs=pl.BlockSpec((1, rows, tn), lambda l, j: (l, 0, j)),
        out_shape=jax.ShapeDtypeStruct((depth, rows, n), F32),
        compiler_params=_params(2),
        name="ada",
    )(cond, ada_w, ada_b.reshape(depth, 1, n))


def _win_kernel(x_ref, g_ref, sc_ref, sh_ref, wa_ref, wb_ref, bg_ref,
                urg_ref, z_ref, us5_ref, gt_ref, hn_ref, *, n_rg, n_cv, n_s5):
    j = pl.program_id(1)

    @pl.when(j == 0)
    def _():
        hn_ref[...] = _norm_mod(x_ref[...], g_ref[...], sc_ref[0], sh_ref[0]).astype(hn_ref.dtype)

    hn = hn_ref[...]
    a = jnp.dot(hn, wa_ref[...], preferred_element_type=F32)

    @pl.when(j < n_rg)
    def _():
        urg_ref[...] = a

    @pl.when((j >= n_rg) & (j < n_rg + n_cv))
    def _():
        b = jnp.dot(hn, wb_ref[...], preferred_element_type=F32)
        z_ref[...] = a * _sigmoid(b)

    @pl.when((j >= n_rg + n_cv) & (j < n_rg + n_cv + n_s5))
    def _():
        us5_ref[...] = a

    @pl.when(j >= n_rg + n_cv + n_s5)
    def _():
        gt_ref[...] = _sigmoid(a + bg_ref[...]).astype(gt_ref.dtype)


def _win(x, g, scale, shift, w_in, b_gate, *, rg, cv, s5, n_sample_rows, tm, tn):
    t, d = x.shape
    n_gate = b_gate.shape[-1]
    n_rg, n_cv, n_s5, n_g = rg // tn, cv // tn, s5 // tn, n_gate // tn
    ns = n_sample_rows // tm
    unit = lambda i, j: ((i >= ns).astype(jnp.int32), 0, 0)
    clip = lambda v, n: jnp.clip(v, 0, n - 1)
    kern = functools.partial(_win_kernel, n_rg=n_rg, n_cv=n_cv, n_s5=n_s5)
    return pl.pallas_call(
        kern,
        grid=(t // tm, n_rg + n_cv + n_s5 + n_g),
        in_specs=[
            pl.BlockSpec((tm, d), lambda i, j: (i, 0)),
            pl.BlockSpec((1, d), lambda i, j: (0, 0)),
            pl.BlockSpec((1, SUBLANES, d), unit),
            pl.BlockSpec((1, SUBLANES, d), unit),
            pl.BlockSpec((d, tn), lambda i, j: (0, jnp.where(j < n_rg + n_cv, j, j + n_cv))),
            pl.BlockSpec((d, tn), lambda i, j: (0, n_rg + n_cv + clip(j - n_rg, n_cv))),
            pl.BlockSpec((1, tn), lambda i, j: (0, clip(j - (n_rg + n_cv + n_s5), n_g))),
        ],
        out_specs=[
            pl.BlockSpec((tm, tn), lambda i, j: (i, clip(j, n_rg))),
            pl.BlockSpec((tm, tn), lambda i, j: (i, clip(j - n_rg, n_cv))),
            pl.BlockSpec((tm, tn), lambda i, j: (i, clip(j - n_rg - n_cv, n_s5))),
            pl.BlockSpec((tm, tn), lambda i, j: (i, clip(j - n_rg - n_cv - n_s5, n_g))),
        ],
        out_shape=[
            jax.ShapeDtypeStruct((t, rg), F32),
            jax.ShapeDtypeStruct((t, cv), F32),
            jax.ShapeDtypeStruct((t, s5), F32),
            jax.ShapeDtypeStruct((t, n_gate), MXU_DTYPE),
        ],
        scratch_shapes=[pltpu.VMEM((tm, d), MXU_DTYPE)],
        compiler_params=_params(2),
        name="win",
    )(x, g, scale, shift, w_in, w_in, b_gate)


def _rg_kernel(u_ref, cw_ref, cb_ref, wa_ref, wx_ref, ba_ref, bx_ref, lam_ref, h0_ref,
               y_ref, fin_ref, ext_ref, a_ref, b_ref, *, units, ch, kconv):
    rows = ch * SUBLANES
    pad_lo = (kconv // 2) * SUBLANES
    pad_hi = (kconv - 1 - kconv // 2) * SUBLANES
    t_rows = u_ref.shape[0]
    for d in (0, 1):
        neg_lam = -lam_ref[d:d + 1, :]
        softplus = jnp.maximum(neg_lam, 0.0) + jnp.log1p(jnp.exp(-jnp.abs(neg_lam)))
        coef = -RG_C * softplus
        wa = wa_ref[d, 0]
        wx = wx_ref[d, 0]
        ba = ba_ref[d:d + 1, :]
        bx = bx_ref[d:d + 1, :]
        for ui, (t0, nt, _, has_h0) in enumerate(units):
            nch = nt // ch

            def chunk_body(ci, h, d=d, t0=t0, nch=nch, coef=coef, wa=wa, wx=wx, ba=ba, bx=bx):
                c = ci if d == 0 else nch - 1 - ci
                r0 = pl.multiple_of((t0 + c * ch) * SUBLANES, SUBLANES)
                lo_start = pl.multiple_of(jnp.maximum(r0 - pad_lo, 0), SUBLANES)
                hi_start = pl.multiple_of(jnp.minimum(r0 + rows, t_rows - pad_hi), SUBLANES)
                ext_ref[0:pad_lo, :] = jnp.where(c > 0, u_ref[pl.ds(lo_start, pad_lo), :], 0.0)
                ext_ref[pad_lo:pad_lo + rows, :] = u_ref[pl.ds(r0, rows), :]
                ext_ref[pad_lo + rows:pad_lo + rows + pad_hi, :] = jnp.where(
                    c < nch - 1, u_ref[pl.ds(hi_start, pad_hi), :], 0.0)
                xc = cb_ref[...] + cw_ref[0:1, :] * ext_ref[0:rows, :]
                for k in range(1, kconv):
                    xc = xc + cw_ref[k:k + 1, :] * ext_ref[k * SUBLANES:k * SUBLANES + rows, :]
                r = _sigmoid(_mm(xc, wa) + ba)
                i = _sigmoid(_mm(xc, wx) + bx)
                a = jnp.exp(coef * r)
                a_ref[...] = a
                b_ref[...] = jnp.sqrt(1.0 - a * a) * i * xc

                def step(s, h):
                    tt = s if d == 0 else ch - 1 - s
                    o = pl.multiple_of(tt * SUBLANES, SUBLANES)
                    h = a_ref[pl.ds(o, SUBLANES), :] * h + b_ref[pl.ds(o, SUBLANES), :]
                    dst = pl.ds(pl.multiple_of(r0 + o, SUBLANES), SUBLANES)
                    if d == 0:
                        y_ref[dst, :] = h
                    else:
                        y_ref[dst, :] = y_ref[dst, :] + h
                    return h

                return lax.fori_loop(0, ch, step, h, unroll=8)

            h_init = h0_ref[d] if has_h0 else jnp.zeros((SUBLANES, LANES), F32)
            fin_ref[ui, d] = lax.fori_loop(0, nch, chunk_body, h_init)


def _rglru(u, conv_w, conv_b, wa, wx, ba, bx, lam, h0, *, units):
    t, rg = u.shape
    heads = rg // LANES
    kconv = conv_w.shape[0]
    n_units = len(units)
    rows = CHUNK_TILES * SUBLANES
    kern = functools.partial(_rg_kernel, units=units, ch=CHUNK_TILES, kconv=kconv)
    col = lambda h: (0, h)
    return pl.pallas_call(
        kern,
        grid=(heads,),
        in_specs=[
            pl.BlockSpec((t, LANES), col),
            pl.BlockSpec((kconv, LANES), col),
            pl.BlockSpec((1, LANES), col),
            pl.BlockSpec((2, 1, LANES, LANES), lambda h: (0, h, 0, 0)),
            pl.BlockSpec((2, 1, LANES, LANES), lambda h: (0, h, 0, 0)),
            pl.BlockSpec((2, LANES), col),
            pl.BlockSpec((2, LANES), col),
            pl.BlockSpec((2, LANES), col),
            pl.BlockSpec((2, SUBLANES, LANES), lambda h: (0, 0, h)),
        ],
        out_specs=[
            pl.BlockSpec((t, LANES), col),
            pl.BlockSpec((n_units, 2, SUBLANES, LANES), lambda h: (0, 0, 0, h)),
        ],
        out_shape=[
            jax.ShapeDtypeStruct((t, rg), F32),
            jax.ShapeDtypeStruct((n_units, 2, SUBLANES, rg), F32),
        ],
        scratch_shapes=[
            pltpu.VMEM((rows + (kconv - 1) * SUBLANES, LANES), F32),
            pltpu.VMEM((rows, LANES), F32),
            pltpu.VMEM((rows, LANES), F32),
        ],
        compiler_params=_params(1),
        name="rglru",
    )(u, conv_w, conv_b, wa, wx, ba, bx, lam, h0)


def _cv_kernel(zp_ref, zc_ref, zn_ref, w_ref, db_ref, lg_ref, lb_ref, o_ref, ext_ref, acc_ref,
               *, n_sample_chunks, ctx_unit_chunks, kc):
    i = pl.program_id(0)
    rows, c = zc_ref.shape
    hr = CONV_HALO_TILES * SUBLANES
    group = SUBLANES
    is_ctx = i >= n_sample_chunks
    cpos = (i - n_sample_chunks) % ctx_unit_chunks
    lo_ok = is_ctx & (cpos > 0)
    hi_ok = is_ctx & (cpos < ctx_unit_chunks - 1)
    ext_ref[0:hr, :] = jnp.where(lo_ok, zp_ref[rows - hr:rows, :], 0.0)
    ext_ref[hr:hr + rows, :] = zc_ref[...]
    ext_ref[hr + rows:hr + rows + hr, :] = jnp.where(hi_ok, zn_ref[0:hr, :], 0.0)
    first_tap_tile = CONV_HALO_TILES - kc // 2

    def lane_body(lb, carry):
        l0 = pl.multiple_of(lb * LANES, LANES)

        def grp_body(g, carry):
            acc = jnp.zeros((group, SUBLANES, LANES), F32)
            for k in range(kc):
                src = pl.multiple_of((g * group + k + first_tap_tile) * SUBLANES, SUBLANES)
                e = ext_ref[pl.ds(src, group * SUBLANES), pl.ds(l0, LANES)]
                acc = acc + e.reshape(group, SUBLANES, LANES) * w_ref[k, :, pl.ds(l0, LANES)][None]
            dst = pl.multiple_of(g * group * SUBLANES, SUBLANES)
            acc_ref[pl.ds(dst, group * SUBLANES), pl.ds(l0, LANES)] = acc.reshape(group * SUBLANES, LANES)
            return carry

        return lax.fori_loop(0, rows // (group * SUBLANES), grp_body, carry)

    lax.fori_loop(0, c // LANES, lane_body, 0)
    z = acc_ref[...] + db_ref[...]
    mu = jnp.mean(z, axis=-1, keepdims=True)
    zc = z - mu
    var = jnp.mean(zc * zc, axis=-1, keepdims=True)
    y = zc * lax.rsqrt(var + EPS) * lg_ref[...] + lb_ref[...]
    o_ref[...] = _silu(y).astype(o_ref.dtype)


def _conv_branch(z, dw_tiles, db, ln_g, ln_b, *, n_sample_chunks, ctx_unit_chunks):
    t, c = z.shape
    kc = dw_tiles.shape[0]
    rows = CHUNK_TILES * SUBLANES
    n = t // rows
    kern = functools.partial(_cv_kernel, n_sample_chunks=n_sample_chunks,
                             ctx_unit_chunks=ctx_unit_chunks, kc=kc)
    vec = pl.BlockSpec((1, c), lambda i: (0, 0))
    return pl.pallas_call(
        kern,
        grid=(n,),
        in_specs=[
            pl.BlockSpec((rows, c), lambda i: (jnp.maximum(i - 1, 0), 0)),
            pl.BlockSpec((rows, c), lambda i: (i, 0)),
            pl.BlockSpec((rows, c), lambda i: (jnp.minimum(i + 1, n - 1), 0)),
            pl.BlockSpec((kc, SUBLANES, c), lambda i: (0, 0, 0)),
            vec, vec, vec,
        ],
        out_specs=pl.BlockSpec((rows, c), lambda i: (i, 0)),
        out_shape=jax.ShapeDtypeStruct((t, c), MXU_DTYPE),
        scratch_shapes=[
            pltpu.VMEM((rows + 2 * CONV_HALO_TILES * SUBLANES, c), F32),
            pltpu.VMEM((rows, c), F32),
        ],
        compiler_params=_params(1),
        name="conv",
    )(z, z, z, dw_tiles, db, ln_g, ln_b)


def _s5_kernel(u_ref, dsk_ref, are_ref, aim_ref, bm_ref, cm_ref, h0_ref,
               y_ref, fin_ref, lhs_ref, hs_ref, yc_ref, *, units, tc, grid_rows):
    ns = are_ref.shape[-1]

    def tile_of(unit, c, s):
        t0, _, is_grid, _ = unit
        if is_grid:
            return t0 + (s % grid_rows) * GRID_W + c * (tc // grid_rows) + s // grid_rows
        return t0 + c * tc + s

    for d in (0, 1):
        a_re = jnp.broadcast_to(are_ref[d, 0], (SUBLANES, ns))
        a_im = jnp.broadcast_to(aim_ref[d, 0], (SUBLANES, ns))
        for ui, unit in enumerate(units):
            nch = unit[1] // tc

            def chunk_body(ci, h, d=d, unit=unit, nch=nch, a_re=a_re, a_im=a_im):
                c = ci if d == 0 else nch - 1 - ci
                for s in range(tc):
                    r = pl.multiple_of(tile_of(unit, c, s) * SUBLANES, SUBLANES)
                    lhs_ref[s * SUBLANES:(s + 1) * SUBLANES, :] = u_ref[pl.ds(r, SUBLANES), :]
                hs_ref[...] = _mm(lhs_ref[...], bm_ref[d, 0])

                def step(si, h):
                    s = si if d == 0 else tc - 1 - si
                    o = pl.ds(pl.multiple_of(s * SUBLANES, SUBLANES), SUBLANES)
                    h_re, h_im = h
                    n_re = a_re * h_re - a_im * h_im + hs_ref[o, 0:ns]
                    n_im = a_re * h_im + a_im * h_re + hs_ref[o, ns:2 * ns]
                    hs_ref[o, 0:ns] = n_re
                    hs_ref[o, ns:2 * ns] = n_im
                    return n_re, n_im

                h = lax.fori_loop(0, tc, step, h, unroll=4)
                yc_ref[...] = _mm(hs_ref[...], cm_ref[d, 0])
                for s in range(tc):
                    r = pl.multiple_of(tile_of(unit, c, s) * SUBLANES, SUBLANES)
                    sl = slice(s * SUBLANES, (s + 1) * SUBLANES)
                    if d == 0:
                        y_ref[pl.ds(r, SUBLANES), :] = dsk_ref[...] * lhs_ref[sl, :] + yc_ref[sl, :]
                    else:
                        y_ref[pl.ds(r, SUBLANES), :] = y_ref[pl.ds(r, SUBLANES), :] + yc_ref[sl, :]
                return h

            if unit[3]:
                h_init = (h0_ref[d, 0, :, 0:ns], h0_ref[d, 0, :, ns:2 * ns])
            else:
                h_init = (jnp.zeros((SUBLANES, ns), F32), jnp.zeros((SUBLANES, ns), F32))
            f_re, f_im = lax.fori_loop(0, nch, chunk_body, h_init)
            fin_ref[ui, d, 0, :, 0:ns] = f_re
            fin_ref[ui, d, 0, :, ns:2 * ns] = f_im


def _s5(u, dskip, a_re, a_im, bmat, cmat, h0, *, units, tc, grid_rows):
    t, width = u.shape
    nb = width // LANES
    ns = a_re.shape[-1]
    n_units = len(units)
    kern = functools.partial(_s5_kernel, units=units, tc=tc, grid_rows=grid_rows)
    blk4 = lambda *shape: pl.BlockSpec((2, 1) + shape, lambda j: (0, j, 0, 0))
    return pl.pallas_call(
        kern,
        grid=(nb,),
        in_specs=[
            pl.BlockSpec((t, LANES), lambda j: (0, j)),
            pl.BlockSpec((1, LANES), lambda j: (0, j)),
            blk4(1, ns), blk4(1, ns),
            blk4(LANES, 2 * ns), blk4(2 * ns, LANES),
            blk4(SUBLANES, 2 * ns),
        ],
        out_specs=[
            pl.BlockSpec((t, LANES), lambda j: (0, j)),
            pl.BlockSpec((n_units, 2, 1, SUBLANES, 2 * ns), lambda j: (0, 0, j, 0, 0)),
        ],
        out_shape=[
            jax.ShapeDtypeStruct((t, width), F32),
            jax.ShapeDtypeStruct((n_units, 2, nb, SUBLANES, 2 * ns), F32),
        ],
        scratch_shapes=[
            pltpu.VMEM((tc * SUBLANES, LANES), F32),
            pltpu.VMEM((tc * SUBLANES, 2 * ns), F32),
            pltpu.VMEM((tc * SUBLANES, LANES), F32),
        ],
        compiler_params=_params(1),
        name="s5",
    )(u, dskip, a_re, a_im, bmat, cmat, h0)


def _merge_kernel(yrg_ref, zc_ref, ys5_ref, g0_ref, g1_ref, g2_ref, wrg_ref, wcv_ref, wsv_ref,
                  wsg_ref, wo_ref, x_ref, gate_ref, o_ref, acc_ref, rgb_ref, s5b_ref):
    j = pl.program_id(1)

    @pl.when(j == 0)
    def _():
        acc_ref[...] = jnp.zeros_like(acc_ref)
        rgb_ref[...] = yrg_ref[...].astype(rgb_ref.dtype)
        s5b_ref[...] = ys5_ref[...].astype(s5b_ref.dtype)

    dot = lambda a, b: jnp.dot(a, b, preferred_element_type=F32)
    br_rg = dot(rgb_ref[...], wrg_ref[...])
    br_cv = dot(zc_ref[...], wcv_ref[...])
    s5b = s5b_ref[...]
    br_s5 = dot(s5b, wsv_ref[...]) * _sigmoid(dot(s5b, wsg_ref[...]))
    merged = (g0_ref[...].astype(F32) * br_rg + g1_ref[...].astype(F32) * br_cv
              + g2_ref[...].astype(F32) * br_s5)
    acc_ref[...] += dot(merged.astype(MXU_DTYPE), wo_ref[...])

    @pl.when(j == pl.num_programs(1) - 1)
    def _():
        o_ref[...] = _gated_residual(x_ref[...], gate_ref[0], acc_ref[...])


def _merge(y_rg, zc, y_s5, gates, rg_out, cv_out, s5_glu, w_out, x, gate, *, n_sample_rows, tm, tk):
    t, d = x.shape
    rg, cv, s5 = y_rg.shape[1], zc.shape[1], y_s5.shape[1]
    nk = d // tk
    ns = n_sample_rows // tm
    row = lambda w: pl.BlockSpec((tm, w), lambda i, j: (i, 0))
    gspec = lambda k: pl.BlockSpec((tm, tk), lambda i, j: (i, k * nk + j))
    wcol = lambda rows, off: pl.BlockSpec((rows, tk), lambda i, j: (0, off + j))
    return pl.pallas_call(
        _merge_kernel,
        grid=(t // tm, nk),
        in_specs=[
            row(rg), row(cv), row(s5),
            gspec(0), gspec(1), gspec(2),
            wcol(rg, 0), wcol(cv, 0), wcol(s5, 0), wcol(s5, nk),
            pl.BlockSpec((tk, d), lambda i, j: (j, 0)),
            row(d),
            pl.BlockSpec((1, SUBLANES, d), lambda i, j: ((i >= ns).astype(jnp.int32), 0, 0)),
        ],
        out_specs=row(d),
        out_shape=jax.ShapeDtypeStruct((t, d), F32),
        scratch_shapes=[
            pltpu.VMEM((tm, d), F32),
            pltpu.VMEM((tm, rg), MXU_DTYPE),
            pltpu.VMEM((tm, s5), MXU_DTYPE),
        ],
        compiler_params=_params(2),
        name="merge",
    )(y_rg, zc, y_s5, gates, gates, gates, rg_out, cv_out, s5_glu, s5_glu, w_out, x, gate)


def _pack_bf16_pairs(v):
    half = v.shape[1] // 2
    bits = lambda a: lax.bitcast_convert_type(a.astype(jnp.bfloat16).astype(F32), jnp.uint32)
    return (bits(v[:, :half]) >> 16) | (bits(v[:, half:]) & jnp.uint32(0xFFFF0000))


def _unpack_bf16_pairs(p):
    lo = lax.bitcast_convert_type(p << 16, F32)
    hi = lax.bitcast_convert_type(p & jnp.uint32(0xFFFF0000), F32)
    return lo, hi


def _router_kernel(x_ref, g_ref, sc_ref, sh_ref, rw_ref, rb_ref,
                   hnp_ref, eidx_ref, rank_ref, wtok_ref, cnt_ref, run_ref):
    @pl.when(pl.program_id(0) == 0)
    def _():
        run_ref[...] = jnp.zeros_like(run_ref)

    hn = _norm_mod(x_ref[...], g_ref[...], sc_ref[0], sh_ref[0])
    hnp_ref[...] = _pack_bf16_pairs(hn)
    logits = lax.dot_general(rw_ref[...], hn, (((1,), (1,)), ((), ())),
                             precision=lax.Precision.HIGHEST, preferred_element_type=F32)
    s = _sigmoid(logits)
    choice = s + rb_ref[...]
    n_exp, tm = choice.shape
    gsize = n_exp // N_ROUTE_GROUPS
    neg_inf = jnp.float32(-jnp.inf)
    c3 = choice.reshape(N_ROUTE_GROUPS, gsize, tm)
    sub = lax.broadcasted_iota(jnp.int32, c3.shape, 1)
    m1 = jnp.max(c3, axis=1, keepdims=True)
    i1 = jnp.min(jnp.where(c3 == m1, sub, gsize), axis=1, keepdims=True)
    m2 = jnp.max(jnp.where(sub == i1, neg_inf, c3), axis=1, keepdims=True)
    gscore = jnp.broadcast_to(m1 + m2, c3.shape)
    gidx = lax.broadcasted_iota(jnp.int32, c3.shape, 0)
    beaten = jnp.zeros(c3.shape, jnp.int32)
    for gp in range(N_ROUTE_GROUPS):
        other = gscore[gp:gp + 1]
        wins = (other > gscore) | ((other == gscore) & (gidx > gp))
        beaten = beaten + wins.astype(jnp.int32)
    masked = jnp.where(beaten < TOPK_GROUPS, c3, neg_inf).reshape(n_exp, tm)
    eidx = lax.broadcasted_iota(jnp.int32, masked.shape, 0)
    beaten = jnp.zeros(masked.shape, jnp.int32)
    for ep in range(n_exp):
        other = masked[ep:ep + 1, :]
        wins = (other > masked) | ((other == masked) & (eidx > ep))
        beaten = beaten + wins.astype(jnp.int32)
    sel = beaten < TOP_K
    w = jnp.where(sel, s, 0.0)
    comb = ROUTE_SCALE * w / jnp.sum(w, axis=0, keepdims=True)
    self32 = sel.astype(F32)
    before = (lax.broadcasted_iota(jnp.int32, (tm, tm), 0) < lax.broadcasted_iota(jnp.int32, (tm, tm), 1))
    rank = run_ref[...] + _mm(self32, before.astype(F32))
    run_ref[...] = run_ref[...] + jnp.sum(self32, axis=1, keepdims=True)
    cnt_ref[...] = jnp.broadcast_to(run_ref[...], cnt_ref.shape)
    eidx_f = eidx.astype(F32)
    w_rows = []
    for j in range(TOP_K):
        m = beaten == j
        pick = lambda v: jnp.sum(jnp.where(m, v, 0.0), axis=0, keepdims=True)
        eidx_ref[j:j + 1, :] = pick(eidx_f).astype(jnp.int32)
        rank_ref[j:j + 1, :] = pick(rank).astype(jnp.int32)
        w_rows.append(pick(comb))
    w_rows.append(jnp.zeros((wtok_ref.shape[1] - TOP_K, tm), F32))
    wtok_ref[...] = jnp.concatenate(w_rows, axis=0).T


def _router(x, g, scale, shift, router_wt, router_b, *, n_sample_rows, tm):
    t, d = x.shape
    n_exp = router_wt.shape[0]
    ns = n_sample_rows // tm
    unit = lambda i: ((i >= ns).astype(jnp.int32), 0, 0)
    return pl.pallas_call(
        _router_kernel,
        grid=(t // tm,),
        in_specs=[
            pl.BlockSpec((tm, d), lambda i: (i, 0)),
            pl.BlockSpec((1, d), lambda i: (0, 0)),
            pl.BlockSpec((1, SUBLANES, d), unit),
            pl.BlockSpec((1, SUBLANES, d), unit),
            pl.BlockSpec((n_exp, d), lambda i: (0, 0)),
            pl.BlockSpec((n_exp, 1), lambda i: (0, 0)),
        ],
        out_specs=[
            pl.BlockSpec((tm, d // 2), lambda i: (i, 0)),
            pl.BlockSpec((TOP_K, tm), lambda i: (0, i)),
            pl.BlockSpec((TOP_K, tm), lambda i: (0, i)),
            pl.BlockSpec((tm, LANES), lambda i: (i, 0)),
            pl.BlockSpec((n_exp, LANES), lambda i: (0, 0)),
        ],
        out_shape=[
            jax.ShapeDtypeStruct((t, d // 2), jnp.uint32),
            jax.ShapeDtypeStruct((TOP_K, t), jnp.int32),
            jax.ShapeDtypeStruct((TOP_K, t), jnp.int32),
            jax.ShapeDtypeStruct((t, LANES), F32),
            jax.ShapeDtypeStruct((n_exp, LANES), F32),
        ],
        scratch_shapes=[pltpu.VMEM((n_exp, 1), F32)],
        compiler_params=_params(1),
        name="router",
    )(x, g, scale, shift, router_wt, router_b)


def _swiglu(lo, hi, w1, w3, w2):
    half = lo.shape[1]
    lo = lo.astype(MXU_DTYPE)
    hi = hi.astype(MXU_DTYPE)
    dot = lambda a, b: jnp.dot(a, b, preferred_element_type=F32)
    h1 = dot(lo, w1[0:half, :]) + dot(hi, w1[half:, :])
    h3 = dot(lo, w3[0:half, :]) + dot(hi, w3[half:, :])
    return dot((_silu(h1) * h3).astype(MXU_DTYPE), w2[...])


def _dispatch_kernel(pos_ref, hn_hbm, xs_in_hbm, xs_hbm, sem):
    del xs_in_hbm
    i = pl.program_id(0)
    n_slots, tm = pos_ref.shape

    def issue(t, carry):
        src = hn_hbm.at[pl.ds(i * tm + t, 1)]
        for j in range(n_slots):
            pltpu.make_async_copy(src, xs_hbm.at[pl.ds(pos_ref[j, t], 1)], sem).start()
        return carry

    lax.fori_loop(0, tm, issue, 0)
    for j in range(n_slots):
        pltpu.make_async_copy(hn_hbm.at[pl.ds(0, tm)], xs_hbm.at[pl.ds(0, tm)], sem).wait()


def _dispatch(pos, hnp, xs_init, *, tm):
    n_slots, t = pos.shape
    return pl.pallas_call(
        _dispatch_kernel,
        grid=(t // tm,),
        in_specs=[
            pl.BlockSpec((n_slots, tm), lambda i: (0, i), memory_space=pltpu.SMEM),
            pl.BlockSpec(memory_space=pl.ANY),
            pl.BlockSpec(memory_space=pl.ANY),
        ],
        out_specs=pl.BlockSpec(memory_space=pl.ANY),
        out_shape=jax.ShapeDtypeStruct(xs_init.shape, xs_init.dtype),
        scratch_shapes=[pltpu.SemaphoreType.DMA],
        input_output_aliases={2: 0},
        compiler_params=_params(1),
        name="dispatch",
    )(pos, hnp, xs_init)


def _ffn_kernel(te_ref, nu_ref, xs_ref, w1_ref, w3_ref, w2_ref, ys_ref, w1b_ref, w3b_ref, w2b_ref):
    i = pl.program_id(0)

    @pl.when((i == 0) | (te_ref[i] != te_ref[jnp.maximum(i - 1, 0)]))
    def _():
        w1b_ref[...] = w1_ref[0].astype(w1b_ref.dtype)
        w3b_ref[...] = w3_ref[0].astype(w3b_ref.dtype)
        w2b_ref[...] = w2_ref[0].astype(w2b_ref.dtype)

    @pl.when(i < nu_ref[0])
    def _():
        lo, hi = _unpack_bf16_pairs(xs_ref[...])
        ys_ref[...] = _pack_bf16_pairs(_swiglu(lo, hi, w1b_ref, w3b_ref, w2b_ref))

    @pl.when(i >= nu_ref[0])
    def _():
        ys_ref[...] = jnp.zeros_like(ys_ref)


def _ffn(tile_expert, n_used, xs, w1, w3, w2, *, tg):
    s_pad, half = xs.shape
    _, d, f = w1.shape
    row = lambda i, te, nu: (jnp.minimum(i, nu[0] - 1), 0)
    wblk = lambda i, te, nu: (te[i], 0, 0)
    grid_spec = pltpu.PrefetchScalarGridSpec(
        num_scalar_prefetch=2,
        grid=(s_pad // tg,),
        in_specs=[
            pl.BlockSpec((tg, half), row),
            pl.BlockSpec((1, d, f), wblk),
            pl.BlockSpec((1, d, f), wblk),
            pl.BlockSpec((1, f, d), wblk),
        ],
        out_specs=pl.BlockSpec((tg, half), lambda i, te, nu: (i, 0)),
        scratch_shapes=[
            pltpu.VMEM((d, f), MXU_DTYPE),
            pltpu.VMEM((d, f), MXU_DTYPE),
            pltpu.VMEM((f, d), MXU_DTYPE),
        ],
    )
    return pl.pallas_call(
        _ffn_kernel,
        grid_spec=grid_spec,
        out_shape=jax.ShapeDtypeStruct((s_pad, half), jnp.uint32),
        compiler_params=_params(1),
        name="ffn",
    )(tile_expert, n_used, xs, w1, w3, w2)


def _combine_kernel(pos_ref, hnp_ref, wtok_ref, s1_ref, s3_ref, s2_ref, x_ref, gate_ref, ys_hbm,
                    o_ref, buf_ref, sem):
    n_slots, tm = pos_ref.shape

    def issue(t, carry):
        for j in range(n_slots):
            pltpu.make_async_copy(ys_hbm.at[pl.ds(pos_ref[j, t], 1)],
                                  buf_ref.at[j, pl.ds(t, 1)], sem).start()
        return carry

    lax.fori_loop(0, tm, issue, 0)
    lo, hi = _unpack_bf16_pairs(hnp_ref[...])
    moe = _swiglu(lo, hi, s1_ref, s3_ref, s2_ref)
    half = lo.shape[1]
    acc_lo = moe[:, :half]
    acc_hi = moe[:, half:]
    wtok = wtok_ref[...]
    for j in range(n_slots):
        pltpu.make_async_copy(ys_hbm.at[pl.ds(0, tm)], buf_ref.at[j], sem).wait()
    for j in range(n_slots):
        y_lo, y_hi = _unpack_bf16_pairs(buf_ref[j])
        wj = wtok[:, j:j + 1]
        acc_lo = acc_lo + wj * y_lo
        acc_hi = acc_hi + wj * y_hi
    x = x_ref[...]
    gate = gate_ref[0]
    o_ref[:, :half] = _gated_residual(x[:, :half], gate[:, :half], acc_lo)
    o_ref[:, half:] = _gated_residual(x[:, half:], gate[:, half:], acc_hi)


def _combine(pos, hnp, wtok, s1, s3, s2, x, gate, ys, *, n_sample_rows, tm):
    t, d = x.shape
    n_slots = pos.shape[0]
    f = s1.shape[1]
    ns = n_sample_rows // tm
    full = lambda shape: pl.BlockSpec(shape, lambda i: (0,) * len(shape))
    return pl.pallas_call(
        _combine_kernel,
        grid=(t // tm,),
        in_specs=[
            pl.BlockSpec((n_slots, tm), lambda i: (0, i), memory_space=pltpu.SMEM),
            pl.BlockSpec((tm, d // 2), lambda i: (i, 0)),
            pl.BlockSpec((tm, LANES), lambda i: (i, 0)),
            full((d, f)), full((d, f)), full((f, d)),
            pl.BlockSpec((tm, d), lambda i: (i, 0)),
            pl.BlockSpec((1, SUBLANES, d), lambda i: ((i >= ns).astype(jnp.int32), 0, 0)),
            pl.BlockSpec(memory_space=pl.ANY),
        ],
        out_specs=pl.BlockSpec((tm, d), lambda i: (i, 0)),
        out_shape=jax.ShapeDtypeStruct((t, d), F32),
        scratch_shapes=[
            pltpu.VMEM((n_slots, tm, d // 2), jnp.uint32),
            pltpu.SemaphoreType.DMA,
        ],
        compiler_params=_params(1),
        name="combine",
    )(pos, hnp, wtok, s1, s3, s2, x, gate, ys)


def _route_layout(counts, eidx, rank, *, tg, n_tiles):
    padded = (counts + tg - 1) // tg * tg
    ends = jnp.cumsum(padded)
    pos = (ends - padded)[eidx] + rank
    n_used = ends[-1] // tg
    tile_row = jnp.minimum(jnp.arange(n_tiles, dtype=jnp.int32), n_used - 1) * tg
    tile_expert = jnp.searchsorted(ends, tile_row, side="right").astype(jnp.int32)
    return pos, tile_expert, n_used.astype(jnp.int32)[None]


def _final_norm_kernel(x_ref, g_ref, o_ref):
    x = x_ref[...]
    ms = jnp.mean(x * x, axis=-1, keepdims=True)
    o_ref[...] = x * lax.rsqrt(ms + EPS) * g_ref[...]


def _final_norm(x, g, *, tm):
    t, d = x.shape
    return pl.pallas_call(
        _final_norm_kernel,
        grid=(t // tm,),
        in_specs=[pl.BlockSpec((tm, d), lambda i: (i, 0)), pl.BlockSpec((1, d), lambda i: (0, 0))],
        out_specs=pl.BlockSpec((tm, d), lambda i: (i, 0)),
        out_shape=jax.ShapeDtypeStruct((t, d), F32),
        compiler_params=_params(1),
        name="final_norm",
    )(x, g)


def _s5_discretise(lam_re, lam_im, log_dt, b_re, b_im, c_re, c_im):
    two, g, p = lam_re.shape
    s = b_re.shape[-1]
    gb = LANES // s
    nb = g // gb
    dt = jnp.exp(log_dt)[..., None]
    mag = jnp.exp(lam_re * dt)
    a_re = mag * jnp.cos(lam_im * dt)
    a_im = mag * jnp.sin(lam_im * dt)
    den = lam_re * lam_re + lam_im * lam_im
    q_re = ((a_re - 1.0) * lam_re + a_im * lam_im) / den
    q_im = (a_im * lam_re - (a_re - 1.0) * lam_im) / den
    bb_re = q_re[..., None] * b_re - q_im[..., None] * b_im
    bb_im = q_re[..., None] * b_im + q_im[..., None] * b_re
    eye = jnp.eye(gb, dtype=F32)

    def bdiag_b(m):
        m = m.reshape(two, nb, gb, p, s)
        return jnp.einsum("dbgps,gh->dbgshp", m, eye).reshape(two, nb, gb * s, gb * p)

    def bdiag_c(m):
        m = m.reshape(two, nb, gb, s, p)
        return jnp.einsum("dbgsp,gh->dbgphs", m, eye).reshape(two, nb, gb * p, gb * s)

    bmat = jnp.concatenate([bdiag_b(bb_re), bdiag_b(bb_im)], axis=-1)
    cmat = jnp.concatenate([bdiag_c(c_re), -bdiag_c(c_im)], axis=-2)
    blk = lambda a: a.reshape(two, nb, 1, gb * p)
    return blk(a_re), blk(a_im), bmat.astype(MXU_DTYPE), cmat.astype(MXU_DTYPE)


def kernel(x_prompt, x_sample, state_rglru, state_s5, c, c_ctx, norm_mix, norm_ffn, ada_w, ada_b, w_in, b_gate, rg_conv_w, rg_conv_b, rg_wa, rg_ba, rg_wx, rg_bx, rg_lambda, rg_out, cv_dw, cv_db, cv_ln_g, cv_ln_b, cv_out, s5_lambda_re, s5_lambda_im, s5_log_dt, s5_b_re, s5_b_im, s5_c_re, s5_c_im, s5_d, s5_glu, w_out, router_w, router_b, exp_w1, exp_w3, exp_w2, sh_w1, sh_w3, sh_w2, norm_final):
    batch, lc, d = x_prompt.shape
    dec_batch, ls, _ = x_sample.shape
    depth = w_in.shape[0]
    rg = rg_out.shape[1]
    cv = cv_out.shape[1]
    s5 = s5_glu.shape[1]
    n_groups_s5, n_state = s5_lambda_re.shape[2], s5_lambda_re.shape[3]
    assert dec_batch == SUBLANES and batch % SUBLANES == 0
    assert ls % GRID_W == 0 and lc % CHUNK_TILES == 0
    assert rg_wa.shape[-1] == LANES
    n_ctx_units = batch // SUBLANES
    grid_rows = ls // GRID_W
    n_sample_rows = ls * SUBLANES
    units = ((0, ls, True, True),) + tuple(
        (ls + k * lc, lc, False, False) for k in range(n_ctx_units))
    tm = 512
    tn = min(512, rg, cv, s5)
    tk = min(512, d)
    tc = min(128, lc)
    assert tc % grid_rows == 0 and (GRID_W * grid_rows) % tc == 0
    n_tokens = (ls + n_ctx_units * lc) * SUBLANES
    n_exp = router_w.shape[-1]
    tg = 256
    tmc = 256
    n_ffn_tiles = pl.cdiv(n_tokens * TOP_K, tg) + n_exp

    xs = jnp.transpose(x_sample, (1, 0, 2)).reshape(ls * SUBLANES, d)
    xc = jnp.transpose(x_prompt.reshape(n_ctx_units, SUBLANES, lc, d), (0, 2, 1, 3))
    x = jnp.concatenate([xs, xc.reshape(n_ctx_units * lc * SUBLANES, d)], axis=0)

    cond = jnp.concatenate([c, c_ctx[None], jnp.zeros((SUBLANES - 1, d), F32)], axis=0)
    mod = _ada(cond, ada_w, ada_b)
    mod = jnp.stack([mod[:, :SUBLANES],
                     jnp.broadcast_to(mod[:, SUBLANES:SUBLANES + 1], (depth, SUBLANES, 6 * d))], axis=1)
    mod = mod.reshape(depth, 2, SUBLANES, 6, d)

    cast = lambda a: a.astype(MXU_DTYPE)
    sorted_rows = jnp.zeros((n_ffn_tiles * tg, d // 2), jnp.uint32)
    rg_fin, s5_fin = [], []
    for l in range(depth):
        shift1, scale1, gate1, shift2, scale2, gate2 = (mod[l, :, :, k] for k in range(6))
        u_rg, z_cv, u_s5, gates = _win(
            x, norm_mix[l][None], scale1, shift1, cast(w_in[l]), b_gate[l][None],
            rg=rg, cv=cv, s5=s5, n_sample_rows=n_sample_rows, tm=tm, tn=tn)

        h0_rg = jnp.transpose(state_rglru[:, l], (1, 0, 2))
        heads = rg // LANES
        y_rg, fin_rg = _rglru(
            u_rg, rg_conv_w[l], rg_conv_b[l][None],
            cast(rg_wa[l]), cast(rg_wx[l]), rg_ba[l], rg_bx[l], rg_lambda[l], h0_rg, units=units)
        rg_fin.append(fin_rg)

        dw_tiles = jnp.broadcast_to(cv_dw[l][:, None, :], (cv_dw.shape[1], SUBLANES, cv))
        zc = _conv_branch(z_cv, dw_tiles, cv_db[l][None], cv_ln_g[l][None], cv_ln_b[l][None],
                          n_sample_chunks=ls // CHUNK_TILES, ctx_unit_chunks=lc // CHUNK_TILES)

        a_re, a_im, bmat, cmat = _s5_discretise(
            s5_lambda_re[l], s5_lambda_im[l], s5_log_dt[l], s5_b_re[l], s5_b_im[l],
            s5_c_re[l], s5_c_im[l])
        nb = s5 // LANES
        st = jnp.transpose(state_s5[:, l], (1, 4, 0, 2, 3))
        st = st.reshape(2, 2, SUBLANES, nb, (n_groups_s5 // nb) * n_state)
        h0_s5 = jnp.transpose(st, (0, 3, 2, 1, 4)).reshape(2, nb, SUBLANES, -1)
        y_s5, fin_s5 = _s5(u_s5, s5_d[l][None], a_re, a_im, bmat, cmat, h0_s5,
                           units=units, tc=tc, grid_rows=grid_rows)
        s5_fin.append(fin_s5)

        x = _merge(y_rg, zc, y_s5, gates, cast(rg_out[l]), cast(cv_out[l]), cast(s5_glu[l]),
                   cast(w_out[l]), x, gate1, n_sample_rows=n_sample_rows, tm=tm, tk=tk)

        hnp, eidx, rank, wtok, counts = _router(
            x, norm_ffn[l][None], scale2, shift2, router_w[l].T, router_b[l][:, None],
            n_sample_rows=n_sample_rows, tm=tm)
        pos, tile_expert, n_used = _route_layout(
            counts[:, 0].astype(jnp.int32), eidx, rank, tg=tg, n_tiles=n_ffn_tiles)
        sorted_rows = _dispatch(pos, hnp, sorted_rows, tm=tm)
        ys = _ffn(tile_expert, n_used, sorted_rows, exp_w1[l], exp_w3[l], exp_w2[l], tg=tg)
        x = _combine(pos, hnp, wtok, cast(sh_w1[l]), cast(sh_w3[l]), cast(sh_w2[l]), x, gate2, ys,
                     n_sample_rows=n_sample_rows, tm=tmc)

    y = _final_norm(x, norm_final[None], tm=tm)
    y_sample = jnp.transpose(y[:n_sample_rows].reshape(ls, SUBLANES, d), (1, 0, 2))
    y_prompt = jnp.transpose(y[n_sample_rows:].reshape(n_ctx_units, lc, SUBLANES, d),
                             (0, 2, 1, 3)).reshape(batch, lc, d)

    fr = jnp.stack(rg_fin, axis=0)[:, 1:]
    new_state_rglru = jnp.transpose(fr, (1, 3, 0, 2, 4)).reshape(batch, depth, 2, rg)
    fs = jnp.stack(s5_fin, axis=0)[:, 1:]
    nb = s5 // LANES
    fs = fs.reshape(depth, n_ctx_units, 2, nb, SUBLANES, 2, n_groups_s5 // nb, n_state)
    new_state_s5 = jnp.transpose(fs, (1, 4, 0, 2, 3, 6, 7, 5)).reshape(
        batch, depth, 2, n_groups_s5, n_state, 2)
    return y_prompt, y_sample, new_state_rglru, new_state_s5
```

```python
import functools
import math

import jax
import jax.numpy as jnp
from jax import lax
from jax.experimental import pallas as pl
from jax.experimental.pallas import tpu as pltpu

F32 = jnp.float32
MXU_DTYPE = jnp.bfloat16

SUBLANES = 8
LANES = 128
VMEM_LIMIT_BYTES = 56 * 1024 * 1024

GRID_W = 64
RG_C = 8.0
S5_GROUP = 16
TOP_K = 8
N_ROUTE_GROUPS = 8
TOPK_GROUPS = 4
ROUTE_SCALE = 2.5
EPS = 1e-6

CHUNK_TILES = 64
CONV_HALO_TILES = 16


def _params(n_grid_dims):
    return pltpu.CompilerParams(
        dimension_semantics=("arbitrary",) * n_grid_dims,
        vmem_limit_bytes=VMEM_LIMIT_BYTES,
    )


def _mm(a, b):
    return jnp.dot(a.astype(MXU_DTYPE), b.astype(MXU_DTYPE), preferred_element_type=F32)


def _sigmoid(x):
    return 1.0 / (1.0 + jnp.exp(-x))


def _silu(x):
    return x * _sigmoid(x)


def _tile_bcast_mul_add(y, scale, shift):
    rows, d = y.shape
    y3 = y.reshape(rows // SUBLANES, SUBLANES, d)
    return (y3 * scale[None] + shift[None]).reshape(rows, d)


def _gated_residual(x, gate, y):
    rows, d = y.shape
    return (y.reshape(rows // SUBLANES, SUBLANES, d) * gate[None]).reshape(rows, d) + x


def _norm_mod(x, g, scale, shift):
    ms = jnp.mean(x * x, axis=-1, keepdims=True)
    y = x * lax.rsqrt(ms + EPS) * g
    return _tile_bcast_mul_add(y, 1.0 + scale, shift)


def _ada_kernel(c_ref, w_ref, b_ref, o_ref):
    o_ref[0] = _mm(_silu(c_ref[...]), w_ref[0]) + b_ref[0]


def _ada(cond, ada_w, ada_b):
    depth, d, n = ada_w.shape
    tn = math.gcd(n, 1024)
    rows = cond.shape[0]
    return pl.pallas_call(
        _ada_kernel,
        grid=(depth, n // tn),
        in_specs=[
            pl.BlockSpec((rows, d), lambda l, j: (0, 0)),
            pl.BlockSpec((1, d, tn), lambda l, j: (l, 0, j)),
            pl.BlockSpec((1, 1, tn), lambda l, j: (l, 0, j)),
        ],
        out_specs=pl.BlockSpec((1, rows, tn), lambda l, j: (l, 0, j)),
        out_shape=jax.ShapeDtypeStruct((depth, rows, n), F32),
        compiler_params=_params(2),
        name="ada",
    )(cond, ada_w, ada_b.reshape(depth, 1, n))


def _win_kernel(x_ref, g_ref, sc_ref, sh_ref, wa_ref, wb_ref, bg_ref,
                urg_ref, z_ref, us5_ref, gt_ref, hn_ref, *, n_rg, n_cv, n_s5):
    j = pl.program_id(1)

    @pl.when(j == 0)
    def _():
        hn_ref[...] = _norm_mod(x_ref[...], g_ref[...], sc_ref[0], sh_ref[0]).astype(hn_ref.dtype)

    hn = hn_ref[...]
    a = jnp.dot(hn, wa_ref[...], preferred_element_type=F32)

    @pl.when(j < n_rg)
    def _():
        urg_ref[...] = a

    @pl.when((j >= n_rg) & (j < n_rg + n_cv))
    def _():
        b = jnp.dot(hn, wb_ref[...], preferred_element_type=F32)
        z_ref[...] = a * _sigmoid(b)

    @pl.when((j >= n_rg + n_cv) & (j < n_rg + n_cv + n_s5))
    def _():
        us5_ref[...] = a

    @pl.when(j >= n_rg + n_cv + n_s5)
    def _():
        gt_ref[...] = _sigmoid(a + bg_ref[...]).astype(gt_ref.dtype)


def _win(x, g, scale, shift, w_in, b_gate, *, rg, cv, s5, n_sample_rows, tm, tn):
    t, d = x.shape
    n_gate = b_gate.shape[-1]
    n_rg, n_cv, n_s5, n_g = rg // tn, cv // tn, s5 // tn, n_gate // tn
    ns = n_sample_rows // tm
    unit = lambda i, j: ((i >= ns).astype(jnp.int32), 0, 0)
    clip = lambda v, n: jnp.clip(v, 0, n - 1)
    kern = functools.partial(_win_kernel, n_rg=n_rg, n_cv=n_cv, n_s5=n_s5)
    return pl.pallas_call(
        kern,
        grid=(t // tm, n_rg + n_cv + n_s5 + n_g),
        in_specs=[
            pl.BlockSpec((tm, d), lambda i, j: (i, 0)),
            pl.BlockSpec((1, d), lambda i, j: (0, 0)),
            pl.BlockSpec((1, SUBLANES, d), unit),
            pl.BlockSpec((1, SUBLANES, d), unit),
            pl.BlockSpec((d, tn), lambda i, j: (0, jnp.where(j < n_rg + n_cv, j, j + n_cv))),
            pl.BlockSpec((d, tn), lambda i, j: (0, n_rg + n_cv + clip(j - n_rg, n_cv))),
            pl.BlockSpec((1, tn), lambda i, j: (0, clip(j - (n_rg + n_cv + n_s5), n_g))),
        ],
        out_specs=[
            pl.BlockSpec((tm, tn), lambda i, j: (i, clip(j, n_rg))),
            pl.BlockSpec((tm, tn), lambda i, j: (i, clip(j - n_rg, n_cv))),
            pl.BlockSpec((tm, tn), lambda i, j: (i, clip(j - n_rg - n_cv, n_s5))),
            pl.BlockSpec((tm, tn), lambda i, j: (i, clip(j - n_rg - n_cv - n_s5, n_g))),
        ],
        out_shape=[
            jax.ShapeDtypeStruct((t, rg), F32),
            jax.ShapeDtypeStruct((t, cv), F32),
            jax.ShapeDtypeStruct((t, s5), F32),
            jax.ShapeDtypeStruct((t, n_gate), MXU_DTYPE),
        ],
        scratch_shapes=[pltpu.VMEM((tm, d), MXU_DTYPE)],
        compiler_params=_params(2),
        name="win",
    )(x, g, scale, shift, w_in, w_in, b_gate)


def _rg_kernel(u_ref, cw_ref, cb_ref, wa_ref, wx_ref, ba_ref, bx_ref, lam_ref, h0_ref,
               y_ref, fin_ref, ext_ref, a_ref, b_ref, *, units, ch, kconv):
    rows = ch * SUBLANES
    pad_lo = (kconv // 2) * SUBLANES
    pad_hi = (kconv - 1 - kconv // 2) * SUBLANES
    t_rows = u_ref.shape[0]
    for d in (0, 1):
        neg_lam = -lam_ref[d:d + 1, :]
        softplus = jnp.maximum(neg_lam, 0.0) + jnp.log1p(jnp.exp(-jnp.abs(neg_lam)))
        coef = -RG_C * softplus
        wa = wa_ref[d, 0]
        wx = wx_ref[d, 0]
        ba = ba_ref[d:d + 1, :]
        bx = bx_ref[d:d + 1, :]
        for ui, (t0, nt, _, has_h0) in enumerate(units):
            nch = nt // ch

            def chunk_body(ci, h, d=d, t0=t0, nch=nch, coef=coef, wa=wa, wx=wx, ba=ba, bx=bx):
                c = ci if d == 0 else nch - 1 - ci
                r0 = pl.multiple_of((t0 + c * ch) * SUBLANES, SUBLANES)
                lo_start = pl.multiple_of(jnp.maximum(r0 - pad_lo, 0), SUBLANES)
                hi_start = pl.multiple_of(jnp.minimum(r0 + rows, t_rows - pad_hi), SUBLANES)
                ext_ref[0:pad_lo, :] = jnp.where(c > 0, u_ref[pl.ds(lo_start, pad_lo), :], 0.0)
                ext_ref[pad_lo:pad_lo + rows, :] = u_ref[pl.ds(r0, rows), :]
                ext_ref[pad_lo + rows:pad_lo + rows + pad_hi, :] = jnp.where(
                    c < nch - 1, u_ref[pl.ds(hi_start, pad_hi), :], 0.0)
                xc = cb_ref[...] + cw_ref[0:1, :] * ext_ref[0:rows, :]
                for k in range(1, kconv):
                    xc = xc + cw_ref[k:k + 1, :] * ext_ref[k * SUBLANES:k * SUBLANES + rows, :]
                r = _sigmoid(_mm(xc, wa) + ba)
                i = _sigmoid(_mm(xc, wx) + bx)
                a = jnp.exp(coef * r)
                a_ref[...] = a
                b_ref[...] = jnp.sqrt(1.0 - a * a) * i * xc

                def step(s, h):
                    tt = s if d == 0 else ch - 1 - s
                    o = pl.multiple_of(tt * SUBLANES, SUBLANES)
                    h = a_ref[pl.ds(o, SUBLANES), :] * h + b_ref[pl.ds(o, SUBLANES), :]
                    dst = pl.ds(pl.multiple_of(r0 + o, SUBLANES), SUBLANES)
                    if d == 0:
                        y_ref[dst, :] = h
                    else:
                        y_ref[dst, :] = y_ref[dst, :] + h
                    return h

                return lax.fori_loop(0, ch, step, h, unroll=8)

            h_init = h0_ref[d] if has_h0 else jnp.zeros((SUBLANES, LANES), F32)
            fin_ref[ui, d] = lax.fori_loop(0, nch, chunk_body, h_init)


def _rglru(u, conv_w, conv_b, wa, wx, ba, bx, lam, h0, *, units):
    t, rg = u.shape
    heads = rg // LANES
    kconv = conv_w.shape[0]
    n_units = len(units)
    rows = CHUNK_TILES * SUBLANES
    kern = functools.partial(_rg_kernel, units=units, ch=CHUNK_TILES, kconv=kconv)
    col = lambda h: (0, h)
    return pl.pallas_call(
        kern,
        grid=(heads,),
        in_specs=[
            pl.BlockSpec((t, LANES), col),
            pl.BlockSpec((kconv, LANES), col),
            pl.BlockSpec((1, LANES), col),
            pl.BlockSpec((2, 1, LANES, LANES), lambda h: (0, h, 0, 0)),
            pl.BlockSpec((2, 1, LANES, LANES), lambda h: (0, h, 0, 0)),
            pl.BlockSpec((2, LANES), col),
            pl.BlockSpec((2, LANES), col),
            pl.BlockSpec((2, LANES), col),
            pl.BlockSpec((2, SUBLANES, LANES), lambda h: (0, 0, h)),
        ],
        out_specs=[
            pl.BlockSpec((t, LANES), col),
            pl.BlockSpec((n_units, 2, SUBLANES, LANES), lambda h: (0, 0, 0, h)),
        ],
        out_shape=[
            jax.ShapeDtypeStruct((t, rg), F32),
            jax.ShapeDtypeStruct((n_units, 2, SUBLANES, rg), F32),
        ],
        scratch_shapes=[
            pltpu.VMEM((rows + (kconv - 1) * SUBLANES, LANES), F32),
            pltpu.VMEM((rows, LANES), F32),
            pltpu.VMEM((rows, LANES), F32),
        ],
        compiler_params=_params(1),
        name="rglru",
    )(u, conv_w, conv_b, wa, wx, ba, bx, lam, h0)


def _cv_kernel(zp_ref, zc_ref, zn_ref, w_ref, db_ref, lg_ref, lb_ref, o_ref, ext_ref, acc_ref,
               *, n_sample_chunks, ctx_unit_chunks, kc):
    i = pl.program_id(0)
    rows, c = zc_ref.shape
    hr = CONV_HALO_TILES * SUBLANES
    group = SUBLANES
    is_ctx = i >= n_sample_chunks
    cpos = (i - n_sample_chunks) % ctx_unit_chunks
    lo_ok = is_ctx & (cpos > 0)
    hi_ok = is_ctx & (cpos < ctx_unit_chunks - 1)
    ext_ref[0:hr, :] = jnp.where(lo_ok, zp_ref[rows - hr:rows, :], 0.0)
    ext_ref[hr:hr + rows, :] = zc_ref[...]
    ext_ref[hr + rows:hr + rows + hr, :] = jnp.where(hi_ok, zn_ref[0:hr, :], 0.0)
    first_tap_tile = CONV_HALO_TILES - kc // 2

    def lane_body(lb, carry):
        l0 = pl.multiple_of(lb * LANES, LANES)

        def grp_body(g, carry):
            acc = jnp.zeros((group, SUBLANES, LANES), F32)
            for k in range(kc):
                src = pl.multiple_of((g * group + k + first_tap_tile) * SUBLANES, SUBLANES)
                e = ext_ref[pl.ds(src, group * SUBLANES), pl.ds(l0, LANES)]
                acc = acc + e.reshape(group, SUBLANES, LANES) * w_ref[k, :, pl.ds(l0, LANES)][None]
            dst = pl.multiple_of(g * group * SUBLANES, SUBLANES)
            acc_ref[pl.ds(dst, group * SUBLANES), pl.ds(l0, LANES)] = acc.reshape(group * SUBLANES, LANES)
            return carry

        return lax.fori_loop(0, rows // (group * SUBLANES), grp_body, carry)

    lax.fori_loop(0, c // LANES, lane_body, 0)
    z = acc_ref[...] + db_ref[...]
    mu = jnp.mean(z, axis=-1, keepdims=True)
    zc = z - mu
    var = jnp.mean(zc * zc, axis=-1, keepdims=True)
    y = zc * lax.rsqrt(var + EPS) * lg_ref[...] + lb_ref[...]
    o_ref[...] = _silu(y).astype(o_ref.dtype)


def _conv_branch(z, dw_tiles, db, ln_g, ln_b, *, n_sample_chunks, ctx_unit_chunks):
    t, c = z.shape
    kc = dw_tiles.shape[0]
    rows = CHUNK_TILES * SUBLANES
    n = t // rows
    kern = functools.partial(_cv_kernel, n_sample_chunks=n_sample_chunks,
                             ctx_unit_chunks=ctx_unit_chunks, kc=kc)
    vec = pl.BlockSpec((1, c), lambda i: (0, 0))
    return pl.pallas_call(
        kern,
        grid=(n,),
        in_specs=[
            pl.BlockSpec((rows, c), lambda i: (jnp.maximum(i - 1, 0), 0)),
            pl.BlockSpec((rows, c), lambda i: (i, 0)),
            pl.BlockSpec((rows, c), lambda i: (jnp.minimum(i + 1, n - 1), 0)),
            pl.BlockSpec((kc, SUBLANES, c), lambda i: (0, 0, 0)),
            vec, vec, vec,
        ],
        out_specs=pl.BlockSpec((rows, c), lambda i: (i, 0)),
        out_shape=jax.ShapeDtypeStruct((t, c), MXU_DTYPE),
        scratch_shapes=[
            pltpu.VMEM((rows + 2 * CONV_HALO_TILES * SUBLANES, c), F32),
            pltpu.VMEM((rows, c), F32),
        ],
        compiler_params=_params(1),
        name="conv",
    )(z, z, z, dw_tiles, db, ln_g, ln_b)


def _s5_kernel(u_ref, dsk_ref, are_ref, aim_ref, bm_ref, cm_ref, h0_ref,
               y_ref, fin_ref, lhs_ref, hs_ref, yc_ref, *, units, tc, grid_rows):
    ns = are_ref.shape[-1]

    def tile_of(unit, c, s):
        t0, _, is_grid, _ = unit
        if is_grid:
            return t0 + (s % grid_rows) * GRID_W + c * (tc // grid_rows) + s // grid_rows
        return t0 + c * tc + s

    for d in (0, 1):
        a_re = jnp.broadcast_to(are_ref[d, 0], (SUBLANES, ns))
        a_im = jnp.broadcast_to(aim_ref[d, 0], (SUBLANES, ns))
        for ui, unit in enumerate(units):
            nch = unit[1] // tc

            def chunk_body(ci, h, d=d, unit=unit, nch=nch, a_re=a_re, a_im=a_im):
                c = ci if d == 0 else nch - 1 - ci
                for s in range(tc):
                    r = pl.multiple_of(tile_of(unit, c, s) * SUBLANES, SUBLANES)
                    lhs_ref[s * SUBLANES:(s + 1) * SUBLANES, :] = u_ref[pl.ds(r, SUBLANES), :]
                hs_ref[...] = _mm(lhs_ref[...], bm_ref[d, 0])

                def step(si, h):
                    s = si if d == 0 else tc - 1 - si
                    o = pl.ds(pl.multiple_of(s * SUBLANES, SUBLANES), SUBLANES)
                    h_re, h_im = h
                    n_re = a_re * h_re - a_im * h_im + hs_ref[o, 0:ns]
                    n_im = a_re * h_im + a_im * h_re + hs_ref[o, ns:2 * ns]
                    hs_ref[o, 0:ns] = n_re
                    hs_ref[o, ns:2 * ns] = n_im
                    return n_re, n_im

                h = lax.fori_loop(0, tc, step, h, unroll=4)
                yc_ref[...] = _mm(hs_ref[...], cm_ref[d, 0])
                for s in range(tc):
                    r = pl.multiple_of(tile_of(unit, c, s) * SUBLANES, SUBLANES)
                    sl = slice(s * SUBLANES, (s + 1) * SUBLANES)
                    if d == 0:
                        y_ref[pl.ds(r, SUBLANES), :] = dsk_ref[...] * lhs_ref[sl, :] + yc_ref[sl, :]
                    else:
                        y_ref[pl.ds(r, SUBLANES), :] = y_ref[pl.ds(r, SUBLANES), :] + yc_ref[sl, :]
                return h

            if unit[3]:
                h_init = (h0_ref[d, 0, :, 0:ns], h0_ref[d, 0, :, ns:2 * ns])
            else:
                h_init = (jnp.zeros((SUBLANES, ns), F32), jnp.zeros((SUBLANES, ns), F32))
            f_re, f_im = lax.fori_loop(0, nch, chunk_body, h_init)
            fin_ref[ui, d, 0, :, 0:ns] = f_re
            fin_ref[ui, d, 0, :, ns:2 * ns] = f_im


def _s5(u, dskip, a_re, a_im, bmat, cmat, h0, *, units, tc, grid_rows):
    t, width = u.shape
    nb = width // LANES
    ns = a_re.shape[-1]
    n_units = len(units)
    kern = functools.partial(_s5_kernel, units=units, tc=tc, grid_rows=grid_rows)
    blk4 = lambda *shape: pl.BlockSpec((2, 1) + shape, lambda j: (0, j, 0, 0))
    return pl.pallas_call(
        kern,
        grid=(nb,),
        in_specs=[
            pl.BlockSpec((t, LANES), lambda j: (0, j)),
            pl.BlockSpec((1, LANES), lambda j: (0, j)),
            blk4(1, ns), blk4(1, ns),
            blk4(LANES, 2 * ns), blk4(2 * ns, LANES),
            blk4(SUBLANES, 2 * ns),
        ],
        out_specs=[
            pl.BlockSpec((t, LANES), lambda j: (0, j)),
            pl.BlockSpec((n_units, 2, 1, SUBLANES, 2 * ns), lambda j: (0, 0, j, 0, 0)),
        ],
        out_shape=[
            jax.ShapeDtypeStruct((t, width), F32),
            jax.ShapeDtypeStruct((n_units, 2, nb, SUBLANES, 2 * ns), F32),
        ],
        scratch_shapes=[
            pltpu.VMEM((tc * SUBLANES, LANES), F32),
            pltpu.VMEM((tc * SUBLANES, 2 * ns), F32),
            pltpu.VMEM((tc * SUBLANES, LANES), F32),
        ],
        compiler_params=_params(1),
        name="s5",
    )(u, dskip, a_re, a_im, bmat, cmat, h0)


def _merge_kernel(yrg_ref, zc_ref, ys5_ref, g0_ref, g1_ref, g2_ref, wrg_ref, wcv_ref, wsv_ref,
                  wsg_ref, wo_ref, x_ref, gate_ref, o_ref, acc_ref, rgb_ref, s5b_ref):
    j = pl.program_id(1)

    @pl.when(j == 0)
    def _():
        acc_ref[...] = jnp.zeros_like(acc_ref)
        rgb_ref[...] = yrg_ref[...].astype(rgb_ref.dtype)
        s5b_ref[...] = ys5_ref[...].astype(s5b_ref.dtype)

    dot = lambda a, b: jnp.dot(a, b, preferred_element_type=F32)
    br_rg = dot(rgb_ref[...], wrg_ref[...])
    br_cv = dot(zc_ref[...], wcv_ref[...])
    s5b = s5b_ref[...]
    br_s5 = dot(s5b, wsv_ref[...]) * _sigmoid(dot(s5b, wsg_ref[...]))
    merged = (g0_ref[...].astype(F32) * br_rg + g1_ref[...].astype(F32) * br_cv
              + g2_ref[...].astype(F32) * br_s5)
    acc_ref[...] += dot(merged.astype(MXU_DTYPE), wo_ref[...])

    @pl.when(j == pl.num_programs(1) - 1)
    def _():
        o_ref[...] = _gated_residual(x_ref[...], gate_ref[0], acc_ref[...])


def _merge(y_rg, zc, y_s5, gates, rg_out, cv_out, s5_glu, w_out, x, gate, *, n_sample_rows, tm, tk):
    t, d = x.shape
    rg, cv, s5 = y_rg.shape[1], zc.shape[1], y_s5.shape[1]
    nk = d // tk
    ns = n_sample_rows // tm
    row = lambda w: pl.BlockSpec((tm, w), lambda i, j: (i, 0))
    gspec = lambda k: pl.BlockSpec((tm, tk), lambda i, j: (i, k * nk + j))
    wcol = lambda rows, off: pl.BlockSpec((rows, tk), lambda i, j: (0, off + j))
    return pl.pallas_call(
        _merge_kernel,
        grid=(t // tm, nk),
        in_specs=[
            row(rg), row(cv), row(s5),
            gspec(0), gspec(1), gspec(2),
            wcol(rg, 0), wcol(cv, 0), wcol(s5, 0), wcol(s5, nk),
            pl.BlockSpec((tk, d), lambda i, j: (j, 0)),
            row(d),
            pl.BlockSpec((1, SUBLANES, d), lambda i, j: ((i >= ns).astype(jnp.int32), 0, 0)),
        ],
        out_specs=row(d),
        out_shape=jax.ShapeDtypeStruct((t, d), F32),
        scratch_shapes=[
            pltpu.VMEM((tm, d), F32),
            pltpu.VMEM((tm, rg), MXU_DTYPE),
            pltpu.VMEM((tm, s5), MXU_DTYPE),
        ],
        compiler_params=_params(2),
        name="merge",
    )(y_rg, zc, y_s5, gates, gates, gates, rg_out, cv_out, s5_glu, s5_glu, w_out, x, gate)


def _pack_bf16_pairs(v):
    half = v.shape[1] // 2
    bits = lambda a: lax.bitcast_convert_type(a.astype(jnp.bfloat16).astype(F32), jnp.uint32)
    return (bits(v[:, :half]) >> 16) | (bits(v[:, half:]) & jnp.uint32(0xFFFF0000))


def _unpack_bf16_pairs(p):
    lo = lax.bitcast_convert_type(p << 16, F32)
    hi = lax.bitcast_convert_type(p & jnp.uint32(0xFFFF0000), F32)
    return lo, hi


def _router_kernel(x_ref, g_ref, sc_ref, sh_ref, rw_ref, rb_ref,
                   hnp_ref, eidx_ref, rank_ref, wtok_ref, cnt_ref, run_ref):
    @pl.when(pl.program_id(0) == 0)
    def _():
        run_ref[...] = jnp.zeros_like(run_ref)

    hn = _norm_mod(x_ref[...], g_ref[...], sc_ref[0], sh_ref[0])
    hnp_ref[...] = _pack_bf16_pairs(hn)
    logits = lax.dot_general(rw_ref[...], hn, (((1,), (1,)), ((), ())),
                             precision=lax.Precision.HIGHEST, preferred_element_type=F32)
    s = _sigmoid(logits)
    choice = s + rb_ref[...]
    n_exp, tm = choice.shape
    gsize = n_exp // N_ROUTE_GROUPS
    neg_inf = jnp.float32(-jnp.inf)
    c3 = choice.reshape(N_ROUTE_GROUPS, gsize, tm)
    sub = lax.broadcasted_iota(jnp.int32, c3.shape, 1)
    m1 = jnp.max(c3, axis=1, keepdims=True)
    i1 = jnp.min(jnp.where(c3 == m1, sub, gsize), axis=1, keepdims=True)
    m2 = jnp.max(jnp.where(sub == i1, neg_inf, c3), axis=1, keepdims=True)
    gscore = jnp.broadcast_to(m1 + m2, c3.shape)
    gidx = lax.broadcasted_iota(jnp.int32, c3.shape, 0)
    beaten = jnp.zeros(c3.shape, jnp.int32)
    for gp in range(N_ROUTE_GROUPS):
        other = gscore[gp:gp + 1]
        wins = (other > gscore) | ((other == gscore) & (gidx > gp))
        beaten = beaten + wins.astype(jnp.int32)
    masked = jnp.where(beaten < TOPK_GROUPS, c3, neg_inf).reshape(n_exp, tm)
    eidx = lax.broadcasted_iota(jnp.int32, masked.shape, 0)
    beaten = jnp.zeros(masked.shape, jnp.int32)
    for ep in range(n_exp):
        other = masked[ep:ep + 1, :]
        wins = (other > masked) | ((other == masked) & (eidx > ep))
        beaten = beaten + wins.astype(jnp.int32)
    sel = beaten < TOP_K
    w = jnp.where(sel, s, 0.0)
    comb = ROUTE_SCALE * w / jnp.sum(w, axis=0, keepdims=True)
    self32 = sel.astype(F32)
    before = (lax.broadcasted_iota(jnp.int32, (tm, tm), 0) < lax.broadcasted_iota(jnp.int32, (tm, tm), 1))
    rank = run_ref[...] + _mm(self32, before.astype(F32))
    run_ref[...] = run_ref[...] + jnp.sum(self32, axis=1, keepdims=True)
    cnt_ref[...] = jnp.broadcast_to(run_ref[...], cnt_ref.shape)
    eidx_f = eidx.astype(F32)
    w_rows = []
    for j in range(TOP_K):
        m = beaten == j
        pick = lambda v: jnp.sum(jnp.where(m, v, 0.0), axis=0, keepdims=True)
        eidx_ref[j:j + 1, :] = pick(eidx_f).astype(jnp.int32)
        rank_ref[j:j + 1, :] = pick(rank).astype(jnp.int32)
        w_rows.append(pick(comb))
    w_rows.append(jnp.zeros((wtok_ref.shape[1] - TOP_K, tm), F32))
    wtok_ref[...] = jnp.concatenate(w_rows, axis=0).T


def _router(x, g, scale, shift, router_wt, router_b, *, n_sample_rows, tm):
    t, d = x.shape
    n_exp = router_wt.shape[0]
    ns = n_sample_rows // tm
    unit = lambda i: ((i >= ns).astype(jnp.int32), 0, 0)
    return pl.pallas_call(
        _router_kernel,
        grid=(t // tm,),
        in_specs=[
            pl.BlockSpec((tm, d), lambda i: (i, 0)),
            pl.BlockSpec((1, d), lambda i: (0, 0)),
            pl.BlockSpec((1, SUBLANES, d), unit),
            pl.BlockSpec((1, SUBLANES, d), unit),
            pl.BlockSpec((n_exp, d), lambda i: (0, 0)),
            pl.BlockSpec((n_exp, 1), lambda i: (0, 0)),
        ],
        out_specs=[
            pl.BlockSpec((tm, d // 2), lambda i: (i, 0)),
            pl.BlockSpec((TOP_K, tm), lambda i: (0, i)),
            pl.BlockSpec((TOP_K, tm), lambda i: (0, i)),
            pl.BlockSpec((tm, LANES), lambda i: (i, 0)),
            pl.BlockSpec((n_exp, LANES), lambda i: (0, 0)),
        ],
        out_shape=[
            jax.ShapeDtypeStruct((t, d // 2), jnp.uint32),
            jax.ShapeDtypeStruct((TOP_K, t), jnp.int32),
            jax.ShapeDtypeStruct((TOP_K, t), jnp.int32),
            jax.ShapeDtypeStruct((t, LANES), F32),
            jax.ShapeDtypeStruct((n_exp, LANES), F32),
        ],
        scratch_shapes=[pltpu.VMEM((n_exp, 1), F32)],
        compiler_params=_params(1),
        name="router",
    )(x, g, scale, shift, router_wt, router_b)


def _swiglu(lo, hi, w1, w3, w2):
    half = lo.shape[1]
    lo = lo.astype(MXU_DTYPE)
    hi = hi.astype(MXU_DTYPE)
    dot = lambda a, b: jnp.dot(a, b, preferred_element_type=F32)
    h1 = dot(lo, w1[0:half, :]) + dot(hi, w1[half:, :])
    h3 = dot(lo, w3[0:half, :]) + dot(hi, w3[half:, :])
    return dot((_silu(h1) * h3).astype(MXU_DTYPE), w2[...])


def _dispatch_kernel(pos_ref, hn_ref, xs_in_hbm, xs_hbm, sem):
    del xs_in_hbm
    n_slots, tm = pos_ref.shape

    def issue(t, carry):
        src = hn_ref.at[pl.ds(t, 1)]
        for j in range(n_slots):
            pltpu.make_async_copy(src, xs_hbm.at[pl.ds(pos_ref[j, t], 1)], sem).start()
        return carry

    lax.fori_loop(0, tm, issue, 0)
    for j in range(n_slots):
        pltpu.make_async_copy(hn_ref, xs_hbm.at[pl.ds(0, tm)], sem).wait()


def _dispatch(pos, hnp, xs_init, *, tm):
    n_slots, t = pos.shape
    return pl.pallas_call(
        _dispatch_kernel,
        grid=(t // tm,),
        in_specs=[
            pl.BlockSpec((n_slots, tm), lambda i: (0, i), memory_space=pltpu.SMEM),
            pl.BlockSpec((tm, hnp.shape[1]), lambda i: (i, 0)),
            pl.BlockSpec(memory_space=pl.ANY),
        ],
        out_specs=pl.BlockSpec(memory_space=pl.ANY),
        out_shape=jax.ShapeDtypeStruct(xs_init.shape, xs_init.dtype),
        scratch_shapes=[pltpu.SemaphoreType.DMA],
        input_output_aliases={2: 0},
        compiler_params=_params(1),
        name="dispatch",
    )(pos, hnp, xs_init)


def _ffn_kernel(te_ref, nu_ref, xs_ref, w1_ref, w3_ref, w2_ref, ys_ref, w1b_ref, w3b_ref, w2b_ref):
    i = pl.program_id(0)

    @pl.when((i == 0) | (te_ref[i] != te_ref[jnp.maximum(i - 1, 0)]))
    def _():
        w1b_ref[...] = w1_ref[0].astype(w1b_ref.dtype)
        w3b_ref[...] = w3_ref[0].astype(w3b_ref.dtype)
        w2b_ref[...] = w2_ref[0].astype(w2b_ref.dtype)

    @pl.when(i < nu_ref[0])
    def _():
        lo, hi = _unpack_bf16_pairs(xs_ref[...])
        ys_ref[...] = _pack_bf16_pairs(_swiglu(lo, hi, w1b_ref, w3b_ref, w2b_ref))

    @pl.when(i >= nu_ref[0])
    def _():
        ys_ref[...] = jnp.zeros_like(ys_ref)


def _ffn(tile_expert, n_used, xs, w1, w3, w2, *, tg):
    s_pad, half = xs.shape
    _, d, f = w1.shape
    row = lambda i, te, nu: (jnp.maximum(jnp.minimum(i, nu[0] - 1), 0), 0)
    wblk = lambda i, te, nu: (te[i], 0, 0)
    grid_spec = pltpu.PrefetchScalarGridSpec(
        num_scalar_prefetch=2,
        grid=(s_pad // tg,),
        in_specs=[
            pl.BlockSpec((tg, half), row),
            pl.BlockSpec((1, d, f), wblk),
            pl.BlockSpec((1, d, f), wblk),
            pl.BlockSpec((1, f, d), wblk),
        ],
        out_specs=pl.BlockSpec((tg, half), lambda i, te, nu: (i, 0)),
        scratch_shapes=[
            pltpu.VMEM((d, f), MXU_DTYPE),
            pltpu.VMEM((d, f), MXU_DTYPE),
            pltpu.VMEM((f, d), MXU_DTYPE),
        ],
    )
    return pl.pallas_call(
        _ffn_kernel,
        grid_spec=grid_spec,
        out_shape=jax.ShapeDtypeStruct((s_pad, half), jnp.uint32),
        compiler_params=_params(1),
        name="ffn",
    )(tile_expert, n_used, xs, w1, w3, w2)


def _combine_kernel(pos_ref, hnp_ref, wtok_ref, s1_ref, s3_ref, s2_ref, x_ref, gate_ref, ys_hbm,
                    o_ref, buf_ref, sem):
    n_slots, tm = pos_ref.shape

    def issue(t, carry):
        for j in range(n_slots):
            pltpu.make_async_copy(ys_hbm.at[pl.ds(pos_ref[j, t], 1)],
                                  buf_ref.at[j, pl.ds(t, 1)], sem).start()
        return carry

    lax.fori_loop(0, tm, issue, 0)
    lo, hi = _unpack_bf16_pairs(hnp_ref[...])
    moe = _swiglu(lo, hi, s1_ref, s3_ref, s2_ref)
    half = lo.shape[1]
    acc_lo = moe[:, :half]
    acc_hi = moe[:, half:]
    wtok = wtok_ref[...]
    for j in range(n_slots):
        pltpu.make_async_copy(ys_hbm.at[pl.ds(0, tm)], buf_ref.at[j], sem).wait()
    for j in range(n_slots):
        y_lo, y_hi = _unpack_bf16_pairs(buf_ref[j])
        wj = wtok[:, j:j + 1]
        acc_lo = acc_lo + wj * y_lo
        acc_hi = acc_hi + wj * y_hi
    x = x_ref[...]
    gate = gate_ref[0]
    o_ref[:, :half] = _gated_residual(x[:, :half], gate[:, :half], acc_lo)
    o_ref[:, half:] = _gated_residual(x[:, half:], gate[:, half:], acc_hi)


def _combine(pos, hnp, wtok, s1, s3, s2, x, gate, ys, *, n_sample_rows, tm):
    t, d = x.shape
    n_slots = pos.shape[0]
    f = s1.shape[1]
    ns = n_sample_rows // tm
    full = lambda shape: pl.BlockSpec(shape, lambda i: (0,) * len(shape))
    return pl.pallas_call(
        _combine_kernel,
        grid=(t // tm,),
        in_specs=[
            pl.BlockSpec((n_slots, tm), lambda i: (0, i), memory_space=pltpu.SMEM),
            pl.BlockSpec((tm, d // 2), lambda i: (i, 0)),
            pl.BlockSpec((tm, LANES), lambda i: (i, 0)),
            full((d, f)), full((d, f)), full((f, d)),
            pl.BlockSpec((tm, d), lambda i: (i, 0)),
            pl.BlockSpec((1, SUBLANES, d), lambda i: ((i >= ns).astype(jnp.int32), 0, 0)),
            pl.BlockSpec(memory_space=pl.ANY),
        ],
        out_specs=pl.BlockSpec((tm, d), lambda i: (i, 0)),
        out_shape=jax.ShapeDtypeStruct((t, d), F32),
        scratch_shapes=[
            pltpu.VMEM((n_slots, tm, d // 2), jnp.uint32),
            pltpu.SemaphoreType.DMA,
        ],
        compiler_params=_params(1),
        name="combine",
    )(pos, hnp, wtok, s1, s3, s2, x, gate, ys)


def _slots_kernel(cnt_ref, eidx_ref, rank_ref, pos_ref, te_ref, nu_ref, *, tg):
    counts = cnt_ref[...]
    n_exp = counts.shape[0]
    padded = jnp.floor((counts + (tg - 1)) * (1.0 / tg)) * tg
    incl = (lax.broadcasted_iota(jnp.int32, (n_exp, n_exp), 0)
            >= lax.broadcasted_iota(jnp.int32, (n_exp, n_exp), 1)).astype(F32)
    ends = jnp.dot(incl, padded, precision=lax.Precision.HIGHEST, preferred_element_type=F32)
    ends_col = ends[:, 0:1]
    offs_col = ends_col - padded[:, 0:1]
    eidx = eidx_ref[...]
    rank = rank_ref[...]
    expert = lax.broadcasted_iota(jnp.int32, (n_exp, eidx.shape[1]), 0)
    for j in range(eidx.shape[0]):
        off = jnp.sum(jnp.where(expert == eidx[j:j + 1, :], offs_col, 0.0), axis=0, keepdims=True)
        pos_ref[j:j + 1, :] = off.astype(jnp.int32) + rank[j:j + 1, :]
    n_used = ends[n_exp - 1:n_exp, 0:1] * (1.0 / tg)
    tile = lax.broadcasted_iota(jnp.int32, te_ref.shape, 1).astype(F32)
    tile_row = jnp.minimum(tile, n_used - 1.0) * tg
    te_ref[...] = jnp.sum((ends_col <= tile_row).astype(F32), axis=0, keepdims=True).astype(jnp.int32)
    nu_ref[...] = jnp.broadcast_to(n_used, nu_ref.shape).astype(jnp.int32)


def _slots(counts, eidx, rank, *, tg, n_tiles, tb):
    assert tg & (tg - 1) == 0
    n_slots, t = eidx.shape
    n_tiles_pad = pl.cdiv(n_tiles, LANES) * LANES
    pos, te, nu = pl.pallas_call(
        functools.partial(_slots_kernel, tg=tg),
        grid=(t // tb,),
        in_specs=[
            pl.BlockSpec(counts.shape, lambda i: (0, 0)),
            pl.BlockSpec((n_slots, tb), lambda i: (0, i)),
            pl.BlockSpec((n_slots, tb), lambda i: (0, i)),
        ],
        out_specs=[
            pl.BlockSpec((n_slots, tb), lambda i: (0, i)),
            pl.BlockSpec((1, n_tiles_pad), lambda i: (0, 0)),
            pl.BlockSpec((1, LANES), lambda i: (0, 0)),
        ],
        out_shape=[
            jax.ShapeDtypeStruct((n_slots, t), jnp.int32),
            jax.ShapeDtypeStruct((1, n_tiles_pad), jnp.int32),
            jax.ShapeDtypeStruct((1, LANES), jnp.int32),
        ],
        compiler_params=_params(1),
        name="slots",
    )(counts, eidx, rank)
    return pos, te[0, :n_tiles], nu[0, :1]


def _final_norm_kernel(x_ref, g_ref, os_ref, oc_ref, y_ref, *, n_sample_tiles):
    x = x_ref[...]
    ms = jnp.mean(x * x, axis=-1, keepdims=True)
    y = x * lax.rsqrt(ms + EPS) * g_ref[...]
    steps = x.shape[0] // SUBLANES
    n_lane_blocks = x.shape[1] // LANES
    for k in range(n_lane_blocks):
        y_ref[k] = y[:, k * LANES:(k + 1) * LANES]

    def emit(o_ref):
        for b in range(SUBLANES):
            for k in range(n_lane_blocks):
                o_ref[b, :, k * LANES:(k + 1) * LANES] = y_ref[k, pl.ds(b, steps, stride=SUBLANES), :]

    @pl.when(pl.program_id(0) < n_sample_tiles)
    def _():
        emit(os_ref)

    @pl.when(pl.program_id(0) >= n_sample_tiles)
    def _():
        emit(oc_ref)


def _final_norm(x, g, *, ls, lc, n_ctx_units, tm):
    t, d = x.shape
    steps = tm // SUBLANES
    nst = ls // steps
    ctx_tiles = lc // steps
    n_ctx = n_ctx_units * ctx_tiles

    def ctx_block(i):
        k = jnp.clip(i - nst, 0, n_ctx - 1)
        return (k // ctx_tiles, k % ctx_tiles, 0)

    return pl.pallas_call(
        functools.partial(_final_norm_kernel, n_sample_tiles=nst),
        grid=(t // tm,),
        in_specs=[pl.BlockSpec((tm, d), lambda i: (i, 0)), pl.BlockSpec((1, d), lambda i: (0, 0))],
        out_specs=[
            pl.BlockSpec((SUBLANES, steps, d), lambda i: (0, jnp.minimum(i, nst - 1), 0)),
            pl.BlockSpec((SUBLANES, steps, d), ctx_block),
        ],
        out_shape=[
            jax.ShapeDtypeStruct((SUBLANES, ls, d), F32),
            jax.ShapeDtypeStruct((n_ctx_units * SUBLANES, lc, d), F32),
        ],
        scratch_shapes=[pltpu.VMEM((d // LANES, tm, LANES), F32)],
        compiler_params=_params(1),
        name="final_norm",
    )(x, g)


def _s5_discretise(lam_re, lam_im, log_dt, b_re, b_im, c_re, c_im):
    two, g, p = lam_re.shape
    s = b_re.shape[-1]
    gb = LANES // s
    nb = g // gb
    dt = jnp.exp(log_dt)[..., None]
    mag = jnp.exp(lam_re * dt)
    a_re = mag * jnp.cos(lam_im * dt)
    a_im = mag * jnp.sin(lam_im * dt)
    den = lam_re * lam_re + lam_im * lam_im
    q_re = ((a_re - 1.0) * lam_re + a_im * lam_im) / den
    q_im = (a_im * lam_re - (a_re - 1.0) * lam_im) / den
    bb_re = q_re[..., None] * b_re - q_im[..., None] * b_im
    bb_im = q_re[..., None] * b_im + q_im[..., None] * b_re
    eye = jnp.eye(gb, dtype=F32)

    def bdiag_b(m):
        m = m.reshape(two, nb, gb, p, s)
        return jnp.einsum("dbgps,gh->dbgshp", m, eye).reshape(two, nb, gb * s, gb * p)

    def bdiag_c(m):
        m = m.reshape(two, nb, gb, s, p)
        return jnp.einsum("dbgsp,gh->dbgphs", m, eye).reshape(two, nb, gb * p, gb * s)

    bmat = jnp.concatenate([bdiag_b(bb_re), bdiag_b(bb_im)], axis=-1)
    cmat = jnp.concatenate([bdiag_c(c_re), -bdiag_c(c_im)], axis=-2)
    blk = lambda a: a.reshape(two, nb, 1, gb * p)
    return blk(a_re), blk(a_im), bmat.astype(MXU_DTYPE), cmat.astype(MXU_DTYPE)


def kernel(x_prompt, x_sample, state_rglru, state_s5, c, c_ctx, norm_mix, norm_ffn, ada_w, ada_b, w_in, b_gate, rg_conv_w, rg_conv_b, rg_wa, rg_ba, rg_wx, rg_bx, rg_lambda, rg_out, cv_dw, cv_db, cv_ln_g, cv_ln_b, cv_out, s5_lambda_re, s5_lambda_im, s5_log_dt, s5_b_re, s5_b_im, s5_c_re, s5_c_im, s5_d, s5_glu, w_out, router_w, router_b, exp_w1, exp_w3, exp_w2, sh_w1, sh_w3, sh_w2, norm_final):
    batch, lc, d = x_prompt.shape
    dec_batch, ls, _ = x_sample.shape
    depth = w_in.shape[0]
    rg = rg_out.shape[1]
    cv = cv_out.shape[1]
    s5 = s5_glu.shape[1]
    n_groups_s5, n_state = s5_lambda_re.shape[2], s5_lambda_re.shape[3]
    assert dec_batch == SUBLANES and batch % SUBLANES == 0
    assert ls % GRID_W == 0 and lc % CHUNK_TILES == 0
    assert rg_wa.shape[-1] == LANES
    n_ctx_units = batch // SUBLANES
    grid_rows = ls // GRID_W
    n_sample_rows = ls * SUBLANES
    units = ((0, ls, True, True),) + tuple(
        (ls + k * lc, lc, False, False) for k in range(n_ctx_units))
    tm = 512
    tn = min(1024, rg, cv, s5)
    tk = min(512, d)
    tc = min(128, lc)
    assert tc % grid_rows == 0 and (GRID_W * grid_rows) % tc == 0
    n_tokens = (ls + n_ctx_units * lc) * SUBLANES
    n_exp = router_w.shape[-1]
    tg = 256
    tmc = 256
    n_ffn_tiles = pl.cdiv(n_tokens * TOP_K, tg) + n_exp

    xs = jnp.transpose(x_sample, (1, 0, 2)).reshape(ls * SUBLANES, d)
    xc = jnp.transpose(x_prompt.reshape(n_ctx_units, SUBLANES, lc, d), (0, 2, 1, 3))
    x = jnp.concatenate([xs, xc.reshape(n_ctx_units * lc * SUBLANES, d)], axis=0)

    cond = jnp.concatenate([c, c_ctx[None], jnp.zeros((SUBLANES - 1, d), F32)], axis=0)
    mod = _ada(cond, ada_w, ada_b)
    mod = jnp.stack([mod[:, :SUBLANES],
                     jnp.broadcast_to(mod[:, SUBLANES:SUBLANES + 1], (depth, SUBLANES, 6 * d))], axis=1)
    mod = mod.reshape(depth, 2, SUBLANES, 6, d)

    cast = lambda a: a.astype(MXU_DTYPE)
    sorted_rows = jnp.zeros((n_ffn_tiles * tg, d // 2), jnp.uint32)
    rg_fin, s5_fin = [], []
    for l in range(depth):
        shift1, scale1, gate1, shift2, scale2, gate2 = (mod[l, :, :, k] for k in range(6))
        u_rg, z_cv, u_s5, gates = _win(
            x, norm_mix[l][None], scale1, shift1, cast(w_in[l]), b_gate[l][None],
            rg=rg, cv=cv, s5=s5, n_sample_rows=n_sample_rows, tm=tm, tn=tn)

        h0_rg = jnp.transpose(state_rglru[:, l], (1, 0, 2))
        heads = rg // LANES
        y_rg, fin_rg = _rglru(
            u_rg, rg_conv_w[l], rg_conv_b[l][None],
            cast(rg_wa[l]), cast(rg_wx[l]), rg_ba[l], rg_bx[l], rg_lambda[l], h0_rg, units=units)
        rg_fin.append(fin_rg)

        dw_tiles = jnp.broadcast_to(cv_dw[l][:, None, :], (cv_dw.shape[1], SUBLANES, cv))
        zc = _conv_branch(z_cv, dw_tiles, cv_db[l][None], cv_ln_g[l][None], cv_ln_b[l][None],
                          n_sample_chunks=ls // CHUNK_TILES, ctx_unit_chunks=lc // CHUNK_TILES)

        a_re, a_im, bmat, cmat = _s5_discretise(
            s5_lambda_re[l], s5_lambda_im[l], s5_log_dt[l], s5_b_re[l], s5_b_im[l],
            s5_c_re[l], s5_c_im[l])
        nb = s5 // LANES
        st = jnp.transpose(state_s5[:, l], (1, 4, 0, 2, 3))
        st = st.reshape(2, 2, SUBLANES, nb, (n_groups_s5 // nb) * n_state)
        h0_s5 = jnp.transpose(st, (0, 3, 2, 1, 4)).reshape(2, nb, SUBLANES, -1)
        y_s5, fin_s5 = _s5(u_s5, s5_d[l][None], a_re, a_im, bmat, cmat, h0_s5,
                           units=units, tc=tc, grid_rows=grid_rows)
        s5_fin.append(fin_s5)

        x = _merge(y_rg, zc, y_s5, gates, cast(rg_out[l]), cast(cv_out[l]), cast(s5_glu[l]),
                   cast(w_out[l]), x, gate1, n_sample_rows=n_sample_rows, tm=tm, tk=tk)

        hnp, eidx, rank, wtok, counts = _router(
            x, norm_ffn[l][None], scale2, shift2, router_w[l].T, router_b[l][:, None],
            n_sample_rows=n_sample_rows, tm=tm)
        pos, tile_expert, n_used = _slots(counts, eidx, rank, tg=tg, n_tiles=n_ffn_tiles, tb=tm)
        sorted_rows = _dispatch(pos, hnp, sorted_rows, tm=tm)
        ys = _ffn(tile_expert, n_used, sorted_rows, exp_w1[l], exp_w3[l], exp_w2[l], tg=tg)
        x = _combine(pos, hnp, wtok, cast(sh_w1[l]), cast(sh_w3[l]), cast(sh_w2[l]), x, gate2, ys,
                     n_sample_rows=n_sample_rows, tm=tmc)

    y_sample, y_prompt = _final_norm(x, norm_final[None], ls=ls, lc=lc, n_ctx_units=n_ctx_units, tm=tm)

    fr = jnp.stack(rg_fin, axis=0)[:, 1:]
    new_state_rglru = jnp.transpose(fr, (1, 3, 0, 2, 4)).reshape(batch, depth, 2, rg)
    fs = jnp.stack(s5_fin, axis=0)[:, 1:]
    nb = s5 // LANES
    fs = fs.reshape(depth, n_ctx_units, 2, nb, SUBLANES, 2, n_groups_s5 // nb, n_state)
    new_state_s5 = jnp.transpose(fs, (1, 4, 0, 2, 3, 6, 7, 5)).reshape(
        batch, depth, 2, n_groups_s5, n_state, 2)
    return y_prompt, y_sample, new_state_rglru, new_state_s5
```

```python
import functools
import math

import jax
import jax.numpy as jnp
from jax import lax
from jax.experimental import pallas as pl
from jax.experimental.pallas import tpu as pltpu

F32 = jnp.float32
MXU_DTYPE = jnp.bfloat16

SUBLANES = 8
LANES = 128
VMEM_LIMIT_BYTES = 56 * 1024 * 1024

GRID_W = 64
RG_C = 8.0
S5_GROUP = 16
TOP_K = 8
N_ROUTE_GROUPS = 8
TOPK_GROUPS = 4
ROUTE_SCALE = 2.5
EPS = 1e-6

CHUNK_TILES = 64
CONV_HALO_TILES = 16


def _params(n_grid_dims):
    return pltpu.CompilerParams(
        dimension_semantics=("arbitrary",) * n_grid_dims,
        vmem_limit_bytes=VMEM_LIMIT_BYTES,
    )


def _mm(a, b):
    return jnp.dot(a.astype(MXU_DTYPE), b.astype(MXU_DTYPE), preferred_element_type=F32)


def _sigmoid(x):
    return 0.5 * jnp.tanh(0.5 * x) + 0.5


def _silu(x):
    return x * _sigmoid(x)


def _tile_bcast_mul_add(y, scale, shift):
    rows, d = y.shape
    y3 = y.reshape(rows // SUBLANES, SUBLANES, d)
    return (y3 * scale[None] + shift[None]).reshape(rows, d)


def _gated_residual(x, gate, y):
    rows, d = y.shape
    return (y.reshape(rows // SUBLANES, SUBLANES, d) * gate[None]).reshape(rows, d) + x


def _norm_mod(x, g, scale, shift):
    ms = jnp.mean(x * x, axis=-1, keepdims=True)
    y = x * lax.rsqrt(ms + EPS) * g
    return _tile_bcast_mul_add(y, 1.0 + scale, shift)


def _ada_kernel(c_ref, w_ref, b_ref, o_ref):
    o_ref[0] = _mm(_silu(c_ref[...]), w_ref[0]) + b_ref[0]


def _ada(cond, ada_w, ada_b):
    depth, d, n = ada_w.shape
    tn = math.gcd(n, 1024)
    rows = cond.shape[0]
    return pl.pallas_call(
        _ada_kernel,
        grid=(depth, n // tn),
        in_specs=[
            pl.BlockSpec((rows, d), lambda l, j: (0, 0)),
            pl.BlockSpec((1, d, tn), lambda l, j: (l, 0, j)),
            pl.BlockSpec((1, 1, tn), lambda l, j: (l, 0, j)),
        ],
        out_specs=pl.BlockSpec((1, rows, tn), lambda l, j: (l, 0, j)),
        out_shape=jax.ShapeDtypeStruct((depth, rows, n), F32),
        compiler_params=_params(2),
        name="ada",
    )(cond, ada_w, ada_b.reshape(depth, 1, n))


def _win_kernel(x_ref, g_ref, sc_ref, sh_ref, wa_ref, wb_ref, bg_ref,
                urg_ref, z_ref, us5_ref, gt_ref, hn_ref, *, n_rg, n_cv, n_s5):
    j = pl.program_id(1)

    @pl.when(j == 0)
    def _():
        hn_ref[...] = _norm_mod(x_ref[...], g_ref[...], sc_ref[0], sh_ref[0]).astype(hn_ref.dtype)

    hn = hn_ref[...]
    a = jnp.dot(hn, wa_ref[...], preferred_element_type=F32)

    @pl.when(j < n_rg)
    def _():
        urg_ref[...] = a

    @pl.when((j >= n_rg) & (j < n_rg + n_cv))
    def _():
        b = jnp.dot(hn, wb_ref[...], preferred_element_type=F32)
        z_ref[...] = a * _sigmoid(b)

    @pl.when((j >= n_rg + n_cv) & (j < n_rg + n_cv + n_s5))
    def _():
        us5_ref[...] = a

    @pl.when(j >= n_rg + n_cv + n_s5)
    def _():
        gt_ref[...] = _sigmoid(a + bg_ref[...]).astype(gt_ref.dtype)


def _win(x, g, scale, shift, w_in, b_gate, *, rg, cv, s5, n_sample_rows, tm, tn):
    t, d = x.shape
    n_gate = b_gate.shape[-1]
    n_rg, n_cv, n_s5, n_g = rg // tn, cv // tn, s5 // tn, n_gate // tn
    ns = n_sample_rows // tm
    unit = lambda i, j: ((i >= ns).astype(jnp.int32), 0, 0)
    clip = lambda v, n: jnp.clip(v, 0, n - 1)
    kern = functools.partial(_win_kernel, n_rg=n_rg, n_cv=n_cv, n_s5=n_s5)
    return pl.pallas_call(
        kern,
        grid=(t // tm, n_rg + n_cv + n_s5 + n_g),
        in_specs=[
            pl.BlockSpec((tm, d), lambda i, j: (i, 0)),
            pl.BlockSpec((1, d), lambda i, j: (0, 0)),
            pl.BlockSpec((1, SUBLANES, d), unit),
            pl.BlockSpec((1, SUBLANES, d), unit),
            pl.BlockSpec((d, tn), lambda i, j: (0, jnp.where(j < n_rg + n_cv, j, j + n_cv))),
            pl.BlockSpec((d, tn), lambda i, j: (0, n_rg + n_cv + clip(j - n_rg, n_cv))),
            pl.BlockSpec((1, tn), lambda i, j: (0, clip(j - (n_rg + n_cv + n_s5), n_g))),
        ],
        out_specs=[
            pl.BlockSpec((tm, tn), lambda i, j: (i, clip(j, n_rg))),
            pl.BlockSpec((tm, tn), lambda i, j: (i, clip(j - n_rg, n_cv))),
            pl.BlockSpec((tm, tn), lambda i, j: (i, clip(j - n_rg - n_cv, n_s5))),
            pl.BlockSpec((tm, tn), lambda i, j: (i, clip(j - n_rg - n_cv - n_s5, n_g))),
        ],
        out_shape=[
            jax.ShapeDtypeStruct((t, rg), F32),
            jax.ShapeDtypeStruct((t, cv), F32),
            jax.ShapeDtypeStruct((t, s5), F32),
            jax.ShapeDtypeStruct((t, n_gate), MXU_DTYPE),
        ],
        scratch_shapes=[pltpu.VMEM((tm, d), MXU_DTYPE)],
        compiler_params=_params(2),
        name="win",
    )(x, g, scale, shift, w_in, w_in, b_gate)


def _rg_kernel(u_ref, cw_ref, cb_ref, wa_ref, wx_ref, ba_ref, bx_ref, lam_ref, h0_ref,
               y_ref, fin_ref, ext_ref, a_ref, b_ref, *, units, ch, kconv):
    rows = ch * SUBLANES
    pad_lo = (kconv // 2) * SUBLANES
    pad_hi = (kconv - 1 - kconv // 2) * SUBLANES
    t_rows = u_ref.shape[0]
    for d in (0, 1):
        neg_lam = -lam_ref[d:d + 1, :]
        softplus = jnp.maximum(neg_lam, 0.0) + jnp.log1p(jnp.exp(-jnp.abs(neg_lam)))
        coef = -RG_C * softplus
        wa = wa_ref[d, 0]
        wx = wx_ref[d, 0]
        ba = ba_ref[d:d + 1, :]
        bx = bx_ref[d:d + 1, :]
        for ui, (t0, nt, _, has_h0) in enumerate(units):
            nch = nt // ch

            def chunk_body(ci, h, d=d, t0=t0, nch=nch, coef=coef, wa=wa, wx=wx, ba=ba, bx=bx):
                c = ci if d == 0 else nch - 1 - ci
                r0 = pl.multiple_of((t0 + c * ch) * SUBLANES, SUBLANES)
                lo_start = pl.multiple_of(jnp.maximum(r0 - pad_lo, 0), SUBLANES)
                hi_start = pl.multiple_of(jnp.minimum(r0 + rows, t_rows - pad_hi), SUBLANES)
                ext_ref[0:pad_lo, :] = jnp.where(c > 0, u_ref[pl.ds(lo_start, pad_lo), :], 0.0)
                ext_ref[pad_lo:pad_lo + rows, :] = u_ref[pl.ds(r0, rows), :]
                ext_ref[pad_lo + rows:pad_lo + rows + pad_hi, :] = jnp.where(
                    c < nch - 1, u_ref[pl.ds(hi_start, pad_hi), :], 0.0)
                xc = cb_ref[...] + cw_ref[0:1, :] * ext_ref[0:rows, :]
                for k in range(1, kconv):
                    xc = xc + cw_ref[k:k + 1, :] * ext_ref[k * SUBLANES:k * SUBLANES + rows, :]
                r = _sigmoid(_mm(xc, wa) + ba)
                i = _sigmoid(_mm(xc, wx) + bx)
                a = jnp.exp(coef * r)
                a_ref[...] = a
                b_ref[...] = jnp.sqrt(1.0 - a * a) * i * xc

                def step(s, h):
                    tt = s if d == 0 else ch - 1 - s
                    o = pl.multiple_of(tt * SUBLANES, SUBLANES)
                    h = a_ref[pl.ds(o, SUBLANES), :] * h + b_ref[pl.ds(o, SUBLANES), :]
                    dst = pl.ds(pl.multiple_of(r0 + o, SUBLANES), SUBLANES)
                    if d == 0:
                        y_ref[dst, :] = h
                    else:
                        y_ref[dst, :] = y_ref[dst, :] + h
                    return h

                return lax.fori_loop(0, ch, step, h, unroll=8)

            h_init = h0_ref[d] if has_h0 else jnp.zeros((SUBLANES, LANES), F32)
            fin_ref[ui, d] = lax.fori_loop(0, nch, chunk_body, h_init)


def _rglru(u, conv_w, conv_b, wa, wx, ba, bx, lam, h0, *, units):
    t, rg = u.shape
    heads = rg // LANES
    kconv = conv_w.shape[0]
    n_units = len(units)
    rows = CHUNK_TILES * SUBLANES
    kern = functools.partial(_rg_kernel, units=units, ch=CHUNK_TILES, kconv=kconv)
    col = lambda h: (0, h)
    return pl.pallas_call(
        kern,
        grid=(heads,),
        in_specs=[
            pl.BlockSpec((t, LANES), col),
            pl.BlockSpec((kconv, LANES), col),
            pl.BlockSpec((1, LANES), col),
            pl.BlockSpec((2, 1, LANES, LANES), lambda h: (0, h, 0, 0)),
            pl.BlockSpec((2, 1, LANES, LANES), lambda h: (0, h, 0, 0)),
            pl.BlockSpec((2, LANES), col),
            pl.BlockSpec((2, LANES), col),
            pl.BlockSpec((2, LANES), col),
            pl.BlockSpec((2, SUBLANES, LANES), lambda h: (0, 0, h)),
        ],
        out_specs=[
            pl.BlockSpec((t, LANES), col),
            pl.BlockSpec((n_units, 2, SUBLANES, LANES), lambda h: (0, 0, 0, h)),
        ],
        out_shape=[
            jax.ShapeDtypeStruct((t, rg), F32),
            jax.ShapeDtypeStruct((n_units, 2, SUBLANES, rg), F32),
        ],
        scratch_shapes=[
            pltpu.VMEM((rows + (kconv - 1) * SUBLANES, LANES), F32),
            pltpu.VMEM((rows, LANES), F32),
            pltpu.VMEM((rows, LANES), F32),
        ],
        compiler_params=_params(1),
        name="rglru",
    )(u, conv_w, conv_b, wa, wx, ba, bx, lam, h0)


def _cv_kernel(zp_ref, zc_ref, zn_ref, w_ref, db_ref, lg_ref, lb_ref, o_ref, ext_ref, acc_ref,
               *, n_sample_chunks, ctx_unit_chunks, kc):
    i = pl.program_id(0)
    rows, c = zc_ref.shape
    hr = CONV_HALO_TILES * SUBLANES
    group = SUBLANES
    is_ctx = i >= n_sample_chunks
    cpos = (i - n_sample_chunks) % ctx_unit_chunks
    lo_ok = is_ctx & (cpos > 0)
    hi_ok = is_ctx & (cpos < ctx_unit_chunks - 1)
    ext_ref[0:hr, :] = jnp.where(lo_ok, zp_ref[rows - hr:rows, :], 0.0)
    ext_ref[hr:hr + rows, :] = zc_ref[...]
    ext_ref[hr + rows:hr + rows + hr, :] = jnp.where(hi_ok, zn_ref[0:hr, :], 0.0)
    first_tap_tile = CONV_HALO_TILES - kc // 2

    def lane_body(lb, carry):
        lanes = pl.ds(pl.multiple_of(lb * LANES, LANES), LANES)
        taps = [w_ref[k, :, lanes] for k in range(kc)]

        def grp_body(g, carry):
            accs = [None] * group
            for e in range(group + kc - 1):
                src = pl.multiple_of((g * group + first_tap_tile + e) * SUBLANES, SUBLANES)
                tile = ext_ref[pl.ds(src, SUBLANES), lanes]
                for t in range(group):
                    k = e - t
                    if 0 <= k < kc:
                        term = taps[k] * tile
                        accs[t] = term if accs[t] is None else accs[t] + term
            for t in range(group):
                dst = pl.multiple_of((g * group + t) * SUBLANES, SUBLANES)
                acc_ref[pl.ds(dst, SUBLANES), lanes] = accs[t]
            return carry

        return lax.fori_loop(0, rows // (group * SUBLANES), grp_body, carry)

    lax.fori_loop(0, c // LANES, lane_body, 0)
    z = acc_ref[...] + db_ref[...]
    mu = jnp.mean(z, axis=-1, keepdims=True)
    zc = z - mu
    var = jnp.mean(zc * zc, axis=-1, keepdims=True)
    y = zc * lax.rsqrt(var + EPS) * lg_ref[...] + lb_ref[...]
    o_ref[...] = _silu(y).astype(o_ref.dtype)


def _conv_branch(z, dw_tiles, db, ln_g, ln_b, *, n_sample_chunks, ctx_unit_chunks):
    t, c = z.shape
    kc = dw_tiles.shape[0]
    rows = CHUNK_TILES * SUBLANES
    n = t // rows
    kern = functools.partial(_cv_kernel, n_sample_chunks=n_sample_chunks,
                             ctx_unit_chunks=ctx_unit_chunks, kc=kc)
    vec = pl.BlockSpec((1, c), lambda i: (0, 0))
    return pl.pallas_call(
        kern,
        grid=(n,),
        in_specs=[
            pl.BlockSpec((rows, c), lambda i: (jnp.maximum(i - 1, 0), 0)),
            pl.BlockSpec((rows, c), lambda i: (i, 0)),
            pl.BlockSpec((rows, c), lambda i: (jnp.minimum(i + 1, n - 1), 0)),
            pl.BlockSpec((kc, SUBLANES, c), lambda i: (0, 0, 0)),
            vec, vec, vec,
        ],
        out_specs=pl.BlockSpec((rows, c), lambda i: (i, 0)),
        out_shape=jax.ShapeDtypeStruct((t, c), MXU_DTYPE),
        scratch_shapes=[
            pltpu.VMEM((rows + 2 * CONV_HALO_TILES * SUBLANES, c), F32),
            pltpu.VMEM((rows, c), F32),
        ],
        compiler_params=_params(1),
        name="conv",
    )(z, z, z, dw_tiles, db, ln_g, ln_b)


def _s5_kernel(u_ref, dsk_ref, are_ref, aim_ref, bm_ref, cm_ref, h0_ref,
               y_ref, fin_ref, lhs_ref, hs_ref, yc_ref, *, units, tc, grid_rows):
    ns = are_ref.shape[-1]

    def tile_of(unit, c, s):
        t0, _, is_grid, _ = unit
        if is_grid:
            return t0 + (s % grid_rows) * GRID_W + c * (tc // grid_rows) + s // grid_rows
        return t0 + c * tc + s

    for d in (0, 1):
        a_re = jnp.broadcast_to(are_ref[d, 0], (SUBLANES, ns))
        a_im = jnp.broadcast_to(aim_ref[d, 0], (SUBLANES, ns))
        for ui, unit in enumerate(units):
            nch = unit[1] // tc

            def chunk_body(ci, h, d=d, unit=unit, nch=nch, a_re=a_re, a_im=a_im):
                c = ci if d == 0 else nch - 1 - ci
                for s in range(tc):
                    r = pl.multiple_of(tile_of(unit, c, s) * SUBLANES, SUBLANES)
                    lhs_ref[s * SUBLANES:(s + 1) * SUBLANES, :] = u_ref[pl.ds(r, SUBLANES), :]
                hs_ref[...] = _mm(lhs_ref[...], bm_ref[d, 0])

                def step(si, h):
                    s = si if d == 0 else tc - 1 - si
                    o = pl.ds(pl.multiple_of(s * SUBLANES, SUBLANES), SUBLANES)
                    h_re, h_im = h
                    n_re = a_re * h_re - a_im * h_im + hs_ref[o, 0:ns]
                    n_im = a_re * h_im + a_im * h_re + hs_ref[o, ns:2 * ns]
                    hs_ref[o, 0:ns] = n_re
                    hs_ref[o, ns:2 * ns] = n_im
                    return n_re, n_im

                h = lax.fori_loop(0, tc, step, h, unroll=8)
                yc_ref[...] = _mm(hs_ref[...], cm_ref[d, 0])
                for s in range(tc):
                    r = pl.multiple_of(tile_of(unit, c, s) * SUBLANES, SUBLANES)
                    sl = slice(s * SUBLANES, (s + 1) * SUBLANES)
                    if d == 0:
                        y_ref[pl.ds(r, SUBLANES), :] = dsk_ref[...] * lhs_ref[sl, :] + yc_ref[sl, :]
                    else:
                        y_ref[pl.ds(r, SUBLANES), :] = y_ref[pl.ds(r, SUBLANES), :] + yc_ref[sl, :]
                return h

            if unit[3]:
                h_init = (h0_ref[d, 0, :, 0:ns], h0_ref[d, 0, :, ns:2 * ns])
            else:
                h_init = (jnp.zeros((SUBLANES, ns), F32), jnp.zeros((SUBLANES, ns), F32))
            f_re, f_im = lax.fori_loop(0, nch, chunk_body, h_init)
            fin_ref[ui, d, 0, :, 0:ns] = f_re
            fin_ref[ui, d, 0, :, ns:2 * ns] = f_im


def _s5(u, dskip, a_re, a_im, bmat, cmat, h0, *, units, tc, grid_rows):
    t, width = u.shape
    nb = width // LANES
    ns = a_re.shape[-1]
    n_units = len(units)
    kern = functools.partial(_s5_kernel, units=units, tc=tc, grid_rows=grid_rows)
    blk4 = lambda *shape: pl.BlockSpec((2, 1) + shape, lambda j: (0, j, 0, 0))
    return pl.pallas_call(
        kern,
        grid=(nb,),
        in_specs=[
            pl.BlockSpec((t, LANES), lambda j: (0, j)),
            pl.BlockSpec((1, LANES), lambda j: (0, j)),
            blk4(1, ns), blk4(1, ns),
            blk4(LANES, 2 * ns), blk4(2 * ns, LANES),
            blk4(SUBLANES, 2 * ns),
        ],
        out_specs=[
            pl.BlockSpec((t, LANES), lambda j: (0, j)),
            pl.BlockSpec((n_units, 2, 1, SUBLANES, 2 * ns), lambda j: (0, 0, j, 0, 0)),
        ],
        out_shape=[
            jax.ShapeDtypeStruct((t, width), F32),
            jax.ShapeDtypeStruct((n_units, 2, nb, SUBLANES, 2 * ns), F32),
        ],
        scratch_shapes=[
            pltpu.VMEM((tc * SUBLANES, LANES), F32),
            pltpu.VMEM((tc * SUBLANES, 2 * ns), F32),
            pltpu.VMEM((tc * SUBLANES, LANES), F32),
        ],
        compiler_params=_params(1),
        name="s5",
    )(u, dskip, a_re, a_im, bmat, cmat, h0)


def _merge_kernel(yrg_ref, zc_ref, ys5_ref, g0_ref, g1_ref, g2_ref, wrg_ref, wcv_ref, wsv_ref,
                  wsg_ref, wo_ref, x_ref, gate_ref, o_ref, acc_ref, rgb_ref, s5b_ref):
    j = pl.program_id(1)

    @pl.when(j == 0)
    def _():
        acc_ref[...] = jnp.zeros_like(acc_ref)
        rgb_ref[...] = yrg_ref[...].astype(rgb_ref.dtype)
        s5b_ref[...] = ys5_ref[...].astype(s5b_ref.dtype)

    dot = lambda a, b: jnp.dot(a, b, preferred_element_type=F32)
    br_rg = dot(rgb_ref[...], wrg_ref[...])
    br_cv = dot(zc_ref[...], wcv_ref[...])
    s5b = s5b_ref[...]
    br_s5 = dot(s5b, wsv_ref[...]) * _sigmoid(dot(s5b, wsg_ref[...]))
    merged = (g0_ref[...].astype(F32) * br_rg + g1_ref[...].astype(F32) * br_cv
              + g2_ref[...].astype(F32) * br_s5)
    acc_ref[...] += dot(merged.astype(MXU_DTYPE), wo_ref[...])

    @pl.when(j == pl.num_programs(1) - 1)
    def _():
        o_ref[...] = _gated_residual(x_ref[...], gate_ref[0], acc_ref[...])


def _merge(y_rg, zc, y_s5, gates, rg_out, cv_out, s5_glu, w_out, x, gate, *, n_sample_rows, tm, tk):
    t, d = x.shape
    rg, cv, s5 = y_rg.shape[1], zc.shape[1], y_s5.shape[1]
    nk = d // tk
    ns = n_sample_rows // tm
    row = lambda w: pl.BlockSpec((tm, w), lambda i, j: (i, 0))
    gspec = lambda k: pl.BlockSpec((tm, tk), lambda i, j: (i, k * nk + j))
    wcol = lambda rows, off: pl.BlockSpec((rows, tk), lambda i, j: (0, off + j))
    return pl.pallas_call(
        _merge_kernel,
        grid=(t // tm, nk),
        in_specs=[
            row(rg), row(cv), row(s5),
            gspec(0), gspec(1), gspec(2),
            wcol(rg, 0), wcol(cv, 0), wcol(s5, 0), wcol(s5, nk),
            pl.BlockSpec((tk, d), lambda i, j: (j, 0)),
            row(d),
            pl.BlockSpec((1, SUBLANES, d), lambda i, j: ((i >= ns).astype(jnp.int32), 0, 0)),
        ],
        out_specs=row(d),
        out_shape=jax.ShapeDtypeStruct((t, d), F32),
        scratch_shapes=[
            pltpu.VMEM((tm, d), F32),
            pltpu.VMEM((tm, rg), MXU_DTYPE),
            pltpu.VMEM((tm, s5), MXU_DTYPE),
        ],
        compiler_params=_params(2),
        name="merge",
    )(y_rg, zc, y_s5, gates, gates, gates, rg_out, cv_out, s5_glu, s5_glu, w_out, x, gate)


def _pack_bf16_pairs(v):
    half = v.shape[1] // 2
    bits = lambda a: lax.bitcast_convert_type(a.astype(jnp.bfloat16).astype(F32), jnp.uint32)
    return (bits(v[:, :half]) >> 16) | (bits(v[:, half:]) & jnp.uint32(0xFFFF0000))


def _unpack_bf16_pairs(p):
    lo = lax.bitcast_convert_type(p << 16, F32)
    hi = lax.bitcast_convert_type(p & jnp.uint32(0xFFFF0000), F32)
    return lo, hi


def _router_kernel(x_ref, g_ref, sc_ref, sh_ref, rw_ref, rb_ref,
                   hnp_ref, eidx_ref, rank_ref, wtok_ref, cnt_ref, run_ref):
    @pl.when(pl.program_id(0) == 0)
    def _():
        run_ref[...] = jnp.zeros_like(run_ref)

    hn = _norm_mod(x_ref[...], g_ref[...], sc_ref[0], sh_ref[0])
    hnp_ref[...] = _pack_bf16_pairs(hn)
    logits = lax.dot_general(rw_ref[...], hn, (((1,), (1,)), ((), ())),
                             precision=lax.Precision.HIGHEST, preferred_element_type=F32)
    s = _sigmoid(logits)
    choice = s + rb_ref[...]
    n_exp, tm = choice.shape
    gsize = n_exp // N_ROUTE_GROUPS
    neg_inf = jnp.float32(-jnp.inf)
    c3 = choice.reshape(N_ROUTE_GROUPS, gsize, tm)
    sub = lax.broadcasted_iota(jnp.int32, c3.shape, 1)
    m1 = jnp.max(c3, axis=1, keepdims=True)
    i1 = jnp.min(jnp.where(c3 == m1, sub, gsize), axis=1, keepdims=True)
    m2 = jnp.max(jnp.where(sub == i1, neg_inf, c3), axis=1, keepdims=True)
    gscore = jnp.broadcast_to(m1 + m2, c3.shape)
    gidx = lax.broadcasted_iota(jnp.int32, c3.shape, 0)
    beaten = jnp.zeros(c3.shape, jnp.int32)
    for gp in range(N_ROUTE_GROUPS):
        other = gscore[gp:gp + 1]
        wins = (other > gscore) | ((other == gscore) & (gidx > gp))
        beaten = beaten + wins.astype(jnp.int32)
    masked = jnp.where(beaten < TOPK_GROUPS, c3, neg_inf).reshape(n_exp, tm)
    eidx = lax.broadcasted_iota(jnp.int32, masked.shape, 0)
    beaten = jnp.zeros(masked.shape, jnp.int32)
    for ep in range(n_exp):
        other = masked[ep:ep + 1, :]
        wins = (other > masked) | ((other == masked) & (eidx > ep))
        beaten = beaten + wins.astype(jnp.int32)
    sel = beaten < TOP_K
    w = jnp.where(sel, s, 0.0)
    comb = ROUTE_SCALE * w / jnp.sum(w, axis=0, keepdims=True)
    self32 = sel.astype(F32)
    before = (lax.broadcasted_iota(jnp.int32, (tm, tm), 0) < lax.broadcasted_iota(jnp.int32, (tm, tm), 1))
    rank = run_ref[...] + _mm(self32, before.astype(F32))
    run_ref[...] = run_ref[...] + jnp.sum(self32, axis=1, keepdims=True)
    cnt_ref[...] = jnp.broadcast_to(run_ref[...], cnt_ref.shape)
    eidx_f = eidx.astype(F32)
    w_rows = []
    for j in range(TOP_K):
        m = beaten == j
        pick = lambda v: jnp.sum(jnp.where(m, v, 0.0), axis=0, keepdims=True)
        eidx_ref[j:j + 1, :] = pick(eidx_f).astype(jnp.int32)
        rank_ref[j:j + 1, :] = pick(rank).astype(jnp.int32)
        w_rows.append(pick(comb))
    w_rows.append(jnp.zeros((wtok_ref.shape[1] - TOP_K, tm), F32))
    wtok_ref[...] = jnp.concatenate(w_rows, axis=0).T


def _router(x, g, scale, shift, router_wt, router_b, *, n_sample_rows, tm):
    t, d = x.shape
    n_exp = router_wt.shape[0]
    ns = n_sample_rows // tm
    unit = lambda i: ((i >= ns).astype(jnp.int32), 0, 0)
    return pl.pallas_call(
        _router_kernel,
        grid=(t // tm,),
        in_specs=[
            pl.BlockSpec((tm, d), lambda i: (i, 0)),
            pl.BlockSpec((1, d), lambda i: (0, 0)),
            pl.BlockSpec((1, SUBLANES, d), unit),
            pl.BlockSpec((1, SUBLANES, d), unit),
            pl.BlockSpec((n_exp, d), lambda i: (0, 0)),
            pl.BlockSpec((n_exp, 1), lambda i: (0, 0)),
        ],
        out_specs=[
            pl.BlockSpec((tm, d // 2), lambda i: (i, 0)),
            pl.BlockSpec((TOP_K, tm), lambda i: (0, i)),
            pl.BlockSpec((TOP_K, tm), lambda i: (0, i)),
            pl.BlockSpec((tm, LANES), lambda i: (i, 0)),
            pl.BlockSpec((n_exp, LANES), lambda i: (0, 0)),
        ],
        out_shape=[
            jax.ShapeDtypeStruct((t, d // 2), jnp.uint32),
            jax.ShapeDtypeStruct((TOP_K, t), jnp.int32),
            jax.ShapeDtypeStruct((TOP_K, t), jnp.int32),
            jax.ShapeDtypeStruct((t, LANES), F32),
            jax.ShapeDtypeStruct((n_exp, LANES), F32),
        ],
        scratch_shapes=[pltpu.VMEM((n_exp, 1), F32)],
        compiler_params=_params(1),
        name="router",
    )(x, g, scale, shift, router_wt, router_b)


def _swiglu(lo, hi, w1, w3, w2):
    half = lo.shape[1]
    lo = lo.astype(MXU_DTYPE)
    hi = hi.astype(MXU_DTYPE)
    dot = lambda a, b: jnp.dot(a, b, preferred_element_type=F32)
    h1 = dot(lo, w1[0:half, :]) + dot(hi, w1[half:, :])
    h3 = dot(lo, w3[0:half, :]) + dot(hi, w3[half:, :])
    return dot((_silu(h1) * h3).astype(MXU_DTYPE), w2[...])


def _dispatch_kernel(pos_ref, hn_ref, xs_in_hbm, xs_hbm, sem):
    del xs_in_hbm
    n_slots, tm = pos_ref.shape

    def issue(t, carry):
        src = hn_ref.at[pl.ds(t, 1)]
        for j in range(n_slots):
            pltpu.make_async_copy(src, xs_hbm.at[pl.ds(pos_ref[j, t], 1)], sem).start(priority=j % 2)
        return carry

    lax.fori_loop(0, tm, issue, 0)
    for j in range(n_slots):
        pltpu.make_async_copy(hn_ref, xs_hbm.at[pl.ds(0, tm)], sem).wait()


def _dispatch(pos, hnp, xs_init, *, tm):
    n_slots, t = pos.shape
    return pl.pallas_call(
        _dispatch_kernel,
        grid=(t // tm,),
        in_specs=[
            pl.BlockSpec((n_slots, tm), lambda i: (0, i), memory_space=pltpu.SMEM),
            pl.BlockSpec((tm, hnp.shape[1]), lambda i: (i, 0)),
            pl.BlockSpec(memory_space=pl.ANY),
        ],
        out_specs=pl.BlockSpec(memory_space=pl.ANY),
        out_shape=jax.ShapeDtypeStruct(xs_init.shape, xs_init.dtype),
        scratch_shapes=[pltpu.SemaphoreType.DMA],
        input_output_aliases={2: 0},
        compiler_params=_params(1),
        name="dispatch",
    )(pos, hnp, xs_init)


def _ffn_kernel(te_ref, nu_ref, xs_ref, w1_ref, w3_ref, w2_ref, ys_ref, w1b_ref, w3b_ref, w2b_ref):
    i = pl.program_id(0)

    @pl.when((i == 0) | (te_ref[i] != te_ref[jnp.maximum(i - 1, 0)]))
    def _():
        w1b_ref[...] = w1_ref[0, 0].astype(w1b_ref.dtype)
        w3b_ref[...] = w3_ref[0, 0].astype(w3b_ref.dtype)
        w2b_ref[...] = w2_ref[0, 0].astype(w2b_ref.dtype)

    @pl.when(i < nu_ref[0])
    def _():
        lo, hi = _unpack_bf16_pairs(xs_ref[...])
        ys_ref[...] = _pack_bf16_pairs(_swiglu(lo, hi, w1b_ref, w3b_ref, w2b_ref))

    @pl.when(i >= nu_ref[0])
    def _():
        ys_ref[...] = jnp.zeros_like(ys_ref)


def _ffn(tile_expert, n_used, xs, w1, w3, w2, *, layer, tg):
    s_pad, half = xs.shape
    _, _, d, f = w1.shape
    row = lambda i, te, nu: (jnp.maximum(jnp.minimum(i, nu[0] - 1), 0), 0)
    wblk = lambda i, te, nu: (layer, te[i], 0, 0)
    grid_spec = pltpu.PrefetchScalarGridSpec(
        num_scalar_prefetch=2,
        grid=(s_pad // tg,),
        in_specs=[
            pl.BlockSpec((tg, half), row),
            pl.BlockSpec((1, 1, d, f), wblk),
            pl.BlockSpec((1, 1, d, f), wblk),
            pl.BlockSpec((1, 1, f, d), wblk),
        ],
        out_specs=pl.BlockSpec((tg, half), lambda i, te, nu: (i, 0)),
        scratch_shapes=[
            pltpu.VMEM((d, f), MXU_DTYPE),
            pltpu.VMEM((d, f), MXU_DTYPE),
            pltpu.VMEM((f, d), MXU_DTYPE),
        ],
    )
    return pl.pallas_call(
        _ffn_kernel,
        grid_spec=grid_spec,
        out_shape=jax.ShapeDtypeStruct((s_pad, half), jnp.uint32),
        compiler_params=_params(1),
        name="ffn",
    )(tile_expert, n_used, xs, w1, w3, w2)


def _combine_kernel(pos_ref, hnp_ref, wtok_ref, s1_ref, s3_ref, s2_ref, x_ref, gate_ref, ys_hbm,
                    o_ref, buf_ref, sem):
    n_slots, tm = pos_ref.shape

    def issue(t, carry):
        for j in range(n_slots):
            pltpu.make_async_copy(ys_hbm.at[pl.ds(pos_ref[j, t], 1)],
                                  buf_ref.at[j, pl.ds(t, 1)], sem).start(priority=j % 2)
        return carry

    lax.fori_loop(0, tm, issue, 0)
    lo, hi = _unpack_bf16_pairs(hnp_ref[...])
    moe = _swiglu(lo, hi, s1_ref, s3_ref, s2_ref)
    half = lo.shape[1]
    acc_lo = moe[:, :half]
    acc_hi = moe[:, half:]
    wtok = wtok_ref[...]
    for j in range(n_slots):
        pltpu.make_async_copy(ys_hbm.at[pl.ds(0, tm)], buf_ref.at[j], sem).wait()
    for j in range(n_slots):
        y_lo, y_hi = _unpack_bf16_pairs(buf_ref[j])
        wj = wtok[:, j:j + 1]
        acc_lo = acc_lo + wj * y_lo
        acc_hi = acc_hi + wj * y_hi
    x = x_ref[...]
    gate = gate_ref[0]
    o_ref[:, :half] = _gated_residual(x[:, :half], gate[:, :half], acc_lo)
    o_ref[:, half:] = _gated_residual(x[:, half:], gate[:, half:], acc_hi)


def _combine(pos, hnp, wtok, s1, s3, s2, x, gate, ys, *, n_sample_rows, tm):
    t, d = x.shape
    n_slots = pos.shape[0]
    f = s1.shape[1]
    ns = n_sample_rows // tm
    full = lambda shape: pl.BlockSpec(shape, lambda i: (0,) * len(shape))
    return pl.pallas_call(
        _combine_kernel,
        grid=(t // tm,),
        in_specs=[
            pl.BlockSpec((n_slots, tm), lambda i: (0, i), memory_space=pltpu.SMEM),
            pl.BlockSpec((tm, d // 2), lambda i: (i, 0)),
            pl.BlockSpec((tm, LANES), lambda i: (i, 0)),
            full((d, f)), full((d, f)), full((f, d)),
            pl.BlockSpec((tm, d), lambda i: (i, 0)),
            pl.BlockSpec((1, SUBLANES, d), lambda i: ((i >= ns).astype(jnp.int32), 0, 0)),
            pl.BlockSpec(memory_space=pl.ANY),
        ],
        out_specs=pl.BlockSpec((tm, d), lambda i: (i, 0)),
        out_shape=jax.ShapeDtypeStruct((t, d), F32),
        scratch_shapes=[
            pltpu.VMEM((n_slots, tm, d // 2), jnp.uint32),
            pltpu.SemaphoreType.DMA,
        ],
        compiler_params=_params(1),
        name="combine",
    )(pos, hnp, wtok, s1, s3, s2, x, gate, ys)


def _slots_kernel(cnt_ref, eidx_ref, rank_ref, pos_ref, te_ref, nu_ref, *, tg):
    counts = cnt_ref[...]
    n_exp = counts.shape[0]
    padded = jnp.floor((counts + (tg - 1)) * (1.0 / tg)) * tg
    incl = (lax.broadcasted_iota(jnp.int32, (n_exp, n_exp), 0)
            >= lax.broadcasted_iota(jnp.int32, (n_exp, n_exp), 1)).astype(F32)
    ends = jnp.dot(incl, padded, precision=lax.Precision.HIGHEST, preferred_element_type=F32)
    ends_col = ends[:, 0:1]
    offs_col = ends_col - padded[:, 0:1]
    eidx = eidx_ref[...]
    rank = rank_ref[...]
    expert = lax.broadcasted_iota(jnp.int32, (n_exp, eidx.shape[1]), 0)
    for j in range(eidx.shape[0]):
        off = jnp.sum(jnp.where(expert == eidx[j:j + 1, :], offs_col, 0.0), axis=0, keepdims=True)
        pos_ref[j:j + 1, :] = off.astype(jnp.int32) + rank[j:j + 1, :]
    n_used = ends[n_exp - 1:n_exp, 0:1] * (1.0 / tg)
    tile = lax.broadcasted_iota(jnp.int32, te_ref.shape, 1).astype(F32)
    tile_row = jnp.minimum(tile, n_used - 1.0) * tg
    te_ref[...] = jnp.sum((ends_col <= tile_row).astype(F32), axis=0, keepdims=True).astype(jnp.int32)
    nu_ref[...] = jnp.broadcast_to(n_used, nu_ref.shape).astype(jnp.int32)


def _slots(counts, eidx, rank, *, tg, n_tiles, tb):
    assert tg & (tg - 1) == 0
    n_slots, t = eidx.shape
    n_tiles_pad = pl.cdiv(n_tiles, LANES) * LANES
    pos, te, nu = pl.pallas_call(
        functools.partial(_slots_kernel, tg=tg),
        grid=(t // tb,),
        in_specs=[
            pl.BlockSpec(counts.shape, lambda i: (0, 0)),
            pl.BlockSpec((n_slots, tb), lambda i: (0, i)),
            pl.BlockSpec((n_slots, tb), lambda i: (0, i)),
        ],
        out_specs=[
            pl.BlockSpec((n_slots, tb), lambda i: (0, i)),
            pl.BlockSpec((1, n_tiles_pad), lambda i: (0, 0)),
            pl.BlockSpec((1, LANES), lambda i: (0, 0)),
        ],
        out_shape=[
            jax.ShapeDtypeStruct((n_slots, t), jnp.int32),
            jax.ShapeDtypeStruct((1, n_tiles_pad), jnp.int32),
            jax.ShapeDtypeStruct((1, LANES), jnp.int32),
        ],
        compiler_params=_params(1),
        name="slots",
    )(counts, eidx, rank)
    return pos, te[0, :n_tiles], nu[0, :1]


def _final_norm_kernel(x_ref, g_ref, os_ref, oc_ref, y_ref, *, n_sample_tiles):
    x = x_ref[...]
    ms = jnp.mean(x * x, axis=-1, keepdims=True)
    y = x * lax.rsqrt(ms + EPS) * g_ref[...]
    steps = x.shape[0] // SUBLANES
    n_lane_blocks = x.shape[1] // LANES
    for k in range(n_lane_blocks):
        y_ref[k] = y[:, k * LANES:(k + 1) * LANES]

    def emit(o_ref):
        for b in range(SUBLANES):
            for k in range(n_lane_blocks):
                o_ref[b, :, k * LANES:(k + 1) * LANES] = y_ref[k, pl.ds(b, steps, stride=SUBLANES), :]

    @pl.when(pl.program_id(0) < n_sample_tiles)
    def _():
        emit(os_ref)

    @pl.when(pl.program_id(0) >= n_sample_tiles)
    def _():
        emit(oc_ref)


def _final_norm(x, g, *, ls, lc, n_ctx_units, tm):
    t, d = x.shape
    steps = tm // SUBLANES
    nst = ls // steps
    ctx_tiles = lc // steps
    n_ctx = n_ctx_units * ctx_tiles

    def ctx_block(i):
        k = jnp.clip(i - nst, 0, n_ctx - 1)
        return (k // ctx_tiles, k % ctx_tiles, 0)

    return pl.pallas_call(
        functools.partial(_final_norm_kernel, n_sample_tiles=nst),
        grid=(t // tm,),
        in_specs=[pl.BlockSpec((tm, d), lambda i: (i, 0)), pl.BlockSpec((1, d), lambda i: (0, 0))],
        out_specs=[
            pl.BlockSpec((SUBLANES, steps, d), lambda i: (0, jnp.minimum(i, nst - 1), 0)),
            pl.BlockSpec((SUBLANES, steps, d), ctx_block),
        ],
        out_shape=[
            jax.ShapeDtypeStruct((SUBLANES, ls, d), F32),
            jax.ShapeDtypeStruct((n_ctx_units * SUBLANES, lc, d), F32),
        ],
        scratch_shapes=[pltpu.VMEM((d // LANES, tm, LANES), F32)],
        compiler_params=_params(1),
        name="final_norm",
    )(x, g)


def _s5_discretise(lam_re, lam_im, log_dt, b_re, b_im, c_re, c_im):
    two, g, p = lam_re.shape
    s = b_re.shape[-1]
    gb = LANES // s
    nb = g // gb
    dt = jnp.exp(log_dt)[..., None]
    mag = jnp.exp(lam_re * dt)
    a_re = mag * jnp.cos(lam_im * dt)
    a_im = mag * jnp.sin(lam_im * dt)
    den = lam_re * lam_re + lam_im * lam_im
    q_re = ((a_re - 1.0) * lam_re + a_im * lam_im) / den
    q_im = (a_im * lam_re - (a_re - 1.0) * lam_im) / den
    bb_re = q_re[..., None] * b_re - q_im[..., None] * b_im
    bb_im = q_re[..., None] * b_im + q_im[..., None] * b_re
    eye = jnp.eye(gb, dtype=F32)

    def bdiag_b(m):
        m = m.reshape(two, nb, gb, p, s)
        return jnp.einsum("dbgps,gh->dbgshp", m, eye).reshape(two, nb, gb * s, gb * p)

    def bdiag_c(m):
        m = m.reshape(two, nb, gb, s, p)
        return jnp.einsum("dbgsp,gh->dbgphs", m, eye).reshape(two, nb, gb * p, gb * s)

    bmat = jnp.concatenate([bdiag_b(bb_re), bdiag_b(bb_im)], axis=-1)
    cmat = jnp.concatenate([bdiag_c(c_re), -bdiag_c(c_im)], axis=-2)
    blk = lambda a: a.reshape(two, nb, 1, gb * p)
    return blk(a_re), blk(a_im), bmat.astype(MXU_DTYPE), cmat.astype(MXU_DTYPE)


def kernel(x_prompt, x_sample, state_rglru, state_s5, c, c_ctx, norm_mix, norm_ffn, ada_w, ada_b, w_in, b_gate, rg_conv_w, rg_conv_b, rg_wa, rg_ba, rg_wx, rg_bx, rg_lambda, rg_out, cv_dw, cv_db, cv_ln_g, cv_ln_b, cv_out, s5_lambda_re, s5_lambda_im, s5_log_dt, s5_b_re, s5_b_im, s5_c_re, s5_c_im, s5_d, s5_glu, w_out, router_w, router_b, exp_w1, exp_w3, exp_w2, sh_w1, sh_w3, sh_w2, norm_final):
    batch, lc, d = x_prompt.shape
    dec_batch, ls, _ = x_sample.shape
    depth = w_in.shape[0]
    rg = rg_out.shape[1]
    cv = cv_out.shape[1]
    s5 = s5_glu.shape[1]
    n_groups_s5, n_state = s5_lambda_re.shape[2], s5_lambda_re.shape[3]
    assert dec_batch == SUBLANES and batch % SUBLANES == 0
    assert ls % GRID_W == 0 and lc % CHUNK_TILES == 0
    assert rg_wa.shape[-1] == LANES
    n_ctx_units = batch // SUBLANES
    grid_rows = ls // GRID_W
    n_sample_rows = ls * SUBLANES
    units = ((0, ls, True, True),) + tuple(
        (ls + k * lc, lc, False, False) for k in range(n_ctx_units))
    tm = 512
    tn = min(1024, rg, cv, s5)
    tk = min(512, d)
    tc = min(128, lc)
    assert tc % grid_rows == 0 and (GRID_W * grid_rows) % tc == 0
    n_tokens = (ls + n_ctx_units * lc) * SUBLANES
    n_exp = router_w.shape[-1]
    tg = 512
    tmc = 256
    n_ffn_tiles = pl.cdiv(n_tokens * TOP_K, tg) + n_exp

    xs = jnp.transpose(x_sample, (1, 0, 2)).reshape(ls * SUBLANES, d)
    xc = jnp.transpose(x_prompt.reshape(n_ctx_units, SUBLANES, lc, d), (0, 2, 1, 3))
    x = jnp.concatenate([xs, xc.reshape(n_ctx_units * lc * SUBLANES, d)], axis=0)

    cond = jnp.concatenate([c, c_ctx[None], jnp.zeros((SUBLANES - 1, d), F32)], axis=0)
    mod = _ada(cond, ada_w, ada_b)
    mod = jnp.stack([mod[:, :SUBLANES],
                     jnp.broadcast_to(mod[:, SUBLANES:SUBLANES + 1], (depth, SUBLANES, 6 * d))], axis=1)
    mod = mod.reshape(depth, 2, SUBLANES, 6, d)

    cast = lambda a: a.astype(MXU_DTYPE)
    sorted_rows = jnp.zeros((n_ffn_tiles * tg, d // 2), jnp.uint32)
    rg_fin, s5_fin = [], []
    for l in range(depth):
        shift1, scale1, gate1, shift2, scale2, gate2 = (mod[l, :, :, k] for k in range(6))
        u_rg, z_cv, u_s5, gates = _win(
            x, norm_mix[l][None], scale1, shift1, cast(w_in[l]), b_gate[l][None],
            rg=rg, cv=cv, s5=s5, n_sample_rows=n_sample_rows, tm=tm, tn=tn)

        h0_rg = jnp.transpose(state_rglru[:, l], (1, 0, 2))
        heads = rg // LANES
        y_rg, fin_rg = _rglru(
            u_rg, rg_conv_w[l], rg_conv_b[l][None],
            cast(rg_wa[l]), cast(rg_wx[l]), rg_ba[l], rg_bx[l], rg_lambda[l], h0_rg, units=units)
        rg_fin.append(fin_rg)

        dw_tiles = jnp.broadcast_to(cv_dw[l][:, None, :], (cv_dw.shape[1], SUBLANES, cv))
        zc = _conv_branch(z_cv, dw_tiles, cv_db[l][None], cv_ln_g[l][None], cv_ln_b[l][None],
                          n_sample_chunks=ls // CHUNK_TILES, ctx_unit_chunks=lc // CHUNK_TILES)

        a_re, a_im, bmat, cmat = _s5_discretise(
            s5_lambda_re[l], s5_lambda_im[l], s5_log_dt[l], s5_b_re[l], s5_b_im[l],
            s5_c_re[l], s5_c_im[l])
        nb = s5 // LANES
        st = jnp.transpose(state_s5[:, l], (1, 4, 0, 2, 3))
        st = st.reshape(2, 2, SUBLANES, nb, (n_groups_s5 // nb) * n_state)
        h0_s5 = jnp.transpose(st, (0, 3, 2, 1, 4)).reshape(2, nb, SUBLANES, -1)
        y_s5, fin_s5 = _s5(u_s5, s5_d[l][None], a_re, a_im, bmat, cmat, h0_s5,
                           units=units, tc=tc, grid_rows=grid_rows)
        s5_fin.append(fin_s5)

        x = _merge(y_rg, zc, y_s5, gates, cast(rg_out[l]), cast(cv_out[l]), cast(s5_glu[l]),
                   cast(w_out[l]), x, gate1, n_sample_rows=n_sample_rows, tm=tm, tk=tk)

        hnp, eidx, rank, wtok, counts = _router(
            x, norm_ffn[l][None], scale2, shift2, router_w[l].T, router_b[l][:, None],
            n_sample_rows=n_sample_rows, tm=tm)
        pos, tile_expert, n_used = _slots(counts, eidx, rank, tg=tg, n_tiles=n_ffn_tiles, tb=tm)
        sorted_rows = _dispatch(pos, hnp, sorted_rows, tm=tm)
        ys = _ffn(tile_expert, n_used, sorted_rows, exp_w1, exp_w3, exp_w2, layer=l, tg=tg)
        x = _combine(pos, hnp, wtok, cast(sh_w1[l]), cast(sh_w3[l]), cast(sh_w2[l]), x, gate2, ys,
                     n_sample_rows=n_sample_rows, tm=tmc)

    y_sample, y_prompt = _final_norm(x, norm_final[None], ls=ls, lc=lc, n_ctx_units=n_ctx_units, tm=tm)

    fr = jnp.stack(rg_fin, axis=0)[:, 1:]
    new_state_rglru = jnp.transpose(fr, (1, 3, 0, 2, 4)).reshape(batch, depth, 2, rg)
    fs = jnp.stack(s5_fin, axis=0)[:, 1:]
    nb = s5 // LANES
    fs = fs.reshape(depth, n_ctx_units, 2, nb, SUBLANES, 2, n_groups_s5 // nb, n_state)
    new_state_s5 = jnp.transpose(fs, (1, 4, 0, 2, 3, 6, 7, 5)).reshape(
        batch, depth, 2, n_groups_s5, n_state, 2)
    return y_prompt, y_sample, new_state_rglru, new_state_s5
```

```python
import functools
import math

import jax
import jax.numpy as jnp
from jax import lax
from jax.experimental import pallas as pl
from jax.experimental.pallas import tpu as pltpu

F32 = jnp.float32
MXU_DTYPE = jnp.bfloat16

SUBLANES = 8
LANES = 128
VMEM_LIMIT_BYTES = 56 * 1024 * 1024

GRID_W = 64
RG_C = 8.0
S5_GROUP = 16
TOP_K = 8
N_ROUTE_GROUPS = 8
TOPK_GROUPS = 4
ROUTE_SCALE = 2.5
EPS = 1e-6

CHUNK_TILES = 64
CONV_HALO_TILES = 16


def _params(n_grid_dims):
    return pltpu.CompilerParams(
        dimension_semantics=("arbitrary",) * n_grid_dims,
        vmem_limit_bytes=VMEM_LIMIT_BYTES,
    )


def _mm(a, b):
    return jnp.dot(a.astype(MXU_DTYPE), b.astype(MXU_DTYPE), preferred_element_type=F32)


def _sigmoid(x):
    return 0.5 * jnp.tanh(0.5 * x) + 0.5


def _silu(x):
    return x * _sigmoid(x)


def _tile_bcast_mul_add(y, scale, shift):
    rows, d = y.shape
    y3 = y.reshape(rows // SUBLANES, SUBLANES, d)
    return (y3 * scale[None] + shift[None]).reshape(rows, d)


def _gated_residual(x, gate, y):
    rows, d = y.shape
    return (y.reshape(rows // SUBLANES, SUBLANES, d) * gate[None]).reshape(rows, d) + x


def _norm_mod(x, g, scale, shift):
    ms = jnp.mean(x * x, axis=-1, keepdims=True)
    y = x * lax.rsqrt(ms + EPS) * g
    return _tile_bcast_mul_add(y, 1.0 + scale, shift)


def _ada_kernel(c_ref, w_ref, b_ref, o_ref):
    o_ref[0] = _mm(_silu(c_ref[...]), w_ref[0]) + b_ref[0]


def _ada(cond, ada_w, ada_b):
    depth, d, n = ada_w.shape
    tn = math.gcd(n, 1024)
    rows = cond.shape[0]
    return pl.pallas_call(
        _ada_kernel,
        grid=(depth, n // tn),
        in_specs=[
            pl.BlockSpec((rows, d), lambda l, j: (0, 0)),
            pl.BlockSpec((1, d, tn), lambda l, j: (l, 0, j)),
            pl.BlockSpec((1, 1, tn), lambda l, j: (l, 0, j)),
        ],
        out_specs=pl.BlockSpec((1, rows, tn), lambda l, j: (l, 0, j)),
        out_shape=jax.ShapeDtypeStruct((depth, rows, n), F32),
        compiler_params=_params(2),
        name="ada",
    )(cond, ada_w, ada_b.reshape(depth, 1, n))


def _win_kernel(x_ref, g_ref, sc_ref, sh_ref, wa_ref, wb_ref, bg_ref,
                urg_ref, z_ref, us5_ref, gt_ref, hn_ref, *, n_rg, n_cv, n_s5):
    j = pl.program_id(1)

    @pl.when(j == 0)
    def _():
        hn_ref[...] = _norm_mod(x_ref[...], g_ref[...], sc_ref[0], sh_ref[0]).astype(hn_ref.dtype)

    hn = hn_ref[...]
    a = jnp.dot(hn, wa_ref[...], preferred_element_type=F32)

    @pl.when(j < n_rg)
    def _():
        urg_ref[...] = a

    @pl.when((j >= n_rg) & (j < n_rg + n_cv))
    def _():
        b = jnp.dot(hn, wb_ref[...], preferred_element_type=F32)
        z_ref[...] = a * _sigmoid(b)

    @pl.when((j >= n_rg + n_cv) & (j < n_rg + n_cv + n_s5))
    def _():
        us5_ref[...] = a

    @pl.when(j >= n_rg + n_cv + n_s5)
    def _():
        gt_ref[...] = _sigmoid(a + bg_ref[...]).astype(gt_ref.dtype)


def _win(x, g, scale, shift, w_in, b_gate, *, rg, cv, s5, n_sample_rows, tm, tn):
    t, d = x.shape
    n_gate = b_gate.shape[-1]
    n_rg, n_cv, n_s5, n_g = rg // tn, cv // tn, s5 // tn, n_gate // tn
    ns = n_sample_rows // tm
    unit = lambda i, j: ((i >= ns).astype(jnp.int32), 0, 0)
    clip = lambda v, n: jnp.clip(v, 0, n - 1)
    kern = functools.partial(_win_kernel, n_rg=n_rg, n_cv=n_cv, n_s5=n_s5)
    return pl.pallas_call(
        kern,
        grid=(t // tm, n_rg + n_cv + n_s5 + n_g),
        in_specs=[
            pl.BlockSpec((tm, d), lambda i, j: (i, 0)),
            pl.BlockSpec((1, d), lambda i, j: (0, 0)),
            pl.BlockSpec((1, SUBLANES, d), unit),
            pl.BlockSpec((1, SUBLANES, d), unit),
            pl.BlockSpec((d, tn), lambda i, j: (0, jnp.where(j < n_rg + n_cv, j, j + n_cv))),
            pl.BlockSpec((d, tn), lambda i, j: (0, n_rg + n_cv + clip(j - n_rg, n_cv))),
            pl.BlockSpec((1, tn), lambda i, j: (0, clip(j - (n_rg + n_cv + n_s5), n_g))),
        ],
        out_specs=[
            pl.BlockSpec((tm, tn), lambda i, j: (i, clip(j, n_rg))),
            pl.BlockSpec((tm, tn), lambda i, j: (i, clip(j - n_rg, n_cv))),
            pl.BlockSpec((tm, tn), lambda i, j: (i, clip(j - n_rg - n_cv, n_s5))),
            pl.BlockSpec((tm, tn), lambda i, j: (i, clip(j - n_rg - n_cv - n_s5, n_g))),
        ],
        out_shape=[
            jax.ShapeDtypeStruct((t, rg), F32),
            jax.ShapeDtypeStruct((t, cv), F32),
            jax.ShapeDtypeStruct((t, s5), F32),
            jax.ShapeDtypeStruct((t, n_gate), MXU_DTYPE),
        ],
        scratch_shapes=[pltpu.VMEM((tm, d), MXU_DTYPE)],
        compiler_params=_params(2),
        name="win",
    )(x, g, scale, shift, w_in, w_in, b_gate)


def _rg_kernel(u_ref, cw_ref, cb_ref, wa_ref, wx_ref, ba_ref, bx_ref, lam_ref, h0_ref,
               y_ref, fin_ref, ext_ref, a_ref, b_ref, *, units, ch, kconv):
    rows = ch * SUBLANES
    pad_lo = (kconv // 2) * SUBLANES
    pad_hi = (kconv - 1 - kconv // 2) * SUBLANES
    t_rows = u_ref.shape[0]
    for d in (0, 1):
        neg_lam = -lam_ref[d:d + 1, :]
        softplus = jnp.maximum(neg_lam, 0.0) + jnp.log1p(jnp.exp(-jnp.abs(neg_lam)))
        coef = -RG_C * softplus
        wa = wa_ref[d, 0]
        wx = wx_ref[d, 0]
        ba = ba_ref[d:d + 1, :]
        bx = bx_ref[d:d + 1, :]
        for ui, (t0, nt, _, has_h0) in enumerate(units):
            nch = nt // ch

            def chunk_body(ci, h, d=d, t0=t0, nch=nch, coef=coef, wa=wa, wx=wx, ba=ba, bx=bx):
                c = ci if d == 0 else nch - 1 - ci
                r0 = pl.multiple_of((t0 + c * ch) * SUBLANES, SUBLANES)
                lo_start = pl.multiple_of(jnp.maximum(r0 - pad_lo, 0), SUBLANES)
                hi_start = pl.multiple_of(jnp.minimum(r0 + rows, t_rows - pad_hi), SUBLANES)
                ext_ref[0:pad_lo, :] = jnp.where(c > 0, u_ref[pl.ds(lo_start, pad_lo), :], 0.0)
                ext_ref[pad_lo:pad_lo + rows, :] = u_ref[pl.ds(r0, rows), :]
                ext_ref[pad_lo + rows:pad_lo + rows + pad_hi, :] = jnp.where(
                    c < nch - 1, u_ref[pl.ds(hi_start, pad_hi), :], 0.0)
                xc = cb_ref[...] + cw_ref[0:1, :] * ext_ref[0:rows, :]
                for k in range(1, kconv):
                    xc = xc + cw_ref[k:k + 1, :] * ext_ref[k * SUBLANES:k * SUBLANES + rows, :]
                r = _sigmoid(_mm(xc, wa) + ba)
                i = _sigmoid(_mm(xc, wx) + bx)
                a = jnp.exp(coef * r)
                a_ref[...] = a
                b_ref[...] = jnp.sqrt(1.0 - a * a) * i * xc

                def step(s, h):
                    tt = s if d == 0 else ch - 1 - s
                    o = pl.multiple_of(tt * SUBLANES, SUBLANES)
                    h = a_ref[pl.ds(o, SUBLANES), :] * h + b_ref[pl.ds(o, SUBLANES), :]
                    dst = pl.ds(pl.multiple_of(r0 + o, SUBLANES), SUBLANES)
                    if d == 0:
                        y_ref[dst, :] = h
                    else:
                        y_ref[dst, :] = y_ref[dst, :] + h
                    return h

                return lax.fori_loop(0, ch, step, h, unroll=8)

            h_init = h0_ref[d] if has_h0 else jnp.zeros((SUBLANES, LANES), F32)
            fin_ref[ui, d] = lax.fori_loop(0, nch, chunk_body, h_init)


def _rglru(u, conv_w, conv_b, wa, wx, ba, bx, lam, h0, *, units):
    t, rg = u.shape
    heads = rg // LANES
    kconv = conv_w.shape[0]
    n_units = len(units)
    rows = CHUNK_TILES * SUBLANES
    kern = functools.partial(_rg_kernel, units=units, ch=CHUNK_TILES, kconv=kconv)
    col = lambda h: (0, h)
    return pl.pallas_call(
        kern,
        grid=(heads,),
        in_specs=[
            pl.BlockSpec((t, LANES), col),
            pl.BlockSpec((kconv, LANES), col),
            pl.BlockSpec((1, LANES), col),
            pl.BlockSpec((2, 1, LANES, LANES), lambda h: (0, h, 0, 0)),
            pl.BlockSpec((2, 1, LANES, LANES), lambda h: (0, h, 0, 0)),
            pl.BlockSpec((2, LANES), col),
            pl.BlockSpec((2, LANES), col),
            pl.BlockSpec((2, LANES), col),
            pl.BlockSpec((2, SUBLANES, LANES), lambda h: (0, 0, h)),
        ],
        out_specs=[
            pl.BlockSpec((t, LANES), col),
            pl.BlockSpec((n_units, 2, SUBLANES, LANES), lambda h: (0, 0, 0, h)),
        ],
        out_shape=[
            jax.ShapeDtypeStruct((t, rg), F32),
            jax.ShapeDtypeStruct((n_units, 2, SUBLANES, rg), F32),
        ],
        scratch_shapes=[
            pltpu.VMEM((rows + (kconv - 1) * SUBLANES, LANES), F32),
            pltpu.VMEM((rows, LANES), F32),
            pltpu.VMEM((rows, LANES), F32),
        ],
        compiler_params=_params(1),
        name="rglru",
    )(u, conv_w, conv_b, wa, wx, ba, bx, lam, h0)


def _cv_kernel(zp_ref, zc_ref, zn_ref, w_ref, db_ref, lg_ref, lb_ref, o_ref, ext_ref, acc_ref,
               *, n_sample_chunks, ctx_unit_chunks, kc):
    i = pl.program_id(0)
    rows, c = zc_ref.shape
    hr = CONV_HALO_TILES * SUBLANES
    group = SUBLANES
    is_ctx = i >= n_sample_chunks
    cpos = (i - n_sample_chunks) % ctx_unit_chunks
    lo_ok = is_ctx & (cpos > 0)
    hi_ok = is_ctx & (cpos < ctx_unit_chunks - 1)
    ext_ref[0:hr, :] = jnp.where(lo_ok, zp_ref[rows - hr:rows, :], 0.0)
    ext_ref[hr:hr + rows, :] = zc_ref[...]
    ext_ref[hr + rows:hr + rows + hr, :] = jnp.where(hi_ok, zn_ref[0:hr, :], 0.0)
    first_tap_tile = CONV_HALO_TILES - kc // 2

    def lane_body(lb, carry):
        lanes = pl.ds(pl.multiple_of(lb * LANES, LANES), LANES)
        taps = [w_ref[k, :, lanes] for k in range(kc)]

        def grp_body(g, carry):
            accs = [None] * group
            for e in range(group + kc - 1):
                src = pl.multiple_of((g * group + first_tap_tile + e) * SUBLANES, SUBLANES)
                tile = ext_ref[pl.ds(src, SUBLANES), lanes]
                for t in range(group):
                    k = e - t
                    if 0 <= k < kc:
                        term = taps[k] * tile
                        accs[t] = term if accs[t] is None else accs[t] + term
            for t in range(group):
                dst = pl.multiple_of((g * group + t) * SUBLANES, SUBLANES)
                acc_ref[pl.ds(dst, SUBLANES), lanes] = accs[t]
            return carry

        return lax.fori_loop(0, rows // (group * SUBLANES), grp_body, carry)

    lax.fori_loop(0, c // LANES, lane_body, 0)
    z = acc_ref[...] + db_ref[...]
    mu = jnp.mean(z, axis=-1, keepdims=True)
    zc = z - mu
    var = jnp.mean(zc * zc, axis=-1, keepdims=True)
    y = zc * lax.rsqrt(var + EPS) * lg_ref[...] + lb_ref[...]
    o_ref[...] = _silu(y).astype(o_ref.dtype)


def _conv_branch(z, dw_tiles, db, ln_g, ln_b, *, n_sample_chunks, ctx_unit_chunks):
    t, c = z.shape
    kc = dw_tiles.shape[0]
    rows = CHUNK_TILES * SUBLANES
    n = t // rows
    kern = functools.partial(_cv_kernel, n_sample_chunks=n_sample_chunks,
                             ctx_unit_chunks=ctx_unit_chunks, kc=kc)
    vec = pl.BlockSpec((1, c), lambda i: (0, 0))
    return pl.pallas_call(
        kern,
        grid=(n,),
        in_specs=[
            pl.BlockSpec((rows, c), lambda i: (jnp.maximum(i - 1, 0), 0)),
            pl.BlockSpec((rows, c), lambda i: (i, 0)),
            pl.BlockSpec((rows, c), lambda i: (jnp.minimum(i + 1, n - 1), 0)),
            pl.BlockSpec((kc, SUBLANES, c), lambda i: (0, 0, 0)),
            vec, vec, vec,
        ],
        out_specs=pl.BlockSpec((rows, c), lambda i: (i, 0)),
        out_shape=jax.ShapeDtypeStruct((t, c), MXU_DTYPE),
        scratch_shapes=[
            pltpu.VMEM((rows + 2 * CONV_HALO_TILES * SUBLANES, c), F32),
            pltpu.VMEM((rows, c), F32),
        ],
        compiler_params=_params(1),
        name="conv",
    )(z, z, z, dw_tiles, db, ln_g, ln_b)


def _s5_kernel(u_ref, dsk_ref, are_ref, aim_ref, bm_ref, cm_ref, h0_ref,
               y_ref, fin_ref, lhs_ref, hs_ref, yc_ref, *, units, tc, grid_rows):
    ns = are_ref.shape[-1]

    def tile_of(unit, c, s):
        t0, _, is_grid, _ = unit
        if is_grid:
            return t0 + (s % grid_rows) * GRID_W + c * (tc // grid_rows) + s // grid_rows
        return t0 + c * tc + s

    for d in (0, 1):
        a_re = jnp.broadcast_to(are_ref[d, 0], (SUBLANES, ns))
        a_im = jnp.broadcast_to(aim_ref[d, 0], (SUBLANES, ns))
        for ui, unit in enumerate(units):
            nch = unit[1] // tc

            def chunk_body(ci, h, d=d, unit=unit, nch=nch, a_re=a_re, a_im=a_im):
                c = ci if d == 0 else nch - 1 - ci
                for s in range(tc):
                    r = pl.multiple_of(tile_of(unit, c, s) * SUBLANES, SUBLANES)
                    lhs_ref[s * SUBLANES:(s + 1) * SUBLANES, :] = u_ref[pl.ds(r, SUBLANES), :]
                hs_ref[...] = _mm(lhs_ref[...], bm_ref[d, 0])

                def step(si, h):
                    s = si if d == 0 else tc - 1 - si
                    o = pl.ds(pl.multiple_of(s * SUBLANES, SUBLANES), SUBLANES)
                    h_re, h_im = h
                    n_re = a_re * h_re - a_im * h_im + hs_ref[o, 0:ns]
                    n_im = a_re * h_im + a_im * h_re + hs_ref[o, ns:2 * ns]
                    hs_ref[o, 0:ns] = n_re
                    hs_ref[o, ns:2 * ns] = n_im
                    return n_re, n_im

                h = lax.fori_loop(0, tc, step, h, unroll=8)
                yc_ref[...] = _mm(hs_ref[...], cm_ref[d, 0])
                for s in range(tc):
                    r = pl.multiple_of(tile_of(unit, c, s) * SUBLANES, SUBLANES)
                    sl = slice(s * SUBLANES, (s + 1) * SUBLANES)
                    if d == 0:
                        y_ref[pl.ds(r, SUBLANES), :] = dsk_ref[...] * lhs_ref[sl, :] + yc_ref[sl, :]
                    else:
                        y_ref[pl.ds(r, SUBLANES), :] = y_ref[pl.ds(r, SUBLANES), :] + yc_ref[sl, :]
                return h

            if unit[3]:
                h_init = (h0_ref[d, 0, :, 0:ns], h0_ref[d, 0, :, ns:2 * ns])
            else:
                h_init = (jnp.zeros((SUBLANES, ns), F32), jnp.zeros((SUBLANES, ns), F32))
            f_re, f_im = lax.fori_loop(0, nch, chunk_body, h_init)
            fin_ref[ui, d, 0, :, 0:ns] = f_re
            fin_ref[ui, d, 0, :, ns:2 * ns] = f_im


def _s5(u, dskip, a_re, a_im, bmat, cmat, h0, *, units, tc, grid_rows):
    t, width = u.shape
    nb = width // LANES
    ns = a_re.shape[-1]
    n_units = len(units)
    kern = functools.partial(_s5_kernel, units=units, tc=tc, grid_rows=grid_rows)
    blk4 = lambda *shape: pl.BlockSpec((2, 1) + shape, lambda j: (0, j, 0, 0))
    return pl.pallas_call(
        kern,
        grid=(nb,),
        in_specs=[
            pl.BlockSpec((t, LANES), lambda j: (0, j)),
            pl.BlockSpec((1, LANES), lambda j: (0, j)),
            blk4(1, ns), blk4(1, ns),
            blk4(LANES, 2 * ns), blk4(2 * ns, LANES),
            blk4(SUBLANES, 2 * ns),
        ],
        out_specs=[
            pl.BlockSpec((t, LANES), lambda j: (0, j)),
            pl.BlockSpec((n_units, 2, 1, SUBLANES, 2 * ns), lambda j: (0, 0, j, 0, 0)),
        ],
        out_shape=[
            jax.ShapeDtypeStruct((t, width), F32),
            jax.ShapeDtypeStruct((n_units, 2, nb, SUBLANES, 2 * ns), F32),
        ],
        scratch_shapes=[
            pltpu.VMEM((tc * SUBLANES, LANES), F32),
            pltpu.VMEM((tc * SUBLANES, 2 * ns), F32),
            pltpu.VMEM((tc * SUBLANES, LANES), F32),
        ],
        compiler_params=_params(1),
        name="s5",
    )(u, dskip, a_re, a_im, bmat, cmat, h0)


def _merge_kernel(yrg_ref, zc_ref, ys5_ref, g0_ref, g1_ref, g2_ref, wrg_ref, wcv_ref, wsv_ref,
                  wsg_ref, wo_ref, x_ref, gate_ref, o_ref, acc_ref, rgb_ref, s5b_ref):
    j = pl.program_id(1)

    @pl.when(j == 0)
    def _():
        acc_ref[...] = jnp.zeros_like(acc_ref)
        rgb_ref[...] = yrg_ref[...].astype(rgb_ref.dtype)
        s5b_ref[...] = ys5_ref[...].astype(s5b_ref.dtype)

    dot = lambda a, b: jnp.dot(a, b, preferred_element_type=F32)
    br_rg = dot(rgb_ref[...], wrg_ref[...])
    br_cv = dot(zc_ref[...], wcv_ref[...])
    s5b = s5b_ref[...]
    br_s5 = dot(s5b, wsv_ref[...]) * _sigmoid(dot(s5b, wsg_ref[...]))
    merged = (g0_ref[...].astype(F32) * br_rg + g1_ref[...].astype(F32) * br_cv
              + g2_ref[...].astype(F32) * br_s5)
    acc_ref[...] += dot(merged.astype(MXU_DTYPE), wo_ref[...])

    @pl.when(j == pl.num_programs(1) - 1)
    def _():
        o_ref[...] = _gated_residual(x_ref[...], gate_ref[0], acc_ref[...])


def _merge(y_rg, zc, y_s5, gates, rg_out, cv_out, s5_glu, w_out, x, gate, *, n_sample_rows, tm, tk):
    t, d = x.shape
    rg, cv, s5 = y_rg.shape[1], zc.shape[1], y_s5.shape[1]
    nk = d // tk
    ns = n_sample_rows // tm
    row = lambda w: pl.BlockSpec((tm, w), lambda i, j: (i, 0))
    gspec = lambda k: pl.BlockSpec((tm, tk), lambda i, j: (i, k * nk + j))
    wcol = lambda rows, off: pl.BlockSpec((rows, tk), lambda i, j: (0, off + j))
    return pl.pallas_call(
        _merge_kernel,
        grid=(t // tm, nk),
        in_specs=[
            row(rg), row(cv), row(s5),
            gspec(0), gspec(1), gspec(2),
            wcol(rg, 0), wcol(cv, 0), wcol(s5, 0), wcol(s5, nk),
            pl.BlockSpec((tk, d), lambda i, j: (j, 0)),
            row(d),
            pl.BlockSpec((1, SUBLANES, d), lambda i, j: ((i >= ns).astype(jnp.int32), 0, 0)),
        ],
        out_specs=row(d),
        out_shape=jax.ShapeDtypeStruct((t, d), F32),
        scratch_shapes=[
            pltpu.VMEM((tm, d), F32),
            pltpu.VMEM((tm, rg), MXU_DTYPE),
            pltpu.VMEM((tm, s5), MXU_DTYPE),
        ],
        compiler_params=_params(2),
        name="merge",
    )(y_rg, zc, y_s5, gates, gates, gates, rg_out, cv_out, s5_glu, s5_glu, w_out, x, gate)


def _pack_bf16_pairs(v):
    half = v.shape[1] // 2
    bits = lambda a: lax.bitcast_convert_type(a.astype(jnp.bfloat16).astype(F32), jnp.uint32)
    return (bits(v[:, :half]) >> 16) | (bits(v[:, half:]) & jnp.uint32(0xFFFF0000))


def _store_token_slabs(ref, packed):
    n, width = packed.shape
    r = width // LANES
    for s in range(r):
        ref[pl.ds(s, n, stride=r), :] = packed[:, s * LANES:(s + 1) * LANES]


def _load_token_slabs(ref, first_row, n, r):
    return jnp.concatenate(
        [ref[pl.ds(first_row + s, n, stride=r), :] for s in range(r)], axis=1)


def _unpack_bf16_pairs(p):
    lo = lax.bitcast_convert_type(p << 16, F32)
    hi = lax.bitcast_convert_type(p & jnp.uint32(0xFFFF0000), F32)
    return lo, hi


def _router_kernel(x_ref, g_ref, sc_ref, sh_ref, rw_ref, rb_ref,
                   hnp_ref, eidx_ref, rank_ref, wtok_ref, cnt_ref, run_ref):
    @pl.when(pl.program_id(0) == 0)
    def _():
        run_ref[...] = jnp.zeros_like(run_ref)

    hn = _norm_mod(x_ref[...], g_ref[...], sc_ref[0], sh_ref[0])
    _store_token_slabs(hnp_ref, _pack_bf16_pairs(hn))
    logits = lax.dot_general(rw_ref[...], hn, (((1,), (1,)), ((), ())),
                             precision=lax.Precision.HIGHEST, preferred_element_type=F32)
    s = _sigmoid(logits)
    choice = s + rb_ref[...]
    n_exp, tm = choice.shape
    gsize = n_exp // N_ROUTE_GROUPS
    neg_inf = jnp.float32(-jnp.inf)
    c3 = choice.reshape(N_ROUTE_GROUPS, gsize, tm)
    sub = lax.broadcasted_iota(jnp.int32, c3.shape, 1)
    m1 = jnp.max(c3, axis=1, keepdims=True)
    i1 = jnp.min(jnp.where(c3 == m1, sub, gsize), axis=1, keepdims=True)
    m2 = jnp.max(jnp.where(sub == i1, neg_inf, c3), axis=1, keepdims=True)
    gscore = jnp.broadcast_to(m1 + m2, c3.shape)
    gidx = lax.broadcasted_iota(jnp.int32, c3.shape, 0)
    beaten = jnp.zeros(c3.shape, jnp.int32)
    for gp in range(N_ROUTE_GROUPS):
        other = gscore[gp:gp + 1]
        wins = (other > gscore) | ((other == gscore) & (gidx > gp))
        beaten = beaten + wins.astype(jnp.int32)
    masked = jnp.where(beaten < TOPK_GROUPS, c3, neg_inf).reshape(n_exp, tm)
    eidx = lax.broadcasted_iota(jnp.int32, masked.shape, 0)
    beaten = jnp.zeros(masked.shape, jnp.int32)
    for ep in range(n_exp):
        other = masked[ep:ep + 1, :]
        wins = (other > masked) | ((other == masked) & (eidx > ep))
        beaten = beaten + wins.astype(jnp.int32)
    sel = beaten < TOP_K
    w = jnp.where(sel, s, 0.0)
    comb = ROUTE_SCALE * w / jnp.sum(w, axis=0, keepdims=True)
    self32 = sel.astype(F32)
    before = (lax.broadcasted_iota(jnp.int32, (tm, tm), 0) < lax.broadcasted_iota(jnp.int32, (tm, tm), 1))
    rank = run_ref[...] + _mm(self32, before.astype(F32))
    run_ref[...] = run_ref[...] + jnp.sum(self32, axis=1, keepdims=True)
    cnt_ref[...] = jnp.broadcast_to(run_ref[...], cnt_ref.shape)
    eidx_f = eidx.astype(F32)
    w_rows = []
    for j in range(TOP_K):
        m = beaten == j
        pick = lambda v: jnp.sum(jnp.where(m, v, 0.0), axis=0, keepdims=True)
        eidx_ref[j:j + 1, :] = pick(eidx_f).astype(jnp.int32)
        rank_ref[j:j + 1, :] = pick(rank).astype(jnp.int32)
        w_rows.append(pick(comb))
    w_rows.append(jnp.zeros((wtok_ref.shape[1] - TOP_K, tm), F32))
    wtok_ref[...] = jnp.concatenate(w_rows, axis=0).T


def _router(x, g, scale, shift, router_wt, router_b, *, n_sample_rows, tm):
    t, d = x.shape
    n_exp = router_wt.shape[0]
    ns = n_sample_rows // tm
    unit = lambda i: ((i >= ns).astype(jnp.int32), 0, 0)
    rpt = d // 2 // LANES
    return pl.pallas_call(
        _router_kernel,
        grid=(t // tm,),
        in_specs=[
            pl.BlockSpec((tm, d), lambda i: (i, 0)),
            pl.BlockSpec((1, d), lambda i: (0, 0)),
            pl.BlockSpec((1, SUBLANES, d), unit),
            pl.BlockSpec((1, SUBLANES, d), unit),
            pl.BlockSpec((n_exp, d), lambda i: (0, 0)),
            pl.BlockSpec((n_exp, 1), lambda i: (0, 0)),
        ],
        out_specs=[
            pl.BlockSpec((tm * rpt, LANES), lambda i: (i, 0)),
            pl.BlockSpec((TOP_K, tm), lambda i: (0, i)),
            pl.BlockSpec((TOP_K, tm), lambda i: (0, i)),
            pl.BlockSpec((tm, LANES), lambda i: (i, 0)),
            pl.BlockSpec((n_exp, LANES), lambda i: (0, 0)),
        ],
        out_shape=[
            jax.ShapeDtypeStruct((t * rpt, LANES), jnp.uint32),
            jax.ShapeDtypeStruct((TOP_K, t), jnp.int32),
            jax.ShapeDtypeStruct((TOP_K, t), jnp.int32),
            jax.ShapeDtypeStruct((t, LANES), F32),
            jax.ShapeDtypeStruct((n_exp, LANES), F32),
        ],
        scratch_shapes=[pltpu.VMEM((n_exp, 1), F32)],
        compiler_params=_params(1),
        name="router",
    )(x, g, scale, shift, router_wt, router_b)


def _swiglu(lo, hi, w1, w3, w2):
    half = lo.shape[1]
    lo = lo.astype(MXU_DTYPE)
    hi = hi.astype(MXU_DTYPE)
    dot = lambda a, b: jnp.dot(a, b, preferred_element_type=F32)
    h1 = dot(lo, w1[0:half, :]) + dot(hi, w1[half:, :])
    h3 = dot(lo, w3[0:half, :]) + dot(hi, w3[half:, :])
    return dot((_silu(h1) * h3).astype(MXU_DTYPE), w2[...])


def _dispatch_kernel(pos_ref, hn_ref, xs_in_hbm, xs_hbm, sem):
    del xs_in_hbm
    n_slots, tm = pos_ref.shape
    rpt = hn_ref.shape[0] // tm

    def slab(k):
        return pl.ds(pl.multiple_of(k * rpt, rpt), rpt)

    def issue(t, carry):
        src = hn_ref.at[slab(t)]
        for j in range(n_slots):
            pltpu.make_async_copy(src, xs_hbm.at[slab(pos_ref[j, t])], sem).start(priority=j % 2)
        return carry

    lax.fori_loop(0, tm, issue, 0)
    for j in range(n_slots):
        pltpu.make_async_copy(hn_ref, xs_hbm.at[pl.ds(0, tm * rpt)], sem).wait()


def _dispatch(pos, hnp, xs_init, *, tm):
    n_slots, t = pos.shape
    rpt = hnp.shape[0] // t
    return pl.pallas_call(
        _dispatch_kernel,
        grid=(t // tm,),
        in_specs=[
            pl.BlockSpec((n_slots, tm), lambda i: (0, i), memory_space=pltpu.SMEM),
            pl.BlockSpec((tm * rpt, LANES), lambda i: (i, 0)),
            pl.BlockSpec(memory_space=pl.ANY),
        ],
        out_specs=pl.BlockSpec(memory_space=pl.ANY),
        out_shape=jax.ShapeDtypeStruct(xs_init.shape, xs_init.dtype),
        scratch_shapes=[pltpu.SemaphoreType.DMA],
        input_output_aliases={2: 0},
        compiler_params=_params(1),
        name="dispatch",
    )(pos, hnp, xs_init)


def _ffn_kernel(te_ref, nu_ref, xs_ref, w1_ref, w3_ref, w2_ref, ys_ref, w1b_ref, w3b_ref, w2b_ref):
    i = pl.program_id(0)

    @pl.when((i == 0) | (te_ref[i] != te_ref[jnp.maximum(i - 1, 0)]))
    def _():
        w1b_ref[...] = w1_ref[0, 0].astype(w1b_ref.dtype)
        w3b_ref[...] = w3_ref[0, 0].astype(w3b_ref.dtype)
        w2b_ref[...] = w2_ref[0, 0].astype(w2b_ref.dtype)

    @pl.when(i < nu_ref[0])
    def _():
        rpt = w1b_ref.shape[0] // 2 // LANES
        lo, hi = _unpack_bf16_pairs(_load_token_slabs(xs_ref, 0, xs_ref.shape[0] // rpt, rpt))
        _store_token_slabs(ys_ref, _pack_bf16_pairs(_swiglu(lo, hi, w1b_ref, w3b_ref, w2b_ref)))

    @pl.when(i >= nu_ref[0])
    def _():
        ys_ref[...] = jnp.zeros_like(ys_ref)


def _ffn(tile_expert, n_used, xs, w1, w3, w2, *, layer, tg):
    _, _, d, f = w1.shape
    rpt = d // 2 // LANES
    n_tiles = xs.shape[0] // (tg * rpt)
    row = lambda i, te, nu: (jnp.maximum(jnp.minimum(i, nu[0] - 1), 0), 0)
    wblk = lambda i, te, nu: (layer, te[i], 0, 0)
    grid_spec = pltpu.PrefetchScalarGridSpec(
        num_scalar_prefetch=2,
        grid=(n_tiles,),
        in_specs=[
            pl.BlockSpec((tg * rpt, LANES), row),
            pl.BlockSpec((1, 1, d, f), wblk),
            pl.BlockSpec((1, 1, d, f), wblk),
            pl.BlockSpec((1, 1, f, d), wblk),
        ],
        out_specs=pl.BlockSpec((tg * rpt, LANES), lambda i, te, nu: (i, 0)),
        scratch_shapes=[
            pltpu.VMEM((d, f), MXU_DTYPE),
            pltpu.VMEM((d, f), MXU_DTYPE),
            pltpu.VMEM((f, d), MXU_DTYPE),
        ],
    )
    return pl.pallas_call(
        _ffn_kernel,
        grid_spec=grid_spec,
        out_shape=jax.ShapeDtypeStruct(xs.shape, jnp.uint32),
        compiler_params=_params(1),
        name="ffn",
    )(tile_expert, n_used, xs, w1, w3, w2)


def _combine_kernel(pos_ref, hnp_ref, wtok_ref, s1_ref, s3_ref, s2_ref, x_ref, gate_ref, ys_hbm,
                    o_ref, buf_ref, sem):
    n_slots, tm = pos_ref.shape
    rpt = hnp_ref.shape[0] // tm
    slab_rows = tm * rpt

    def slab(k):
        return pl.ds(pl.multiple_of(k * rpt, rpt), rpt)

    def issue(t, carry):
        for j in range(n_slots):
            pltpu.make_async_copy(ys_hbm.at[slab(pos_ref[j, t])],
                                  buf_ref.at[slab(j * tm + t)], sem).start(priority=j % 2)
        return carry

    lax.fori_loop(0, tm, issue, 0)
    lo, hi = _unpack_bf16_pairs(_load_token_slabs(hnp_ref, 0, tm, rpt))
    moe = _swiglu(lo, hi, s1_ref, s3_ref, s2_ref)
    half = lo.shape[1]
    acc_lo = moe[:, :half]
    acc_hi = moe[:, half:]
    wtok = wtok_ref[...]
    for j in range(n_slots):
        pltpu.make_async_copy(ys_hbm.at[pl.ds(0, slab_rows)],
                              buf_ref.at[pl.ds(j * slab_rows, slab_rows)], sem).wait()
    for j in range(n_slots):
        y_lo, y_hi = _unpack_bf16_pairs(_load_token_slabs(buf_ref, j * slab_rows, tm, rpt))
        wj = wtok[:, j:j + 1]
        acc_lo = acc_lo + wj * y_lo
        acc_hi = acc_hi + wj * y_hi
    x = x_ref[...]
    gate = gate_ref[0]
    o_ref[:, :half] = _gated_residual(x[:, :half], gate[:, :half], acc_lo)
    o_ref[:, half:] = _gated_residual(x[:, half:], gate[:, half:], acc_hi)


def _combine(pos, hnp, wtok, s1, s3, s2, x, gate, ys, *, n_sample_rows, tm):
    t, d = x.shape
    n_slots = pos.shape[0]
    f = s1.shape[1]
    ns = n_sample_rows // tm
    rpt = d // 2 // LANES
    full = lambda shape: pl.BlockSpec(shape, lambda i: (0,) * len(shape))
    return pl.pallas_call(
        _combine_kernel,
        grid=(t // tm,),
        in_specs=[
            pl.BlockSpec((n_slots, tm), lambda i: (0, i), memory_space=pltpu.SMEM),
            pl.BlockSpec((tm * rpt, LANES), lambda i: (i, 0)),
            pl.BlockSpec((tm, LANES), lambda i: (i, 0)),
            full((d, f)), full((d, f)), full((f, d)),
            pl.BlockSpec((tm, d), lambda i: (i, 0)),
            pl.BlockSpec((1, SUBLANES, d), lambda i: ((i >= ns).astype(jnp.int32), 0, 0)),
            pl.BlockSpec(memory_space=pl.ANY),
        ],
        out_specs=pl.BlockSpec((tm, d), lambda i: (i, 0)),
        out_shape=jax.ShapeDtypeStruct((t, d), F32),
        scratch_shapes=[
            pltpu.VMEM((n_slots * tm * rpt, LANES), jnp.uint32),
            pltpu.SemaphoreType.DMA,
        ],
        compiler_params=_params(1),
        name="combine",
    )(pos, hnp, wtok, s1, s3, s2, x, gate, ys)


def _slots_kernel(cnt_ref, eidx_ref, rank_ref, pos_ref, te_ref, nu_ref, *, tg):
    counts = cnt_ref[...]
    n_exp = counts.shape[0]
    padded = jnp.floor((counts + (tg - 1)) * (1.0 / tg)) * tg
    incl = (lax.broadcasted_iota(jnp.int32, (n_exp, n_exp), 0)
            >= lax.broadcasted_iota(jnp.int32, (n_exp, n_exp), 1)).astype(F32)
    ends = jnp.dot(incl, padded, precision=lax.Precision.HIGHEST, preferred_element_type=F32)
    ends_col = ends[:, 0:1]
    offs_col = ends_col - padded[:, 0:1]
    eidx = eidx_ref[...]
    rank = rank_ref[...]
    expert = lax.broadcasted_iota(jnp.int32, (n_exp, eidx.shape[1]), 0)
    for j in range(eidx.shape[0]):
        off = jnp.sum(jnp.where(expert == eidx[j:j + 1, :], offs_col, 0.0), axis=0, keepdims=True)
        pos_ref[j:j + 1, :] = off.astype(jnp.int32) + rank[j:j + 1, :]
    n_used = ends[n_exp - 1:n_exp, 0:1] * (1.0 / tg)
    tile = lax.broadcasted_iota(jnp.int32, te_ref.shape, 1).astype(F32)
    tile_row = jnp.minimum(tile, n_used - 1.0) * tg
    te_ref[...] = jnp.sum((ends_col <= tile_row).astype(F32), axis=0, keepdims=True).astype(jnp.int32)
    nu_ref[...] = jnp.broadcast_to(n_used, nu_ref.shape).astype(jnp.int32)


def _slots(counts, eidx, rank, *, tg, n_tiles, tb):
    assert tg & (tg - 1) == 0
    n_slots, t = eidx.shape
    n_tiles_pad = pl.cdiv(n_tiles, LANES) * LANES
    pos, te, nu = pl.pallas_call(
        functools.partial(_slots_kernel, tg=tg),
        grid=(t // tb,),
        in_specs=[
            pl.BlockSpec(counts.shape, lambda i: (0, 0)),
            pl.BlockSpec((n_slots, tb), lambda i: (0, i)),
            pl.BlockSpec((n_slots, tb), lambda i: (0, i)),
        ],
        out_specs=[
            pl.BlockSpec((n_slots, tb), lambda i: (0, i)),
            pl.BlockSpec((1, n_tiles_pad), lambda i: (0, 0)),
            pl.BlockSpec((1, LANES), lambda i: (0, 0)),
        ],
        out_shape=[
            jax.ShapeDtypeStruct((n_slots, t), jnp.int32),
            jax.ShapeDtypeStruct((1, n_tiles_pad), jnp.int32),
            jax.ShapeDtypeStruct((1, LANES), jnp.int32),
        ],
        compiler_params=_params(1),
        name="slots",
    )(counts, eidx, rank)
    return pos, te[0, :n_tiles], nu[0, :1]


def _final_norm_kernel(x_ref, g_ref, os_ref, oc_ref, y_ref, *, n_sample_tiles):
    x = x_ref[...]
    ms = jnp.mean(x * x, axis=-1, keepdims=True)
    y = x * lax.rsqrt(ms + EPS) * g_ref[...]
    steps = x.shape[0] // SUBLANES
    n_lane_blocks = x.shape[1] // LANES
    for k in range(n_lane_blocks):
        y_ref[k] = y[:, k * LANES:(k + 1) * LANES]

    def emit(o_ref):
        for b in range(SUBLANES):
            for k in range(n_lane_blocks):
                o_ref[b, :, k * LANES:(k + 1) * LANES] = y_ref[k, pl.ds(b, steps, stride=SUBLANES), :]

    @pl.when(pl.program_id(0) < n_sample_tiles)
    def _():
        emit(os_ref)

    @pl.when(pl.program_id(0) >= n_sample_tiles)
    def _():
        emit(oc_ref)


def _final_norm(x, g, *, ls, lc, n_ctx_units, tm):
    t, d = x.shape
    steps = tm // SUBLANES
    nst = ls // steps
    ctx_tiles = lc // steps
    n_ctx = n_ctx_units * ctx_tiles

    def ctx_block(i):
        k = jnp.clip(i - nst, 0, n_ctx - 1)
        return (k // ctx_tiles, k % ctx_tiles, 0)

    return pl.pallas_call(
        functools.partial(_final_norm_kernel, n_sample_tiles=nst),
        grid=(t // tm,),
        in_specs=[pl.BlockSpec((tm, d), lambda i: (i, 0)), pl.BlockSpec((1, d), lambda i: (0, 0))],
        out_specs=[
            pl.BlockSpec((SUBLANES, steps, d), lambda i: (0, jnp.minimum(i, nst - 1), 0)),
            pl.BlockSpec((SUBLANES, steps, d), ctx_block),
        ],
        out_shape=[
            jax.ShapeDtypeStruct((SUBLANES, ls, d), F32),
            jax.ShapeDtypeStruct((n_ctx_units * SUBLANES, lc, d), F32),
        ],
        scratch_shapes=[pltpu.VMEM((d // LANES, tm, LANES), F32)],
        compiler_params=_params(1),
        name="final_norm",
    )(x, g)


def _s5_discretise(lam_re, lam_im, log_dt, b_re, b_im, c_re, c_im):
    two, g, p = lam_re.shape
    s = b_re.shape[-1]
    gb = LANES // s
    nb = g // gb
    dt = jnp.exp(log_dt)[..., None]
    mag = jnp.exp(lam_re * dt)
    a_re = mag * jnp.cos(lam_im * dt)
    a_im = mag * jnp.sin(lam_im * dt)
    den = lam_re * lam_re + lam_im * lam_im
    q_re = ((a_re - 1.0) * lam_re + a_im * lam_im) / den
    q_im = (a_im * lam_re - (a_re - 1.0) * lam_im) / den
    bb_re = q_re[..., None] * b_re - q_im[..., None] * b_im
    bb_im = q_re[..., None] * b_im + q_im[..., None] * b_re
    eye = jnp.eye(gb, dtype=F32)

    def bdiag_b(m):
        m = m.reshape(two, nb, gb, p, s)
        return jnp.einsum("dbgps,gh->dbgshp", m, eye).reshape(two, nb, gb * s, gb * p)

    def bdiag_c(m):
        m = m.reshape(two, nb, gb, s, p)
        return jnp.einsum("dbgsp,gh->dbgphs", m, eye).reshape(two, nb, gb * p, gb * s)

    bmat = jnp.concatenate([bdiag_b(bb_re), bdiag_b(bb_im)], axis=-1)
    cmat = jnp.concatenate([bdiag_c(c_re), -bdiag_c(c_im)], axis=-2)
    blk = lambda a: a.reshape(two, nb, 1, gb * p)
    return blk(a_re), blk(a_im), bmat.astype(MXU_DTYPE), cmat.astype(MXU_DTYPE)


def kernel(x_prompt, x_sample, state_rglru, state_s5, c, c_ctx, norm_mix, norm_ffn, ada_w, ada_b, w_in, b_gate, rg_conv_w, rg_conv_b, rg_wa, rg_ba, rg_wx, rg_bx, rg_lambda, rg_out, cv_dw, cv_db, cv_ln_g, cv_ln_b, cv_out, s5_lambda_re, s5_lambda_im, s5_log_dt, s5_b_re, s5_b_im, s5_c_re, s5_c_im, s5_d, s5_glu, w_out, router_w, router_b, exp_w1, exp_w3, exp_w2, sh_w1, sh_w3, sh_w2, norm_final):
    batch, lc, d = x_prompt.shape
    dec_batch, ls, _ = x_sample.shape
    depth = w_in.shape[0]
    rg = rg_out.shape[1]
    cv = cv_out.shape[1]
    s5 = s5_glu.shape[1]
    n_groups_s5, n_state = s5_lambda_re.shape[2], s5_lambda_re.shape[3]
    assert dec_batch == SUBLANES and batch % SUBLANES == 0
    assert ls % GRID_W == 0 and lc % CHUNK_TILES == 0
    assert rg_wa.shape[-1] == LANES
    n_ctx_units = batch // SUBLANES
    grid_rows = ls // GRID_W
    n_sample_rows = ls * SUBLANES
    units = ((0, ls, True, True),) + tuple(
        (ls + k * lc, lc, False, False) for k in range(n_ctx_units))
    tm = 512
    tn = min(1024, rg, cv, s5)
    tk = min(512, d)
    tc = min(128, lc)
    assert tc % grid_rows == 0 and (GRID_W * grid_rows) % tc == 0
    n_tokens = (ls + n_ctx_units * lc) * SUBLANES
    n_exp = router_w.shape[-1]
    tg = 512
    tmc = 256
    n_ffn_tiles = pl.cdiv(n_tokens * TOP_K, tg) + n_exp

    xs = jnp.transpose(x_sample, (1, 0, 2)).reshape(ls * SUBLANES, d)
    xc = jnp.transpose(x_prompt.reshape(n_ctx_units, SUBLANES, lc, d), (0, 2, 1, 3))
    x = jnp.concatenate([xs, xc.reshape(n_ctx_units * lc * SUBLANES, d)], axis=0)

    cond = jnp.concatenate([c, c_ctx[None], jnp.zeros((SUBLANES - 1, d), F32)], axis=0)
    mod = _ada(cond, ada_w, ada_b)
    mod = jnp.stack([mod[:, :SUBLANES],
                     jnp.broadcast_to(mod[:, SUBLANES:SUBLANES + 1], (depth, SUBLANES, 6 * d))], axis=1)
    mod = mod.reshape(depth, 2, SUBLANES, 6, d)

    cast = lambda a: a.astype(MXU_DTYPE)
    sorted_rows = jnp.zeros((n_ffn_tiles * tg * (d // 2 // LANES), LANES), jnp.uint32)
    rg_fin, s5_fin = [], []
    for l in range(depth):
        shift1, scale1, gate1, shift2, scale2, gate2 = (mod[l, :, :, k] for k in range(6))
        u_rg, z_cv, u_s5, gates = _win(
            x, norm_mix[l][None], scale1, shift1, cast(w_in[l]), b_gate[l][None],
            rg=rg, cv=cv, s5=s5, n_sample_rows=n_sample_rows, tm=tm, tn=tn)

        h0_rg = jnp.transpose(state_rglru[:, l], (1, 0, 2))
        heads = rg // LANES
        y_rg, fin_rg = _rglru(
            u_rg, rg_conv_w[l], rg_conv_b[l][None],
            cast(rg_wa[l]), cast(rg_wx[l]), rg_ba[l], rg_bx[l], rg_lambda[l], h0_rg, units=units)
        rg_fin.append(fin_rg)

        dw_tiles = jnp.broadcast_to(cv_dw[l][:, None, :], (cv_dw.shape[1], SUBLANES, cv))
        zc = _conv_branch(z_cv, dw_tiles, cv_db[l][None], cv_ln_g[l][None], cv_ln_b[l][None],
                          n_sample_chunks=ls // CHUNK_TILES, ctx_unit_chunks=lc // CHUNK_TILES)

        a_re, a_im, bmat, cmat = _s5_discretise(
            s5_lambda_re[l], s5_lambda_im[l], s5_log_dt[l], s5_b_re[l], s5_b_im[l],
            s5_c_re[l], s5_c_im[l])
        nb = s5 // LANES
        st = jnp.transpose(state_s5[:, l], (1, 4, 0, 2, 3))
        st = st.reshape(2, 2, SUBLANES, nb, (n_groups_s5 // nb) * n_state)
        h0_s5 = jnp.transpose(st, (0, 3, 2, 1, 4)).reshape(2, nb, SUBLANES, -1)
        y_s5, fin_s5 = _s5(u_s5, s5_d[l][None], a_re, a_im, bmat, cmat, h0_s5,
                           units=units, tc=tc, grid_rows=grid_rows)
        s5_fin.append(fin_s5)

        x = _merge(y_rg, zc, y_s5, gates, cast(rg_out[l]), cast(cv_out[l]), cast(s5_glu[l]),
                   cast(w_out[l]), x, gate1, n_sample_rows=n_sample_rows, tm=tm, tk=tk)

        hnp, eidx, rank, wtok, counts = _router(
            x, norm_ffn[l][None], scale2, shift2, router_w[l].T, router_b[l][:, None],
            n_sample_rows=n_sample_rows, tm=tm)
        pos, tile_expert, n_used = _slots(counts, eidx, rank, tg=tg, n_tiles=n_ffn_tiles, tb=tm)
        sorted_rows = _dispatch(pos, hnp, sorted_rows, tm=tm)
        ys = _ffn(tile_expert, n_used, sorted_rows, exp_w1, exp_w3, exp_w2, layer=l, tg=tg)
        x = _combine(pos, hnp, wtok, cast(sh_w1[l]), cast(sh_w3[l]), cast(sh_w2[l]), x, gate2, ys,
                     n_sample_rows=n_sample_rows, tm=tmc)

    y_sample, y_prompt = _final_norm(x, norm_final[None], ls=ls, lc=lc, n_ctx_units=n_ctx_units, tm=tm)

    fr = jnp.stack(rg_fin, axis=0)[:, 1:]
    new_state_rglru = jnp.transpose(fr, (1, 3, 0, 2, 4)).reshape(batch, depth, 2, rg)
    fs = jnp.stack(s5_fin, axis=0)[:, 1:]
    nb = s5 // LANES
    fs = fs.reshape(depth, n_ctx_units, 2, nb, SUBLANES, 2, n_groups_s5 // nb, n_state)
    new_state_s5 = jnp.transpose(fs, (1, 4, 0, 2, 3, 6, 7, 5)).reshape(
        batch, depth, 2, n_groups_s5, n_state, 2)
    return y_prompt, y_sample, new_state_rglru, new_state_s5
```

```python
import functools
import math

import jax
import jax.numpy as jnp
from jax import lax
from jax.experimental import pallas as pl
from jax.experimental.pallas import tpu as pltpu

F32 = jnp.float32
MXU_DTYPE = jnp.bfloat16

SUBLANES = 8
LANES = 128
VMEM_LIMIT_BYTES = 56 * 1024 * 1024

GRID_W = 64
RG_C = 8.0
S5_GROUP = 16
TOP_K = 8
N_ROUTE_GROUPS = 8
TOPK_GROUPS = 4
ROUTE_SCALE = 2.5
EPS = 1e-6

CHUNK_TILES = 64
CONV_HALO_TILES = 16


def _params(n_grid_dims):
    return pltpu.CompilerParams(
        dimension_semantics=("arbitrary",) * n_grid_dims,
        vmem_limit_bytes=VMEM_LIMIT_BYTES,
    )


def _mm(a, b):
    return jnp.dot(a.astype(MXU_DTYPE), b.astype(MXU_DTYPE), preferred_element_type=F32)


def _sigmoid(x):
    return 0.5 * jnp.tanh(0.5 * x) + 0.5


def _silu(x):
    return x * _sigmoid(x)


def _tile_bcast_mul_add(y, scale, shift):
    rows, d = y.shape
    y3 = y.reshape(rows // SUBLANES, SUBLANES, d)
    return (y3 * scale[None] + shift[None]).reshape(rows, d)


def _gated_residual(x, gate, y):
    rows, d = y.shape
    return (y.reshape(rows // SUBLANES, SUBLANES, d) * gate[None]).reshape(rows, d) + x


def _norm_mod(x, g, scale, shift):
    ms = jnp.mean(x * x, axis=-1, keepdims=True)
    y = x * lax.rsqrt(ms + EPS) * g
    return _tile_bcast_mul_add(y, 1.0 + scale, shift)


def _ada_kernel(c_ref, w_ref, b_ref, o_ref):
    o_ref[0] = _mm(_silu(c_ref[...]), w_ref[0]) + b_ref[0]


def _ada(cond, ada_w, ada_b):
    depth, d, n = ada_w.shape
    tn = math.gcd(n, 1024)
    rows = cond.shape[0]
    return pl.pallas_call(
        _ada_kernel,
        grid=(depth, n // tn),
        in_specs=[
            pl.BlockSpec((rows, d), lambda l, j: (0, 0)),
            pl.BlockSpec((1, d, tn), lambda l, j: (l, 0, j)),
            pl.BlockSpec((1, 1, tn), lambda l, j: (l, 0, j)),
        ],
        out_specs=pl.BlockSpec((1, rows, tn), lambda l, j: (l, 0, j)),
        out_shape=jax.ShapeDtypeStruct((depth, rows, n), F32),
        compiler_params=_params(2),
        name="ada",
    )(cond, ada_w, ada_b.reshape(depth, 1, n))


def _win_kernel(x_ref, g_ref, sc_ref, sh_ref, wa_ref, wb_ref, bg_ref,
                urg_ref, z_ref, us5_ref, gt_ref, hn_ref, *, n_rg, n_cv, n_s5):
    j = pl.program_id(1)

    @pl.when(j == 0)
    def _():
        hn_ref[...] = _norm_mod(x_ref[...], g_ref[...], sc_ref[0], sh_ref[0]).astype(hn_ref.dtype)

    hn = hn_ref[...]
    a = jnp.dot(hn, wa_ref[...], preferred_element_type=F32)

    @pl.when(j < n_rg)
    def _():
        urg_ref[...] = a

    @pl.when((j >= n_rg) & (j < n_rg + n_cv))
    def _():
        b = jnp.dot(hn, wb_ref[...], preferred_element_type=F32)
        z_ref[...] = a * _sigmoid(b)

    @pl.when((j >= n_rg + n_cv) & (j < n_rg + n_cv + n_s5))
    def _():
        us5_ref[...] = a

    @pl.when(j >= n_rg + n_cv + n_s5)
    def _():
        gt_ref[...] = _sigmoid(a + bg_ref[...]).astype(gt_ref.dtype)


def _win(x, g, scale, shift, w_in, b_gate, *, rg, cv, s5, n_sample_rows, tm, tn):
    t, d = x.shape
    n_gate = b_gate.shape[-1]
    n_rg, n_cv, n_s5, n_g = rg // tn, cv // tn, s5 // tn, n_gate // tn
    ns = n_sample_rows // tm
    unit = lambda i, j: ((i >= ns).astype(jnp.int32), 0, 0)
    clip = lambda v, n: jnp.clip(v, 0, n - 1)
    kern = functools.partial(_win_kernel, n_rg=n_rg, n_cv=n_cv, n_s5=n_s5)
    return pl.pallas_call(
        kern,
        grid=(t // tm, n_rg + n_cv + n_s5 + n_g),
        in_specs=[
            pl.BlockSpec((tm, d), lambda i, j: (i, 0)),
            pl.BlockSpec((1, d), lambda i, j: (0, 0)),
            pl.BlockSpec((1, SUBLANES, d), unit),
            pl.BlockSpec((1, SUBLANES, d), unit),
            pl.BlockSpec((d, tn), lambda i, j: (0, jnp.where(j < n_rg + n_cv, j, j + n_cv))),
            pl.BlockSpec((d, tn), lambda i, j: (0, n_rg + n_cv + clip(j - n_rg, n_cv))),
            pl.BlockSpec((1, tn), lambda i, j: (0, clip(j - (n_rg + n_cv + n_s5), n_g))),
        ],
        out_specs=[
            pl.BlockSpec((tm, tn), lambda i, j: (i, clip(j, n_rg))),
            pl.BlockSpec((tm, tn), lambda i, j: (i, clip(j - n_rg, n_cv))),
            pl.BlockSpec((tm, tn), lambda i, j: (i, clip(j - n_rg - n_cv, n_s5))),
            pl.BlockSpec((tm, tn), lambda i, j: (i, clip(j - n_rg - n_cv - n_s5, n_g))),
        ],
        out_shape=[
            jax.ShapeDtypeStruct((t, rg), F32),
            jax.ShapeDtypeStruct((t, cv), F32),
            jax.ShapeDtypeStruct((t, s5), F32),
            jax.ShapeDtypeStruct((t, n_gate), MXU_DTYPE),
        ],
        scratch_shapes=[pltpu.VMEM((tm, d), MXU_DTYPE)],
        compiler_params=_params(2),
        name="win",
    )(x, g, scale, shift, w_in, w_in, b_gate)


def _rg_kernel(u_ref, cw_ref, cb_ref, wa_ref, wx_ref, ba_ref, bx_ref, lam_ref, h0_ref,
               y_ref, fin_ref, ext_ref, a_ref, b_ref, *, units, ch, kconv):
    rows = ch * SUBLANES
    pad_lo = (kconv // 2) * SUBLANES
    pad_hi = (kconv - 1 - kconv // 2) * SUBLANES
    t_rows = u_ref.shape[0]
    for d in (0, 1):
        neg_lam = -lam_ref[d:d + 1, :]
        softplus = jnp.maximum(neg_lam, 0.0) + jnp.log1p(jnp.exp(-jnp.abs(neg_lam)))
        coef = -RG_C * softplus
        wa = wa_ref[d, 0]
        wx = wx_ref[d, 0]
        ba = ba_ref[d:d + 1, :]
        bx = bx_ref[d:d + 1, :]
        for ui, (t0, nt, _, has_h0) in enumerate(units):
            nch = nt // ch

            def chunk_body(ci, h, d=d, t0=t0, nch=nch, coef=coef, wa=wa, wx=wx, ba=ba, bx=bx):
                c = ci if d == 0 else nch - 1 - ci
                r0 = pl.multiple_of((t0 + c * ch) * SUBLANES, SUBLANES)
                lo_start = pl.multiple_of(jnp.maximum(r0 - pad_lo, 0), SUBLANES)
                hi_start = pl.multiple_of(jnp.minimum(r0 + rows, t_rows - pad_hi), SUBLANES)
                ext_ref[0:pad_lo, :] = jnp.where(c > 0, u_ref[pl.ds(lo_start, pad_lo), :], 0.0)
                ext_ref[pad_lo:pad_lo + rows, :] = u_ref[pl.ds(r0, rows), :]
                ext_ref[pad_lo + rows:pad_lo + rows + pad_hi, :] = jnp.where(
                    c < nch - 1, u_ref[pl.ds(hi_start, pad_hi), :], 0.0)
                xc = cb_ref[...] + cw_ref[0:1, :] * ext_ref[0:rows, :]
                for k in range(1, kconv):
                    xc = xc + cw_ref[k:k + 1, :] * ext_ref[k * SUBLANES:k * SUBLANES + rows, :]
                r = _sigmoid(_mm(xc, wa) + ba)
                i = _sigmoid(_mm(xc, wx) + bx)
                a = jnp.exp(coef * r)
                a_ref[...] = a
                b_ref[...] = jnp.sqrt(1.0 - a * a) * i * xc

                def step(s, h):
                    tt = s if d == 0 else ch - 1 - s
                    o = pl.multiple_of(tt * SUBLANES, SUBLANES)
                    h = a_ref[pl.ds(o, SUBLANES), :] * h + b_ref[pl.ds(o, SUBLANES), :]
                    dst = pl.ds(pl.multiple_of(r0 + o, SUBLANES), SUBLANES)
                    if d == 0:
                        y_ref[dst, :] = h
                    else:
                        y_ref[dst, :] = y_ref[dst, :] + h
                    return h

                return lax.fori_loop(0, ch, step, h, unroll=8)

            h_init = h0_ref[d] if has_h0 else jnp.zeros((SUBLANES, LANES), F32)
            fin_ref[ui, d] = lax.fori_loop(0, nch, chunk_body, h_init)


def _rglru(u, conv_w, conv_b, wa, wx, ba, bx, lam, h0, *, units):
    t, rg = u.shape
    heads = rg // LANES
    kconv = conv_w.shape[0]
    n_units = len(units)
    rows = CHUNK_TILES * SUBLANES
    kern = functools.partial(_rg_kernel, units=units, ch=CHUNK_TILES, kconv=kconv)
    col = lambda h: (0, h)
    return pl.pallas_call(
        kern,
        grid=(heads,),
        in_specs=[
            pl.BlockSpec((t, LANES), col),
            pl.BlockSpec((kconv, LANES), col),
            pl.BlockSpec((1, LANES), col),
            pl.BlockSpec((2, 1, LANES, LANES), lambda h: (0, h, 0, 0)),
            pl.BlockSpec((2, 1, LANES, LANES), lambda h: (0, h, 0, 0)),
            pl.BlockSpec((2, LANES), col),
            pl.BlockSpec((2, LANES), col),
            pl.BlockSpec((2, LANES), col),
            pl.BlockSpec((2, SUBLANES, LANES), lambda h: (0, 0, h)),
        ],
        out_specs=[
            pl.BlockSpec((t, LANES), col),
            pl.BlockSpec((n_units, 2, SUBLANES, LANES), lambda h: (0, 0, 0, h)),
        ],
        out_shape=[
            jax.ShapeDtypeStruct((t, rg), F32),
            jax.ShapeDtypeStruct((n_units, 2, SUBLANES, rg), F32),
        ],
        scratch_shapes=[
            pltpu.VMEM((rows + (kconv - 1) * SUBLANES, LANES), F32),
            pltpu.VMEM((rows, LANES), F32),
            pltpu.VMEM((rows, LANES), F32),
        ],
        compiler_params=_params(1),
        name="rglru",
    )(u, conv_w, conv_b, wa, wx, ba, bx, lam, h0)


def _cv_kernel(zp_ref, zc_ref, zn_ref, w_ref, db_ref, lg_ref, lb_ref, o_ref, ext_ref, acc_ref,
               *, n_sample_chunks, ctx_unit_chunks, kc):
    i = pl.program_id(0)
    rows, c = zc_ref.shape
    hr = CONV_HALO_TILES * SUBLANES
    group = SUBLANES
    is_ctx = i >= n_sample_chunks
    cpos = (i - n_sample_chunks) % ctx_unit_chunks
    lo_ok = is_ctx & (cpos > 0)
    hi_ok = is_ctx & (cpos < ctx_unit_chunks - 1)
    ext_ref[0:hr, :] = jnp.where(lo_ok, zp_ref[rows - hr:rows, :], 0.0)
    ext_ref[hr:hr + rows, :] = zc_ref[...]
    ext_ref[hr + rows:hr + rows + hr, :] = jnp.where(hi_ok, zn_ref[0:hr, :], 0.0)
    first_tap_tile = CONV_HALO_TILES - kc // 2

    def lane_body(lb, carry):
        lanes = pl.ds(pl.multiple_of(lb * LANES, LANES), LANES)
        taps = [w_ref[k, :, lanes] for k in range(kc)]

        def grp_body(g, carry):
            accs = [None] * group
            for e in range(group + kc - 1):
                src = pl.multiple_of((g * group + first_tap_tile + e) * SUBLANES, SUBLANES)
                tile = ext_ref[pl.ds(src, SUBLANES), lanes]
                for t in range(group):
                    k = e - t
                    if 0 <= k < kc:
                        term = taps[k] * tile
                        accs[t] = term if accs[t] is None else accs[t] + term
            for t in range(group):
                dst = pl.multiple_of((g * group + t) * SUBLANES, SUBLANES)
                acc_ref[pl.ds(dst, SUBLANES), lanes] = accs[t]
            return carry

        return lax.fori_loop(0, rows // (group * SUBLANES), grp_body, carry)

    lax.fori_loop(0, c // LANES, lane_body, 0)
    z = acc_ref[...] + db_ref[...]
    mu = jnp.mean(z, axis=-1, keepdims=True)
    zc = z - mu
    var = jnp.mean(zc * zc, axis=-1, keepdims=True)
    y = zc * lax.rsqrt(var + EPS) * lg_ref[...] + lb_ref[...]
    o_ref[...] = _silu(y).astype(o_ref.dtype)


def _conv_branch(z, dw_tiles, db, ln_g, ln_b, *, n_sample_chunks, ctx_unit_chunks):
    t, c = z.shape
    kc = dw_tiles.shape[0]
    rows = CHUNK_TILES * SUBLANES
    n = t // rows
    kern = functools.partial(_cv_kernel, n_sample_chunks=n_sample_chunks,
                             ctx_unit_chunks=ctx_unit_chunks, kc=kc)
    vec = pl.BlockSpec((1, c), lambda i: (0, 0))
    return pl.pallas_call(
        kern,
        grid=(n,),
        in_specs=[
            pl.BlockSpec((rows, c), lambda i: (jnp.maximum(i - 1, 0), 0)),
            pl.BlockSpec((rows, c), lambda i: (i, 0)),
            pl.BlockSpec((rows, c), lambda i: (jnp.minimum(i + 1, n - 1), 0)),
            pl.BlockSpec((kc, SUBLANES, c), lambda i: (0, 0, 0)),
            vec, vec, vec,
        ],
        out_specs=pl.BlockSpec((rows, c), lambda i: (i, 0)),
        out_shape=jax.ShapeDtypeStruct((t, c), MXU_DTYPE),
        scratch_shapes=[
            pltpu.VMEM((rows + 2 * CONV_HALO_TILES * SUBLANES, c), F32),
            pltpu.VMEM((rows, c), F32),
        ],
        compiler_params=_params(1),
        name="conv",
    )(z, z, z, dw_tiles, db, ln_g, ln_b)


def _s5_kernel(u_ref, dsk_ref, are_ref, aim_ref, bm_ref, cm_ref, h0_ref,
               y_ref, fin_ref, lhs_ref, hs_ref, yc_ref, *, units, tc, grid_rows):
    ns = are_ref.shape[-1]

    def tile_of(unit, c, s):
        t0, _, is_grid, _ = unit
        if is_grid:
            return t0 + (s % grid_rows) * GRID_W + c * (tc // grid_rows) + s // grid_rows
        return t0 + c * tc + s

    for d in (0, 1):
        a_re = jnp.broadcast_to(are_ref[d, 0], (SUBLANES, ns))
        a_im = jnp.broadcast_to(aim_ref[d, 0], (SUBLANES, ns))
        for ui, unit in enumerate(units):
            nch = unit[1] // tc

            def chunk_body(ci, h, d=d, unit=unit, nch=nch, a_re=a_re, a_im=a_im):
                c = ci if d == 0 else nch - 1 - ci
                for s in range(tc):
                    r = pl.multiple_of(tile_of(unit, c, s) * SUBLANES, SUBLANES)
                    lhs_ref[s * SUBLANES:(s + 1) * SUBLANES, :] = u_ref[pl.ds(r, SUBLANES), :]
                hs_ref[...] = _mm(lhs_ref[...], bm_ref[d, 0])

                def step(si, h):
                    s = si if d == 0 else tc - 1 - si
                    o = pl.ds(pl.multiple_of(s * SUBLANES, SUBLANES), SUBLANES)
                    h_re, h_im = h
                    n_re = a_re * h_re - a_im * h_im + hs_ref[o, 0:ns]
                    n_im = a_re * h_im + a_im * h_re + hs_ref[o, ns:2 * ns]
                    hs_ref[o, 0:ns] = n_re
                    hs_ref[o, ns:2 * ns] = n_im
                    return n_re, n_im

                h = lax.fori_loop(0, tc, step, h, unroll=8)
                yc_ref[...] = _mm(hs_ref[...], cm_ref[d, 0])
                for s in range(tc):
                    r = pl.multiple_of(tile_of(unit, c, s) * SUBLANES, SUBLANES)
                    sl = slice(s * SUBLANES, (s + 1) * SUBLANES)
                    if d == 0:
                        y_ref[pl.ds(r, SUBLANES), :] = dsk_ref[...] * lhs_ref[sl, :] + yc_ref[sl, :]
                    else:
                        y_ref[pl.ds(r, SUBLANES), :] = y_ref[pl.ds(r, SUBLANES), :] + yc_ref[sl, :]
                return h

            if unit[3]:
                h_init = (h0_ref[d, 0, :, 0:ns], h0_ref[d, 0, :, ns:2 * ns])
            else:
                h_init = (jnp.zeros((SUBLANES, ns), F32), jnp.zeros((SUBLANES, ns), F32))
            f_re, f_im = lax.fori_loop(0, nch, chunk_body, h_init)
            fin_ref[ui, d, 0, :, 0:ns] = f_re
            fin_ref[ui, d, 0, :, ns:2 * ns] = f_im


def _s5(u, dskip, a_re, a_im, bmat, cmat, h0, *, units, tc, grid_rows):
    t, width = u.shape
    nb = width // LANES
    ns = a_re.shape[-1]
    n_units = len(units)
    kern = functools.partial(_s5_kernel, units=units, tc=tc, grid_rows=grid_rows)
    blk4 = lambda *shape: pl.BlockSpec((2, 1) + shape, lambda j: (0, j, 0, 0))
    return pl.pallas_call(
        kern,
        grid=(nb,),
        in_specs=[
            pl.BlockSpec((t, LANES), lambda j: (0, j)),
            pl.BlockSpec((1, LANES), lambda j: (0, j)),
            blk4(1, ns), blk4(1, ns),
            blk4(LANES, 2 * ns), blk4(2 * ns, LANES),
            blk4(SUBLANES, 2 * ns),
        ],
        out_specs=[
            pl.BlockSpec((t, LANES), lambda j: (0, j)),
            pl.BlockSpec((n_units, 2, 1, SUBLANES, 2 * ns), lambda j: (0, 0, j, 0, 0)),
        ],
        out_shape=[
            jax.ShapeDtypeStruct((t, width), F32),
            jax.ShapeDtypeStruct((n_units, 2, nb, SUBLANES, 2 * ns), F32),
        ],
        scratch_shapes=[
            pltpu.VMEM((tc * SUBLANES, LANES), F32),
            pltpu.VMEM((tc * SUBLANES, 2 * ns), F32),
            pltpu.VMEM((tc * SUBLANES, LANES), F32),
        ],
        compiler_params=_params(1),
        name="s5",
    )(u, dskip, a_re, a_im, bmat, cmat, h0)


def _merge_kernel(yrg_ref, zc_ref, ys5_ref, g0_ref, g1_ref, g2_ref, wrg_ref, wcv_ref, wsv_ref,
                  wsg_ref, wo_ref, x_ref, gate_ref, o_ref, acc_ref, rgb_ref, s5b_ref):
    j = pl.program_id(1)

    @pl.when(j == 0)
    def _():
        acc_ref[...] = jnp.zeros_like(acc_ref)
        rgb_ref[...] = yrg_ref[...].astype(rgb_ref.dtype)
        s5b_ref[...] = ys5_ref[...].astype(s5b_ref.dtype)

    dot = lambda a, b: jnp.dot(a, b, preferred_element_type=F32)
    br_rg = dot(rgb_ref[...], wrg_ref[...])
    br_cv = dot(zc_ref[...], wcv_ref[...])
    s5b = s5b_ref[...]
    br_s5 = dot(s5b, wsv_ref[...]) * _sigmoid(dot(s5b, wsg_ref[...]))
    merged = (g0_ref[...].astype(F32) * br_rg + g1_ref[...].astype(F32) * br_cv
              + g2_ref[...].astype(F32) * br_s5)
    acc_ref[...] += dot(merged.astype(MXU_DTYPE), wo_ref[...])

    @pl.when(j == pl.num_programs(1) - 1)
    def _():
        o_ref[...] = _gated_residual(x_ref[...], gate_ref[0], acc_ref[...])


def _merge(y_rg, zc, y_s5, gates, rg_out, cv_out, s5_glu, w_out, x, gate, *, n_sample_rows, tm, tk):
    t, d = x.shape
    rg, cv, s5 = y_rg.shape[1], zc.shape[1], y_s5.shape[1]
    nk = d // tk
    ns = n_sample_rows // tm
    row = lambda w: pl.BlockSpec((tm, w), lambda i, j: (i, 0))
    gspec = lambda k: pl.BlockSpec((tm, tk), lambda i, j: (i, k * nk + j))
    wcol = lambda rows, off: pl.BlockSpec((rows, tk), lambda i, j: (0, off + j))
    return pl.pallas_call(
        _merge_kernel,
        grid=(t // tm, nk),
        in_specs=[
            row(rg), row(cv), row(s5),
            gspec(0), gspec(1), gspec(2),
            wcol(rg, 0), wcol(cv, 0), wcol(s5, 0), wcol(s5, nk),
            pl.BlockSpec((tk, d), lambda i, j: (j, 0)),
            row(d),
            pl.BlockSpec((1, SUBLANES, d), lambda i, j: ((i >= ns).astype(jnp.int32), 0, 0)),
        ],
        out_specs=row(d),
        out_shape=jax.ShapeDtypeStruct((t, d), F32),
        scratch_shapes=[
            pltpu.VMEM((tm, d), F32),
            pltpu.VMEM((tm, rg), MXU_DTYPE),
            pltpu.VMEM((tm, s5), MXU_DTYPE),
        ],
        compiler_params=_params(2),
        name="merge",
    )(y_rg, zc, y_s5, gates, gates, gates, rg_out, cv_out, s5_glu, s5_glu, w_out, x, gate)


def _pack_bf16_pairs(v):
    half = v.shape[1] // 2
    bits = lambda a: lax.bitcast_convert_type(a.astype(jnp.bfloat16).astype(F32), jnp.uint32)
    return (bits(v[:, :half]) >> 16) | (bits(v[:, half:]) & jnp.uint32(0xFFFF0000))


def _store_token_slabs(ref, packed):
    n, width = packed.shape
    r = width // LANES
    for s in range(r):
        ref[pl.ds(s, n, stride=r), :] = packed[:, s * LANES:(s + 1) * LANES]


def _load_token_slabs(ref, first_row, n, r):
    return jnp.concatenate(
        [ref[pl.ds(first_row + s, n, stride=r), :] for s in range(r)], axis=1)


def _unpack_bf16_pairs(p):
    lo = lax.bitcast_convert_type(p << 16, F32)
    hi = lax.bitcast_convert_type(p & jnp.uint32(0xFFFF0000), F32)
    return lo, hi


def _router_kernel(x_ref, g_ref, sc_ref, sh_ref, rw_ref, rb_ref,
                   hnp_ref, eidx_ref, rank_ref, wtok_ref, cnt_ref, run_ref):
    @pl.when(pl.program_id(0) == 0)
    def _():
        run_ref[...] = jnp.zeros_like(run_ref)

    hn = _norm_mod(x_ref[...], g_ref[...], sc_ref[0], sh_ref[0])
    _store_token_slabs(hnp_ref, _pack_bf16_pairs(hn))
    logits = lax.dot_general(rw_ref[...], hn, (((1,), (1,)), ((), ())),
                             precision=lax.Precision.HIGHEST, preferred_element_type=F32)
    s = _sigmoid(logits)
    choice = s + rb_ref[...]
    n_exp, tm = choice.shape
    gsize = n_exp // N_ROUTE_GROUPS
    neg_inf = jnp.float32(-jnp.inf)
    c3 = choice.reshape(N_ROUTE_GROUPS, gsize, tm)
    sub = lax.broadcasted_iota(jnp.int32, c3.shape, 1)
    m1 = jnp.max(c3, axis=1, keepdims=True)
    i1 = jnp.min(jnp.where(c3 == m1, sub, gsize), axis=1, keepdims=True)
    m2 = jnp.max(jnp.where(sub == i1, neg_inf, c3), axis=1, keepdims=True)
    gscore = jnp.broadcast_to(m1 + m2, c3.shape)
    gidx = lax.broadcasted_iota(jnp.int32, c3.shape, 0)
    beaten = jnp.zeros(c3.shape, jnp.int32)
    for gp in range(N_ROUTE_GROUPS):
        other = gscore[gp:gp + 1]
        wins = (other > gscore) | ((other == gscore) & (gidx > gp))
        beaten = beaten + wins.astype(jnp.int32)
    masked = jnp.where(beaten < TOPK_GROUPS, c3, neg_inf).reshape(n_exp, tm)
    eidx = lax.broadcasted_iota(jnp.int32, masked.shape, 0)
    beaten = jnp.zeros(masked.shape, jnp.int32)
    for ep in range(n_exp):
        other = masked[ep:ep + 1, :]
        wins = (other > masked) | ((other == masked) & (eidx > ep))
        beaten = beaten + wins.astype(jnp.int32)
    sel = beaten < TOP_K
    w = jnp.where(sel, s, 0.0)
    comb = ROUTE_SCALE * w / jnp.sum(w, axis=0, keepdims=True)
    self32 = sel.astype(F32)
    before = (lax.broadcasted_iota(jnp.int32, (tm, tm), 0) < lax.broadcasted_iota(jnp.int32, (tm, tm), 1))
    rank = run_ref[...] + _mm(self32, before.astype(F32))
    run_ref[...] = run_ref[...] + jnp.sum(self32, axis=1, keepdims=True)
    cnt_ref[...] = jnp.broadcast_to(run_ref[...], cnt_ref.shape)
    eidx_f = eidx.astype(F32)
    w_rows = []
    for j in range(TOP_K):
        m = beaten == j
        pick = lambda v: jnp.sum(jnp.where(m, v, 0.0), axis=0, keepdims=True)
        eidx_ref[j:j + 1, :] = pick(eidx_f).astype(jnp.int32)
        rank_ref[j:j + 1, :] = pick(rank).astype(jnp.int32)
        w_rows.append(pick(comb))
    w_rows.append(jnp.zeros((wtok_ref.shape[1] - TOP_K, tm), F32))
    wtok_ref[...] = jnp.concatenate(w_rows, axis=0).T


def _router(x, g, scale, shift, router_wt, router_b, *, n_sample_rows, tm):
    t, d = x.shape
    n_exp = router_wt.shape[0]
    ns = n_sample_rows // tm
    unit = lambda i: ((i >= ns).astype(jnp.int32), 0, 0)
    rpt = d // 2 // LANES
    return pl.pallas_call(
        _router_kernel,
        grid=(t // tm,),
        in_specs=[
            pl.BlockSpec((tm, d), lambda i: (i, 0)),
            pl.BlockSpec((1, d), lambda i: (0, 0)),
            pl.BlockSpec((1, SUBLANES, d), unit),
            pl.BlockSpec((1, SUBLANES, d), unit),
            pl.BlockSpec((n_exp, d), lambda i: (0, 0)),
            pl.BlockSpec((n_exp, 1), lambda i: (0, 0)),
        ],
        out_specs=[
            pl.BlockSpec((tm * rpt, LANES), lambda i: (i, 0)),
            pl.BlockSpec((TOP_K, tm), lambda i: (0, i)),
            pl.BlockSpec((TOP_K, tm), lambda i: (0, i)),
            pl.BlockSpec((tm, LANES), lambda i: (i, 0)),
            pl.BlockSpec((n_exp, LANES), lambda i: (0, 0)),
        ],
        out_shape=[
            jax.ShapeDtypeStruct((t * rpt, LANES), jnp.uint32),
            jax.ShapeDtypeStruct((TOP_K, t), jnp.int32),
            jax.ShapeDtypeStruct((TOP_K, t), jnp.int32),
            jax.ShapeDtypeStruct((t, LANES), F32),
            jax.ShapeDtypeStruct((n_exp, LANES), F32),
        ],
        scratch_shapes=[pltpu.VMEM((n_exp, 1), F32)],
        compiler_params=_params(1),
        name="router",
    )(x, g, scale, shift, router_wt, router_b)


def _swiglu(lo, hi, w1, w3, w2):
    half = lo.shape[1]
    lo = lo.astype(MXU_DTYPE)
    hi = hi.astype(MXU_DTYPE)
    dot = lambda a, b: jnp.dot(a, b, preferred_element_type=F32)
    h1 = dot(lo, w1[0:half, :]) + dot(hi, w1[half:, :])
    h3 = dot(lo, w3[0:half, :]) + dot(hi, w3[half:, :])
    return dot((_silu(h1) * h3).astype(MXU_DTYPE), w2[...])


def _dispatch_kernel(pos_ref, hn_ref, xs_in_hbm, xs_hbm, sem):
    del xs_in_hbm
    n_slots, tm = pos_ref.shape
    rpt = hn_ref.shape[0] // tm

    def slab(k):
        return pl.ds(pl.multiple_of(k * rpt, rpt), rpt)

    def issue(t, carry):
        src = hn_ref.at[slab(t)]
        for j in range(n_slots):
            pltpu.make_async_copy(src, xs_hbm.at[slab(pos_ref[j, t])], sem).start(priority=j % 2)
        return carry

    lax.fori_loop(0, tm, issue, 0, unroll=4)
    for j in range(n_slots):
        pltpu.make_async_copy(hn_ref, xs_hbm.at[pl.ds(0, tm * rpt)], sem).wait()


def _dispatch(pos, hnp, xs_init, *, tm):
    n_slots, t = pos.shape
    rpt = hnp.shape[0] // t
    return pl.pallas_call(
        _dispatch_kernel,
        grid=(t // tm,),
        in_specs=[
            pl.BlockSpec((n_slots, tm), lambda i: (0, i), memory_space=pltpu.SMEM),
            pl.BlockSpec((tm * rpt, LANES), lambda i: (i, 0)),
            pl.BlockSpec(memory_space=pl.ANY),
        ],
        out_specs=pl.BlockSpec(memory_space=pl.ANY),
        out_shape=jax.ShapeDtypeStruct(xs_init.shape, xs_init.dtype),
        scratch_shapes=[pltpu.SemaphoreType.DMA],
        input_output_aliases={2: 0},
        compiler_params=_params(1),
        name="dispatch",
    )(pos, hnp, xs_init)


def _ffn_kernel(te_ref, nu_ref, xs_ref, w1_ref, w3_ref, w2_ref, ys_ref, w1b_ref, w3b_ref, w2b_ref):
    i = pl.program_id(0)

    @pl.when((i == 0) | (te_ref[i] != te_ref[jnp.maximum(i - 1, 0)]))
    def _():
        w1b_ref[...] = w1_ref[0, 0].astype(w1b_ref.dtype)
        w3b_ref[...] = w3_ref[0, 0].astype(w3b_ref.dtype)
        w2b_ref[...] = w2_ref[0, 0].astype(w2b_ref.dtype)

    @pl.when(i < nu_ref[0])
    def _():
        rpt = w1b_ref.shape[0] // 2 // LANES
        lo, hi = _unpack_bf16_pairs(_load_token_slabs(xs_ref, 0, xs_ref.shape[0] // rpt, rpt))
        _store_token_slabs(ys_ref, _pack_bf16_pairs(_swiglu(lo, hi, w1b_ref, w3b_ref, w2b_ref)))

    @pl.when(i >= nu_ref[0])
    def _():
        ys_ref[...] = jnp.zeros_like(ys_ref)


def _ffn(tile_expert, n_used, xs, w1, w3, w2, *, layer, tg):
    _, _, d, f = w1.shape
    rpt = d // 2 // LANES
    n_tiles = xs.shape[0] // (tg * rpt)
    row = lambda i, te, nu: (jnp.maximum(jnp.minimum(i, nu[0] - 1), 0), 0)
    wblk = lambda i, te, nu: (layer, te[i], 0, 0)
    grid_spec = pltpu.PrefetchScalarGridSpec(
        num_scalar_prefetch=2,
        grid=(n_tiles,),
        in_specs=[
            pl.BlockSpec((tg * rpt, LANES), row),
            pl.BlockSpec((1, 1, d, f), wblk),
            pl.BlockSpec((1, 1, d, f), wblk),
            pl.BlockSpec((1, 1, f, d), wblk),
        ],
        out_specs=pl.BlockSpec((tg * rpt, LANES), lambda i, te, nu: (i, 0)),
        scratch_shapes=[
            pltpu.VMEM((d, f), MXU_DTYPE),
            pltpu.VMEM((d, f), MXU_DTYPE),
            pltpu.VMEM((f, d), MXU_DTYPE),
        ],
    )
    return pl.pallas_call(
        _ffn_kernel,
        grid_spec=grid_spec,
        out_shape=jax.ShapeDtypeStruct(xs.shape, jnp.uint32),
        compiler_params=_params(1),
        name="ffn",
    )(tile_expert, n_used, xs, w1, w3, w2)


def _combine_kernel(pos_ref, hnp_ref, wtok_ref, s1_ref, s3_ref, s2_ref, x_ref, gate_ref, ys_hbm,
                    o_ref, buf_ref, sem):
    n_slots, tm = pos_ref.shape
    rpt = hnp_ref.shape[0] // tm
    slab_rows = tm * rpt

    def slab(k):
        return pl.ds(pl.multiple_of(k * rpt, rpt), rpt)

    def issue(t, carry):
        for j in range(n_slots):
            pltpu.make_async_copy(ys_hbm.at[slab(pos_ref[j, t])],
                                  buf_ref.at[slab(j * tm + t)], sem).start(priority=j % 2)
        return carry

    lax.fori_loop(0, tm, issue, 0, unroll=4)
    lo, hi = _unpack_bf16_pairs(_load_token_slabs(hnp_ref, 0, tm, rpt))
    moe = _swiglu(lo, hi, s1_ref, s3_ref, s2_ref)
    half = lo.shape[1]
    acc_lo = moe[:, :half]
    acc_hi = moe[:, half:]
    wtok = wtok_ref[...]
    for j in range(n_slots):
        pltpu.make_async_copy(ys_hbm.at[pl.ds(0, slab_rows)],
                              buf_ref.at[pl.ds(j * slab_rows, slab_rows)], sem).wait()
    for j in range(n_slots):
        y_lo, y_hi = _unpack_bf16_pairs(_load_token_slabs(buf_ref, j * slab_rows, tm, rpt))
        wj = wtok[:, j:j + 1]
        acc_lo = acc_lo + wj * y_lo
        acc_hi = acc_hi + wj * y_hi
    x = x_ref[...]
    gate = gate_ref[0]
    o_ref[:, :half] = _gated_residual(x[:, :half], gate[:, :half], acc_lo)
    o_ref[:, half:] = _gated_residual(x[:, half:], gate[:, half:], acc_hi)


def _combine(pos, hnp, wtok, s1, s3, s2, x, gate, ys, *, n_sample_rows, tm):
    t, d = x.shape
    n_slots = pos.shape[0]
    f = s1.shape[1]
    ns = n_sample_rows // tm
    rpt = d // 2 // LANES
    full = lambda shape: pl.BlockSpec(shape, lambda i: (0,) * len(shape))
    return pl.pallas_call(
        _combine_kernel,
        grid=(t // tm,),
        in_specs=[
            pl.BlockSpec((n_slots, tm), lambda i: (0, i), memory_space=pltpu.SMEM),
            pl.BlockSpec((tm * rpt, LANES), lambda i: (i, 0)),
            pl.BlockSpec((tm, LANES), lambda i: (i, 0)),
            full((d, f)), full((d, f)), full((f, d)),
            pl.BlockSpec((tm, d), lambda i: (i, 0)),
            pl.BlockSpec((1, SUBLANES, d), lambda i: ((i >= ns).astype(jnp.int32), 0, 0)),
            pl.BlockSpec(memory_space=pl.ANY),
        ],
        out_specs=pl.BlockSpec((tm, d), lambda i: (i, 0)),
        out_shape=jax.ShapeDtypeStruct((t, d), F32),
        scratch_shapes=[
            pltpu.VMEM((n_slots * tm * rpt, LANES), jnp.uint32),
            pltpu.SemaphoreType.DMA,
        ],
        compiler_params=_params(1),
        name="combine",
    )(pos, hnp, wtok, s1, s3, s2, x, gate, ys)


def _slots_kernel(cnt_ref, eidx_ref, rank_ref, pos_ref, te_ref, nu_ref, *, tg):
    counts = cnt_ref[...]
    n_exp = counts.shape[0]
    padded = jnp.floor((counts + (tg - 1)) * (1.0 / tg)) * tg
    incl = (lax.broadcasted_iota(jnp.int32, (n_exp, n_exp), 0)
            >= lax.broadcasted_iota(jnp.int32, (n_exp, n_exp), 1)).astype(F32)
    ends = jnp.dot(incl, padded, precision=lax.Precision.HIGHEST, preferred_element_type=F32)
    ends_col = ends[:, 0:1]
    offs_col = ends_col - padded[:, 0:1]
    eidx = eidx_ref[...]
    rank = rank_ref[...]
    expert = lax.broadcasted_iota(jnp.int32, (n_exp, eidx.shape[1]), 0)
    for j in range(eidx.shape[0]):
        off = jnp.sum(jnp.where(expert == eidx[j:j + 1, :], offs_col, 0.0), axis=0, keepdims=True)
        pos_ref[j:j + 1, :] = off.astype(jnp.int32) + rank[j:j + 1, :]
    n_used = ends[n_exp - 1:n_exp, 0:1] * (1.0 / tg)
    tile = lax.broadcasted_iota(jnp.int32, te_ref.shape, 1).astype(F32)
    tile_row = jnp.minimum(tile, n_used - 1.0) * tg
    te_ref[...] = jnp.sum((ends_col <= tile_row).astype(F32), axis=0, keepdims=True).astype(jnp.int32)
    nu_ref[...] = jnp.broadcast_to(n_used, nu_ref.shape).astype(jnp.int32)


def _slots(counts, eidx, rank, *, tg, n_tiles, tb):
    assert tg & (tg - 1) == 0
    n_slots, t = eidx.shape
    n_tiles_pad = pl.cdiv(n_tiles, LANES) * LANES
    pos, te, nu = pl.pallas_call(
        functools.partial(_slots_kernel, tg=tg),
        grid=(t // tb,),
        in_specs=[
            pl.BlockSpec(counts.shape, lambda i: (0, 0)),
            pl.BlockSpec((n_slots, tb), lambda i: (0, i)),
            pl.BlockSpec((n_slots, tb), lambda i: (0, i)),
        ],
        out_specs=[
            pl.BlockSpec((n_slots, tb), lambda i: (0, i)),
            pl.BlockSpec((1, n_tiles_pad), lambda i: (0, 0)),
            pl.BlockSpec((1, LANES), lambda i: (0, 0)),
        ],
        out_shape=[
            jax.ShapeDtypeStruct((n_slots, t), jnp.int32),
            jax.ShapeDtypeStruct((1, n_tiles_pad), jnp.int32),
            jax.ShapeDtypeStruct((1, LANES), jnp.int32),
        ],
        compiler_params=_params(1),
        name="slots",
    )(counts, eidx, rank)
    return pos, te[0, :n_tiles], nu[0, :1]


def _final_norm_kernel(x_ref, g_ref, os_ref, oc_ref, y_ref, *, n_sample_tiles):
    x = x_ref[...]
    ms = jnp.mean(x * x, axis=-1, keepdims=True)
    y = x * lax.rsqrt(ms + EPS) * g_ref[...]
    steps = x.shape[0] // SUBLANES
    n_lane_blocks = x.shape[1] // LANES
    for k in range(n_lane_blocks):
        y_ref[k] = y[:, k * LANES:(k + 1) * LANES]

    def emit(o_ref):
        for b in range(SUBLANES):
            for k in range(n_lane_blocks):
                o_ref[b, :, k * LANES:(k + 1) * LANES] = y_ref[k, pl.ds(b, steps, stride=SUBLANES), :]

    @pl.when(pl.program_id(0) < n_sample_tiles)
    def _():
        emit(os_ref)

    @pl.when(pl.program_id(0) >= n_sample_tiles)
    def _():
        emit(oc_ref)


def _final_norm(x, g, *, ls, lc, n_ctx_units, tm):
    t, d = x.shape
    steps = tm // SUBLANES
    nst = ls // steps
    ctx_tiles = lc // steps
    n_ctx = n_ctx_units * ctx_tiles

    def ctx_block(i):
        k = jnp.clip(i - nst, 0, n_ctx - 1)
        return (k // ctx_tiles, k % ctx_tiles, 0)

    return pl.pallas_call(
        functools.partial(_final_norm_kernel, n_sample_tiles=nst),
        grid=(t // tm,),
        in_specs=[pl.BlockSpec((tm, d), lambda i: (i, 0)), pl.BlockSpec((1, d), lambda i: (0, 0))],
        out_specs=[
            pl.BlockSpec((SUBLANES, steps, d), lambda i: (0, jnp.minimum(i, nst - 1), 0)),
            pl.BlockSpec((SUBLANES, steps, d), ctx_block),
        ],
        out_shape=[
            jax.ShapeDtypeStruct((SUBLANES, ls, d), F32),
            jax.ShapeDtypeStruct((n_ctx_units * SUBLANES, lc, d), F32),
        ],
        scratch_shapes=[pltpu.VMEM((d // LANES, tm, LANES), F32)],
        compiler_params=_params(1),
        name="final_norm",
    )(x, g)


def _s5_discretise(lam_re, lam_im, log_dt, b_re, b_im, c_re, c_im):
    two, g, p = lam_re.shape
    s = b_re.shape[-1]
    gb = LANES // s
    nb = g // gb
    dt = jnp.exp(log_dt)[..., None]
    mag = jnp.exp(lam_re * dt)
    a_re = mag * jnp.cos(lam_im * dt)
    a_im = mag * jnp.sin(lam_im * dt)
    den = lam_re * lam_re + lam_im * lam_im
    q_re = ((a_re - 1.0) * lam_re + a_im * lam_im) / den
    q_im = (a_im * lam_re - (a_re - 1.0) * lam_im) / den
    bb_re = q_re[..., None] * b_re - q_im[..., None] * b_im
    bb_im = q_re[..., None] * b_im + q_im[..., None] * b_re
    eye = jnp.eye(gb, dtype=F32)

    def bdiag_b(m):
        m = m.reshape(two, nb, gb, p, s)
        return jnp.einsum("dbgps,gh->dbgshp", m, eye).reshape(two, nb, gb * s, gb * p)

    def bdiag_c(m):
        m = m.reshape(two, nb, gb, s, p)
        return jnp.einsum("dbgsp,gh->dbgphs", m, eye).reshape(two, nb, gb * p, gb * s)

    bmat = jnp.concatenate([bdiag_b(bb_re), bdiag_b(bb_im)], axis=-1)
    cmat = jnp.concatenate([bdiag_c(c_re), -bdiag_c(c_im)], axis=-2)
    blk = lambda a: a.reshape(two, nb, 1, gb * p)
    return blk(a_re), blk(a_im), bmat.astype(MXU_DTYPE), cmat.astype(MXU_DTYPE)


def kernel(x_prompt, x_sample, state_rglru, state_s5, c, c_ctx, norm_mix, norm_ffn, ada_w, ada_b, w_in, b_gate, rg_conv_w, rg_conv_b, rg_wa, rg_ba, rg_wx, rg_bx, rg_lambda, rg_out, cv_dw, cv_db, cv_ln_g, cv_ln_b, cv_out, s5_lambda_re, s5_lambda_im, s5_log_dt, s5_b_re, s5_b_im, s5_c_re, s5_c_im, s5_d, s5_glu, w_out, router_w, router_b, exp_w1, exp_w3, exp_w2, sh_w1, sh_w3, sh_w2, norm_final):
    batch, lc, d = x_prompt.shape
    dec_batch, ls, _ = x_sample.shape
    depth = w_in.shape[0]
    rg = rg_out.shape[1]
    cv = cv_out.shape[1]
    s5 = s5_glu.shape[1]
    n_groups_s5, n_state = s5_lambda_re.shape[2], s5_lambda_re.shape[3]
    assert dec_batch == SUBLANES and batch % SUBLANES == 0
    assert ls % GRID_W == 0 and lc % CHUNK_TILES == 0
    assert rg_wa.shape[-1] == LANES
    n_ctx_units = batch // SUBLANES
    grid_rows = ls // GRID_W
    n_sample_rows = ls * SUBLANES
    units = ((0, ls, True, True),) + tuple(
        (ls + k * lc, lc, False, False) for k in range(n_ctx_units))
    tm = 512
    tn = min(1024, rg, cv, s5)
    tk = min(512, d)
    tc = min(128, lc)
    tg = 512
    tmc = 256
    n_tokens = (ls + n_ctx_units * lc) * SUBLANES
    n_exp = router_w.shape[-1]
    assert tc % grid_rows == 0 and (GRID_W * grid_rows) % tc == 0
    assert n_sample_rows % tm == 0 and (lc * SUBLANES) % tm == 0 and n_sample_rows % tmc == 0
    assert all(w % tn == 0 for w in (rg, cv, s5, b_gate.shape[-1])) and d % tk == 0
    n_ffn_tiles = pl.cdiv(n_tokens * TOP_K, tg) + n_exp

    xs = jnp.transpose(x_sample, (1, 0, 2)).reshape(ls * SUBLANES, d)
    xc = jnp.transpose(x_prompt.reshape(n_ctx_units, SUBLANES, lc, d), (0, 2, 1, 3))
    x = jnp.concatenate([xs, xc.reshape(n_ctx_units * lc * SUBLANES, d)], axis=0)

    cond = jnp.concatenate([c, c_ctx[None], jnp.zeros((SUBLANES - 1, d), F32)], axis=0)
    mod = _ada(cond, ada_w, ada_b)
    mod = jnp.stack([mod[:, :SUBLANES],
                     jnp.broadcast_to(mod[:, SUBLANES:SUBLANES + 1], (depth, SUBLANES, 6 * d))], axis=1)
    mod = mod.reshape(depth, 2, SUBLANES, 6, d)

    cast = lambda a: a.astype(MXU_DTYPE)
    sorted_rows = jnp.zeros((n_ffn_tiles * tg * (d // 2 // LANES), LANES), jnp.uint32)
    rg_fin, s5_fin = [], []
    for l in range(depth):
        shift1, scale1, gate1, shift2, scale2, gate2 = (mod[l, :, :, k] for k in range(6))
        u_rg, z_cv, u_s5, gates = _win(
            x, norm_mix[l][None], scale1, shift1, cast(w_in[l]), b_gate[l][None],
            rg=rg, cv=cv, s5=s5, n_sample_rows=n_sample_rows, tm=tm, tn=tn)

        h0_rg = jnp.transpose(state_rglru[:, l], (1, 0, 2))
        heads = rg // LANES
        y_rg, fin_rg = _rglru(
            u_rg, rg_conv_w[l], rg_conv_b[l][None],
            cast(rg_wa[l]), cast(rg_wx[l]), rg_ba[l], rg_bx[l], rg_lambda[l], h0_rg, units=units)
        rg_fin.append(fin_rg)

        dw_tiles = jnp.broadcast_to(cv_dw[l][:, None, :], (cv_dw.shape[1], SUBLANES, cv))
        zc = _conv_branch(z_cv, dw_tiles, cv_db[l][None], cv_ln_g[l][None], cv_ln_b[l][None],
                          n_sample_chunks=ls // CHUNK_TILES, ctx_unit_chunks=lc // CHUNK_TILES)

        a_re, a_im, bmat, cmat = _s5_discretise(
            s5_lambda_re[l], s5_lambda_im[l], s5_log_dt[l], s5_b_re[l], s5_b_im[l],
            s5_c_re[l], s5_c_im[l])
        nb = s5 // LANES
        st = jnp.transpose(state_s5[:, l], (1, 4, 0, 2, 3))
        st = st.reshape(2, 2, SUBLANES, nb, (n_groups_s5 // nb) * n_state)
        h0_s5 = jnp.transpose(st, (0, 3, 2, 1, 4)).reshape(2, nb, SUBLANES, -1)
        y_s5, fin_s5 = _s5(u_s5, s5_d[l][None], a_re, a_im, bmat, cmat, h0_s5,
                           units=units, tc=tc, grid_rows=grid_rows)
        s5_fin.append(fin_s5)

        x = _merge(y_rg, zc, y_s5, gates, cast(rg_out[l]), cast(cv_out[l]), cast(s5_glu[l]),
                   cast(w_out[l]), x, gate1, n_sample_rows=n_sample_rows, tm=tm, tk=tk)

        hnp, eidx, rank, wtok, counts = _router(
            x, norm_ffn[l][None], scale2, shift2, router_w[l].T, router_b[l][:, None],
            n_sample_rows=n_sample_rows, tm=tm)
        pos, tile_expert, n_used = _slots(counts, eidx, rank, tg=tg, n_tiles=n_ffn_tiles, tb=tm)
        sorted_rows = _dispatch(pos, hnp, sorted_rows, tm=tm)
        ys = _ffn(tile_expert, n_used, sorted_rows, exp_w1, exp_w3, exp_w2, layer=l, tg=tg)
        x = _combine(pos, hnp, wtok, cast(sh_w1[l]), cast(sh_w3[l]), cast(sh_w2[l]), x, gate2, ys,
                     n_sample_rows=n_sample_rows, tm=tmc)

    y_sample, y_prompt = _final_norm(x, norm_final[None], ls=ls, lc=lc, n_ctx_units=n_ctx_units, tm=tm)

    fr = jnp.stack(rg_fin, axis=0)[:, 1:]
    new_state_rglru = jnp.transpose(fr, (1, 3, 0, 2, 4)).reshape(batch, depth, 2, rg)
    fs = jnp.stack(s5_fin, axis=0)[:, 1:]
    nb = s5 // LANES
    fs = fs.reshape(depth, n_ctx_units, 2, nb, SUBLANES, 2, n_groups_s5 // nb, n_state)
    new_state_s5 = jnp.transpose(fs, (1, 4, 0, 2, 3, 6, 7, 5)).reshape(
        batch, depth, 2, n_groups_s5, n_state, 2)
    return y_prompt, y_sample, new_state_rglru, new_state_s5
```

```python
import functools
import math

import jax
import jax.numpy as jnp
from jax import lax
from jax.experimental import pallas as pl
from jax.experimental.pallas import tpu as pltpu

F32 = jnp.float32
MXU_DTYPE = jnp.bfloat16

SUBLANES = 8
LANES = 128
VMEM_LIMIT_BYTES = 56 * 1024 * 1024

GRID_W = 64
RG_C = 8.0
S5_GROUP = 16
TOP_K = 8
N_ROUTE_GROUPS = 8
TOPK_GROUPS = 4
ROUTE_SCALE = 2.5
EPS = 1e-6

CHUNK_TILES = 64
CONV_HALO_TILES = 16


def _params(n_grid_dims):
    return pltpu.CompilerParams(
        dimension_semantics=("arbitrary",) * n_grid_dims,
        vmem_limit_bytes=VMEM_LIMIT_BYTES,
    )


def _mm(a, b):
    return jnp.dot(a.astype(MXU_DTYPE), b.astype(MXU_DTYPE), preferred_element_type=F32)


def _sigmoid(x):
    return 0.5 * jnp.tanh(0.5 * x) + 0.5


def _silu(x):
    return x * _sigmoid(x)


def _tile_bcast_mul_add(y, scale, shift):
    rows, d = y.shape
    y3 = y.reshape(rows // SUBLANES, SUBLANES, d)
    return (y3 * scale[None] + shift[None]).reshape(rows, d)


def _gated_residual(x, gate, y):
    rows, d = y.shape
    return (y.reshape(rows // SUBLANES, SUBLANES, d) * gate[None]).reshape(rows, d) + x


def _norm_mod(x, g, scale, shift):
    ms = jnp.mean(x * x, axis=-1, keepdims=True)
    y = x * lax.rsqrt(ms + EPS) * g
    return _tile_bcast_mul_add(y, 1.0 + scale, shift)


def _ada_kernel(c_ref, w_ref, b_ref, o_ref):
    o_ref[0] = _mm(_silu(c_ref[...]), w_ref[0]) + b_ref[0]


def _ada(cond, ada_w, ada_b):
    depth, d, n = ada_w.shape
    tn = math.gcd(n, 1024)
    rows = cond.shape[0]
    return pl.pallas_call(
        _ada_kernel,
        grid=(depth, n // tn),
        in_specs=[
            pl.BlockSpec((rows, d), lambda l, j: (0, 0)),
            pl.BlockSpec((1, d, tn), lambda l, j: (l, 0, j)),
            pl.BlockSpec((1, 1, tn), lambda l, j: (l, 0, j)),
        ],
        out_specs=pl.BlockSpec((1, rows, tn), lambda l, j: (l, 0, j)),
        out_shape=jax.ShapeDtypeStruct((depth, rows, n), F32),
        compiler_params=_params(2),
        name="ada",
    )(cond, ada_w, ada_b.reshape(depth, 1, n))


def _win_kernel(x_ref, g_ref, sc_ref, sh_ref, wa_ref, wb_ref, bg_ref,
                urg_ref, z_ref, us5_ref, gt_ref, hn_ref, *, n_rg, n_cv, n_s5):
    j = pl.program_id(1)

    @pl.when(j == 0)
    def _():
        hn_ref[...] = _norm_mod(x_ref[...], g_ref[...], sc_ref[0], sh_ref[0]).astype(hn_ref.dtype)

    hn = hn_ref[...]
    a = jnp.dot(hn, wa_ref[...], preferred_element_type=F32)

    @pl.when(j < n_rg)
    def _():
        urg_ref[...] = a

    @pl.when((j >= n_rg) & (j < n_rg + n_cv))
    def _():
        b = jnp.dot(hn, wb_ref[...], preferred_element_type=F32)
        z_ref[...] = a * _sigmoid(b)

    @pl.when((j >= n_rg + n_cv) & (j < n_rg + n_cv + n_s5))
    def _():
        us5_ref[...] = a

    @pl.when(j >= n_rg + n_cv + n_s5)
    def _():
        gt_ref[...] = _sigmoid(a + bg_ref[...]).astype(gt_ref.dtype)


def _win(x, g, scale, shift, w_in, b_gate, *, rg, cv, s5, n_sample_rows, tm, tn):
    t, d = x.shape
    n_gate = b_gate.shape[-1]
    n_rg, n_cv, n_s5, n_g = rg // tn, cv // tn, s5 // tn, n_gate // tn
    ns = n_sample_rows // tm
    unit = lambda i, j: ((i >= ns).astype(jnp.int32), 0, 0)
    clip = lambda v, n: jnp.clip(v, 0, n - 1)
    kern = functools.partial(_win_kernel, n_rg=n_rg, n_cv=n_cv, n_s5=n_s5)
    return pl.pallas_call(
        kern,
        grid=(t // tm, n_rg + n_cv + n_s5 + n_g),
        in_specs=[
            pl.BlockSpec((tm, d), lambda i, j: (i, 0)),
            pl.BlockSpec((1, d), lambda i, j: (0, 0)),
            pl.BlockSpec((1, SUBLANES, d), unit),
            pl.BlockSpec((1, SUBLANES, d), unit),
            pl.BlockSpec((d, tn), lambda i, j: (0, jnp.where(j < n_rg + n_cv, j, j + n_cv))),
            pl.BlockSpec((d, tn), lambda i, j: (0, n_rg + n_cv + clip(j - n_rg, n_cv))),
            pl.BlockSpec((1, tn), lambda i, j: (0, clip(j - (n_rg + n_cv + n_s5), n_g))),
        ],
        out_specs=[
            pl.BlockSpec((tm, tn), lambda i, j: (i, clip(j, n_rg))),
            pl.BlockSpec((tm, tn), lambda i, j: (i, clip(j - n_rg, n_cv))),
            pl.BlockSpec((tm, tn), lambda i, j: (i, clip(j - n_rg - n_cv, n_s5))),
            pl.BlockSpec((tm, tn), lambda i, j: (i, clip(j - n_rg - n_cv - n_s5, n_g))),
        ],
        out_shape=[
            jax.ShapeDtypeStruct((t, rg), F32),
            jax.ShapeDtypeStruct((t, cv), F32),
            jax.ShapeDtypeStruct((t, s5), F32),
            jax.ShapeDtypeStruct((t, n_gate), MXU_DTYPE),
        ],
        scratch_shapes=[pltpu.VMEM((tm, d), MXU_DTYPE)],
        compiler_params=_params(2),
        name="win",
    )(x, g, scale, shift, w_in, w_in, b_gate)


def _rg_kernel(u_ref, cw_ref, cb_ref, wa_ref, wx_ref, ba_ref, bx_ref, lam_ref, h0_ref,
               y_ref, fin_ref, ext_ref, a_ref, b_ref, *, units, ch, kconv):
    rows = ch * SUBLANES
    pad_lo = (kconv // 2) * SUBLANES
    pad_hi = (kconv - 1 - kconv // 2) * SUBLANES
    t_rows = u_ref.shape[0]
    for d in (0, 1):
        neg_lam = -lam_ref[d:d + 1, :]
        softplus = jnp.maximum(neg_lam, 0.0) + jnp.log1p(jnp.exp(-jnp.abs(neg_lam)))
        coef = -RG_C * softplus
        wa = wa_ref[d, 0]
        wx = wx_ref[d, 0]
        ba = ba_ref[d:d + 1, :]
        bx = bx_ref[d:d + 1, :]
        for ui, (t0, nt, _, has_h0) in enumerate(units):
            nch = nt // ch

            def chunk_body(ci, h, d=d, t0=t0, nch=nch, coef=coef, wa=wa, wx=wx, ba=ba, bx=bx):
                c = ci if d == 0 else nch - 1 - ci
                r0 = pl.multiple_of((t0 + c * ch) * SUBLANES, SUBLANES)
                lo_start = pl.multiple_of(jnp.maximum(r0 - pad_lo, 0), SUBLANES)
                hi_start = pl.multiple_of(jnp.minimum(r0 + rows, t_rows - pad_hi), SUBLANES)
                ext_ref[0:pad_lo, :] = jnp.where(c > 0, u_ref[pl.ds(lo_start, pad_lo), :], 0.0)
                ext_ref[pad_lo:pad_lo + rows, :] = u_ref[pl.ds(r0, rows), :]
                ext_ref[pad_lo + rows:pad_lo + rows + pad_hi, :] = jnp.where(
                    c < nch - 1, u_ref[pl.ds(hi_start, pad_hi), :], 0.0)
                xc = cb_ref[...] + cw_ref[0:1, :] * ext_ref[0:rows, :]
                for k in range(1, kconv):
                    xc = xc + cw_ref[k:k + 1, :] * ext_ref[k * SUBLANES:k * SUBLANES + rows, :]
                r = _sigmoid(_mm(xc, wa) + ba)
                i = _sigmoid(_mm(xc, wx) + bx)
                a = jnp.exp(coef * r)
                a_ref[...] = a
                b_ref[...] = jnp.sqrt(1.0 - a * a) * i * xc

                group = SUBLANES

                def steps(gi, h):
                    g = gi if d == 0 else ch // group - 1 - gi
                    base = pl.multiple_of(g * group * SUBLANES, group * SUBLANES)
                    for k in range(group):
                        o = pl.ds(base + (k if d == 0 else group - 1 - k) * SUBLANES, SUBLANES)
                        h = a_ref[o, :] * h + b_ref[o, :]
                        b_ref[o, :] = h
                    return h

                h = lax.fori_loop(0, ch // group, steps, h)
                if d == 0:
                    y_ref[pl.ds(r0, rows), :] = b_ref[...]
                else:
                    y_ref[pl.ds(r0, rows), :] = y_ref[pl.ds(r0, rows), :] + b_ref[...]
                return h

            h_init = h0_ref[d] if has_h0 else jnp.zeros((SUBLANES, LANES), F32)
            fin_ref[ui, d] = lax.fori_loop(0, nch, chunk_body, h_init)


def _rglru(u, conv_w, conv_b, wa, wx, ba, bx, lam, h0, *, units):
    t, rg = u.shape
    heads = rg // LANES
    kconv = conv_w.shape[0]
    n_units = len(units)
    rows = CHUNK_TILES * SUBLANES
    kern = functools.partial(_rg_kernel, units=units, ch=CHUNK_TILES, kconv=kconv)
    col = lambda h: (0, h)
    return pl.pallas_call(
        kern,
        grid=(heads,),
        in_specs=[
            pl.BlockSpec((t, LANES), col),
            pl.BlockSpec((kconv, LANES), col),
            pl.BlockSpec((1, LANES), col),
            pl.BlockSpec((2, 1, LANES, LANES), lambda h: (0, h, 0, 0)),
            pl.BlockSpec((2, 1, LANES, LANES), lambda h: (0, h, 0, 0)),
            pl.BlockSpec((2, LANES), col),
            pl.BlockSpec((2, LANES), col),
            pl.BlockSpec((2, LANES), col),
            pl.BlockSpec((2, SUBLANES, LANES), lambda h: (0, 0, h)),
        ],
        out_specs=[
            pl.BlockSpec((t, LANES), col),
            pl.BlockSpec((n_units, 2, SUBLANES, LANES), lambda h: (0, 0, 0, h)),
        ],
        out_shape=[
            jax.ShapeDtypeStruct((t, rg), F32),
            jax.ShapeDtypeStruct((n_units, 2, SUBLANES, rg), F32),
        ],
        scratch_shapes=[
            pltpu.VMEM((rows + (kconv - 1) * SUBLANES, LANES), F32),
            pltpu.VMEM((rows, LANES), F32),
            pltpu.VMEM((rows, LANES), F32),
        ],
        compiler_params=_params(1),
        name="rglru",
    )(u, conv_w, conv_b, wa, wx, ba, bx, lam, h0)


def _cv_kernel(zp_ref, zc_ref, zn_ref, w_ref, db_ref, lg_ref, lb_ref, o_ref, ext_ref, acc_ref,
               *, n_sample_chunks, ctx_unit_chunks, kc):
    i = pl.program_id(0)
    rows, c = zc_ref.shape
    hr = CONV_HALO_TILES * SUBLANES
    group = SUBLANES
    is_ctx = i >= n_sample_chunks
    cpos = (i - n_sample_chunks) % ctx_unit_chunks
    lo_ok = is_ctx & (cpos > 0)
    hi_ok = is_ctx & (cpos < ctx_unit_chunks - 1)
    ext_ref[0:hr, :] = jnp.where(lo_ok, zp_ref[rows - hr:rows, :], 0.0)
    ext_ref[hr:hr + rows, :] = zc_ref[...]
    ext_ref[hr + rows:hr + rows + hr, :] = jnp.where(hi_ok, zn_ref[0:hr, :], 0.0)
    first_tap_tile = CONV_HALO_TILES - kc // 2

    def lane_body(lb, carry):
        lanes = pl.ds(pl.multiple_of(lb * LANES, LANES), LANES)
        taps = [w_ref[k, :, lanes] for k in range(kc)]

        def grp_body(g, carry):
            accs = [None] * group
            for e in range(group + kc - 1):
                src = pl.multiple_of((g * group + first_tap_tile + e) * SUBLANES, SUBLANES)
                tile = ext_ref[pl.ds(src, SUBLANES), lanes]
                for t in range(group):
                    k = e - t
                    if 0 <= k < kc:
                        term = taps[k] * tile
                        accs[t] = term if accs[t] is None else accs[t] + term
            for t in range(group):
                dst = pl.multiple_of((g * group + t) * SUBLANES, SUBLANES)
                acc_ref[pl.ds(dst, SUBLANES), lanes] = accs[t]
            return carry

        return lax.fori_loop(0, rows // (group * SUBLANES), grp_body, carry)

    lax.fori_loop(0, c // LANES, lane_body, 0)
    z = acc_ref[...] + db_ref[...]
    mu = jnp.mean(z, axis=-1, keepdims=True)
    zc = z - mu
    var = jnp.mean(zc * zc, axis=-1, keepdims=True)
    y = zc * lax.rsqrt(var + EPS) * lg_ref[...] + lb_ref[...]
    o_ref[...] = _silu(y).astype(o_ref.dtype)


def _conv_branch(z, dw_tiles, db, ln_g, ln_b, *, n_sample_chunks, ctx_unit_chunks):
    t, c = z.shape
    kc = dw_tiles.shape[0]
    rows = CHUNK_TILES * SUBLANES
    n = t // rows
    kern = functools.partial(_cv_kernel, n_sample_chunks=n_sample_chunks,
                             ctx_unit_chunks=ctx_unit_chunks, kc=kc)
    vec = pl.BlockSpec((1, c), lambda i: (0, 0))
    return pl.pallas_call(
        kern,
        grid=(n,),
        in_specs=[
            pl.BlockSpec((rows, c), lambda i: (jnp.maximum(i - 1, 0), 0)),
            pl.BlockSpec((rows, c), lambda i: (i, 0)),
            pl.BlockSpec((rows, c), lambda i: (jnp.minimum(i + 1, n - 1), 0)),
            pl.BlockSpec((kc, SUBLANES, c), lambda i: (0, 0, 0)),
            vec, vec, vec,
        ],
        out_specs=pl.BlockSpec((rows, c), lambda i: (i, 0)),
        out_shape=jax.ShapeDtypeStruct((t, c), MXU_DTYPE),
        scratch_shapes=[
            pltpu.VMEM((rows + 2 * CONV_HALO_TILES * SUBLANES, c), F32),
            pltpu.VMEM((rows, c), F32),
        ],
        compiler_params=_params(1),
        name="conv",
    )(z, z, z, dw_tiles, db, ln_g, ln_b)


def _s5_kernel(u_ref, dsk_ref, are_ref, aim_ref, bm_ref, cm_ref, h0_ref,
               y_ref, fin_ref, lhs_ref, hs_ref, yc_ref, *, units, tc, grid_rows):
    ns = are_ref.shape[-1]

    def tile_of(unit, c, s):
        t0, _, is_grid, _ = unit
        if is_grid:
            return t0 + (s % grid_rows) * GRID_W + c * (tc // grid_rows) + s // grid_rows
        return t0 + c * tc + s

    for d in (0, 1):
        a_re = jnp.broadcast_to(are_ref[d, 0], (SUBLANES, ns))
        a_im = jnp.broadcast_to(aim_ref[d, 0], (SUBLANES, ns))
        for ui, unit in enumerate(units):
            nch = unit[1] // tc

            def chunk_body(ci, h, d=d, unit=unit, nch=nch, a_re=a_re, a_im=a_im):
                c = ci if d == 0 else nch - 1 - ci
                for s in range(tc):
                    r = pl.multiple_of(tile_of(unit, c, s) * SUBLANES, SUBLANES)
                    lhs_ref[s * SUBLANES:(s + 1) * SUBLANES, :] = u_ref[pl.ds(r, SUBLANES), :]
                hs_ref[...] = _mm(lhs_ref[...], bm_ref[d, 0])

                group = SUBLANES

                def steps(gi, h):
                    g = gi if d == 0 else tc // group - 1 - gi
                    base = pl.multiple_of(g * group * SUBLANES, group * SUBLANES)
                    h_re, h_im = h
                    for k in range(group):
                        o = pl.ds(base + (k if d == 0 else group - 1 - k) * SUBLANES, SUBLANES)
                        n_re = a_re * h_re - a_im * h_im + hs_ref[o, 0:ns]
                        n_im = a_re * h_im + a_im * h_re + hs_ref[o, ns:2 * ns]
                        hs_ref[o, 0:ns] = n_re
                        hs_ref[o, ns:2 * ns] = n_im
                        h_re, h_im = n_re, n_im
                    return h_re, h_im

                h = lax.fori_loop(0, tc // group, steps, h)
                yc_ref[...] = _mm(hs_ref[...], cm_ref[d, 0])
                for s in range(tc):
                    r = pl.multiple_of(tile_of(unit, c, s) * SUBLANES, SUBLANES)
                    sl = slice(s * SUBLANES, (s + 1) * SUBLANES)
                    if d == 0:
                        y_ref[pl.ds(r, SUBLANES), :] = dsk_ref[...] * lhs_ref[sl, :] + yc_ref[sl, :]
                    else:
                        y_ref[pl.ds(r, SUBLANES), :] = y_ref[pl.ds(r, SUBLANES), :] + yc_ref[sl, :]
                return h

            if unit[3]:
                h_init = (h0_ref[d, 0, :, 0:ns], h0_ref[d, 0, :, ns:2 * ns])
            else:
                h_init = (jnp.zeros((SUBLANES, ns), F32), jnp.zeros((SUBLANES, ns), F32))
            f_re, f_im = lax.fori_loop(0, nch, chunk_body, h_init)
            fin_ref[ui, d, 0, :, 0:ns] = f_re
            fin_ref[ui, d, 0, :, ns:2 * ns] = f_im


def _s5(u, dskip, a_re, a_im, bmat, cmat, h0, *, units, tc, grid_rows):
    t, width = u.shape
    nb = width // LANES
    ns = a_re.shape[-1]
    n_units = len(units)
    kern = functools.partial(_s5_kernel, units=units, tc=tc, grid_rows=grid_rows)
    blk4 = lambda *shape: pl.BlockSpec((2, 1) + shape, lambda j: (0, j, 0, 0))
    return pl.pallas_call(
        kern,
        grid=(nb,),
        in_specs=[
            pl.BlockSpec((t, LANES), lambda j: (0, j)),
            pl.BlockSpec((1, LANES), lambda j: (0, j)),
            blk4(1, ns), blk4(1, ns),
            blk4(LANES, 2 * ns), blk4(2 * ns, LANES),
            blk4(SUBLANES, 2 * ns),
        ],
        out_specs=[
            pl.BlockSpec((t, LANES), lambda j: (0, j)),
            pl.BlockSpec((n_units, 2, 1, SUBLANES, 2 * ns), lambda j: (0, 0, j, 0, 0)),
        ],
        out_shape=[
            jax.ShapeDtypeStruct((t, width), F32),
            jax.ShapeDtypeStruct((n_units, 2, nb, SUBLANES, 2 * ns), F32),
        ],
        scratch_shapes=[
            pltpu.VMEM((tc * SUBLANES, LANES), F32),
            pltpu.VMEM((tc * SUBLANES, 2 * ns), F32),
            pltpu.VMEM((tc * SUBLANES, LANES), F32),
        ],
        compiler_params=_params(1),
        name="s5",
    )(u, dskip, a_re, a_im, bmat, cmat, h0)


def _merge_kernel(yrg_ref, zc_ref, ys5_ref, g0_ref, g1_ref, g2_ref, wrg_ref, wcv_ref, wsv_ref,
                  wsg_ref, wo_ref, x_ref, gate_ref, o_ref, acc_ref, rgb_ref, s5b_ref):
    j = pl.program_id(1)

    @pl.when(j == 0)
    def _():
        acc_ref[...] = jnp.zeros_like(acc_ref)
        rgb_ref[...] = yrg_ref[...].astype(rgb_ref.dtype)
        s5b_ref[...] = ys5_ref[...].astype(s5b_ref.dtype)

    dot = lambda a, b: jnp.dot(a, b, preferred_element_type=F32)
    br_rg = dot(rgb_ref[...], wrg_ref[...])
    br_cv = dot(zc_ref[...], wcv_ref[...])
    s5b = s5b_ref[...]
    br_s5 = dot(s5b, wsv_ref[...]) * _sigmoid(dot(s5b, wsg_ref[...]))
    merged = (g0_ref[...].astype(F32) * br_rg + g1_ref[...].astype(F32) * br_cv
              + g2_ref[...].astype(F32) * br_s5)
    acc_ref[...] += dot(merged.astype(MXU_DTYPE), wo_ref[...])

    @pl.when(j == pl.num_programs(1) - 1)
    def _():
        o_ref[...] = _gated_residual(x_ref[...], gate_ref[0], acc_ref[...])


def _merge(y_rg, zc, y_s5, gates, rg_out, cv_out, s5_glu, w_out, x, gate, *, n_sample_rows, tm, tk):
    t, d = x.shape
    rg, cv, s5 = y_rg.shape[1], zc.shape[1], y_s5.shape[1]
    nk = d // tk
    ns = n_sample_rows // tm
    row = lambda w: pl.BlockSpec((tm, w), lambda i, j: (i, 0))
    gspec = lambda k: pl.BlockSpec((tm, tk), lambda i, j: (i, k * nk + j))
    wcol = lambda rows, off: pl.BlockSpec((rows, tk), lambda i, j: (0, off + j))
    return pl.pallas_call(
        _merge_kernel,
        grid=(t // tm, nk),
        in_specs=[
            row(rg), row(cv), row(s5),
            gspec(0), gspec(1), gspec(2),
            wcol(rg, 0), wcol(cv, 0), wcol(s5, 0), wcol(s5, nk),
            pl.BlockSpec((tk, d), lambda i, j: (j, 0)),
            row(d),
            pl.BlockSpec((1, SUBLANES, d), lambda i, j: ((i >= ns).astype(jnp.int32), 0, 0)),
        ],
        out_specs=row(d),
        out_shape=jax.ShapeDtypeStruct((t, d), F32),
        scratch_shapes=[
            pltpu.VMEM((tm, d), F32),
            pltpu.VMEM((tm, rg), MXU_DTYPE),
            pltpu.VMEM((tm, s5), MXU_DTYPE),
        ],
        compiler_params=_params(2),
        name="merge",
    )(y_rg, zc, y_s5, gates, gates, gates, rg_out, cv_out, s5_glu, s5_glu, w_out, x, gate)


def _pack_bf16_pairs(v):
    half = v.shape[1] // 2
    bits = lambda a: lax.bitcast_convert_type(a.astype(jnp.bfloat16).astype(F32), jnp.uint32)
    return (bits(v[:, :half]) >> 16) | (bits(v[:, half:]) & jnp.uint32(0xFFFF0000))


def _store_token_slabs(ref, packed):
    n, width = packed.shape
    r = width // LANES
    for s in range(r):
        ref[pl.ds(s, n, stride=r), :] = packed[:, s * LANES:(s + 1) * LANES]


def _load_token_slabs(ref, first_row, n, r):
    return jnp.concatenate(
        [ref[pl.ds(first_row + s, n, stride=r), :] for s in range(r)], axis=1)


def _unpack_bf16_pairs(p):
    lo = lax.bitcast_convert_type(p << 16, F32)
    hi = lax.bitcast_convert_type(p & jnp.uint32(0xFFFF0000), F32)
    return lo, hi


def _router_kernel(x_ref, g_ref, sc_ref, sh_ref, rw_ref, rb_ref,
                   hnp_ref, eidx_ref, rank_ref, wtok_ref, cnt_ref, run_ref):
    @pl.when(pl.program_id(0) == 0)
    def _():
        run_ref[...] = jnp.zeros_like(run_ref)

    hn = _norm_mod(x_ref[...], g_ref[...], sc_ref[0], sh_ref[0])
    _store_token_slabs(hnp_ref, _pack_bf16_pairs(hn))
    logits = lax.dot_general(rw_ref[...], hn, (((1,), (1,)), ((), ())),
                             precision=lax.Precision.HIGHEST, preferred_element_type=F32)
    s = _sigmoid(logits)
    choice = s + rb_ref[...]
    n_exp, tm = choice.shape
    gsize = n_exp // N_ROUTE_GROUPS
    neg_inf = jnp.float32(-jnp.inf)
    c3 = choice.reshape(N_ROUTE_GROUPS, gsize, tm)
    sub = lax.broadcasted_iota(jnp.int32, c3.shape, 1)
    m1 = jnp.max(c3, axis=1, keepdims=True)
    i1 = jnp.min(jnp.where(c3 == m1, sub, gsize), axis=1, keepdims=True)
    m2 = jnp.max(jnp.where(sub == i1, neg_inf, c3), axis=1, keepdims=True)
    gscore = jnp.broadcast_to(m1 + m2, c3.shape)
    gidx = lax.broadcasted_iota(jnp.int32, c3.shape, 0)
    beaten = jnp.zeros(c3.shape, jnp.int32)
    for gp in range(N_ROUTE_GROUPS):
        other = gscore[gp:gp + 1]
        wins = (other > gscore) | ((other == gscore) & (gidx > gp))
        beaten = beaten + wins.astype(jnp.int32)
    masked = jnp.where(beaten < TOPK_GROUPS, c3, neg_inf).reshape(n_exp, tm)
    eidx = lax.broadcasted_iota(jnp.int32, masked.shape, 0)
    beaten = jnp.zeros(masked.shape, jnp.int32)
    for ep in range(n_exp):
        other = masked[ep:ep + 1, :]
        wins = (other > masked) | ((other == masked) & (eidx > ep))
        beaten = beaten + wins.astype(jnp.int32)
    sel = beaten < TOP_K
    w = jnp.where(sel, s, 0.0)
    comb = ROUTE_SCALE * w / jnp.sum(w, axis=0, keepdims=True)
    self32 = sel.astype(F32)
    before = (lax.broadcasted_iota(jnp.int32, (tm, tm), 0) < lax.broadcasted_iota(jnp.int32, (tm, tm), 1))
    rank = run_ref[...] + _mm(self32, before.astype(F32))
    run_ref[...] = run_ref[...] + jnp.sum(self32, axis=1, keepdims=True)
    cnt_ref[...] = jnp.broadcast_to(run_ref[...], cnt_ref.shape)
    eidx_f = eidx.astype(F32)
    w_rows = []
    for j in range(TOP_K):
        m = beaten == j
        pick = lambda v: jnp.sum(jnp.where(m, v, 0.0), axis=0, keepdims=True)
        eidx_ref[j:j + 1, :] = pick(eidx_f).astype(jnp.int32)
        rank_ref[j:j + 1, :] = pick(rank).astype(jnp.int32)
        w_rows.append(pick(comb))
    w_rows.append(jnp.zeros((wtok_ref.shape[1] - TOP_K, tm), F32))
    wtok_ref[...] = jnp.concatenate(w_rows, axis=0).T


def _router(x, g, scale, shift, router_wt, router_b, *, n_sample_rows, tm):
    t, d = x.shape
    n_exp = router_wt.shape[0]
    ns = n_sample_rows // tm
    unit = lambda i: ((i >= ns).astype(jnp.int32), 0, 0)
    rpt = d // 2 // LANES
    return pl.pallas_call(
        _router_kernel,
        grid=(t // tm,),
        in_specs=[
            pl.BlockSpec((tm, d), lambda i: (i, 0)),
            pl.BlockSpec((1, d), lambda i: (0, 0)),
            pl.BlockSpec((1, SUBLANES, d), unit),
            pl.BlockSpec((1, SUBLANES, d), unit),
            pl.BlockSpec((n_exp, d), lambda i: (0, 0)),
            pl.BlockSpec((n_exp, 1), lambda i: (0, 0)),
        ],
        out_specs=[
            pl.BlockSpec((tm * rpt, LANES), lambda i: (i, 0)),
            pl.BlockSpec((TOP_K, tm), lambda i: (0, i)),
            pl.BlockSpec((TOP_K, tm), lambda i: (0, i)),
            pl.BlockSpec((tm, LANES), lambda i: (i, 0)),
            pl.BlockSpec((n_exp, LANES), lambda i: (0, 0)),
        ],
        out_shape=[
            jax.ShapeDtypeStruct((t * rpt, LANES), jnp.uint32),
            jax.ShapeDtypeStruct((TOP_K, t), jnp.int32),
            jax.ShapeDtypeStruct((TOP_K, t), jnp.int32),
            jax.ShapeDtypeStruct((t, LANES), F32),
            jax.ShapeDtypeStruct((n_exp, LANES), F32),
        ],
        scratch_shapes=[pltpu.VMEM((n_exp, 1), F32)],
        compiler_params=_params(1),
        name="router",
    )(x, g, scale, shift, router_wt, router_b)


def _swiglu(lo, hi, w1, w3, w2):
    half = lo.shape[1]
    lo = lo.astype(MXU_DTYPE)
    hi = hi.astype(MXU_DTYPE)
    dot = lambda a, b: jnp.dot(a, b, preferred_element_type=F32)
    h1 = dot(lo, w1[0:half, :]) + dot(hi, w1[half:, :])
    h3 = dot(lo, w3[0:half, :]) + dot(hi, w3[half:, :])
    return dot((_silu(h1) * h3).astype(MXU_DTYPE), w2[...])


def _dispatch_kernel(pos_ref, hn_ref, xs_in_hbm, xs_hbm, sem):
    del xs_in_hbm
    n_slots, tm = pos_ref.shape
    rpt = hn_ref.shape[0] // tm

    def slab(k):
        return pl.ds(pl.multiple_of(k * rpt, rpt), rpt)

    def issue(t, carry):
        src = hn_ref.at[slab(t)]
        for j in range(n_slots):
            pltpu.make_async_copy(src, xs_hbm.at[slab(pos_ref[j, t])], sem).start(priority=j % 2)
        return carry

    lax.fori_loop(0, tm, issue, 0, unroll=4)
    for j in range(n_slots):
        pltpu.make_async_copy(hn_ref, xs_hbm.at[pl.ds(0, tm * rpt)], sem).wait()


def _dispatch(pos, hnp, xs_init, *, tm):
    n_slots, t = pos.shape
    rpt = hnp.shape[0] // t
    return pl.pallas_call(
        _dispatch_kernel,
        grid=(t // tm,),
        in_specs=[
            pl.BlockSpec((n_slots, tm), lambda i: (0, i), memory_space=pltpu.SMEM),
            pl.BlockSpec((tm * rpt, LANES), lambda i: (i, 0)),
            pl.BlockSpec(memory_space=pl.ANY),
        ],
        out_specs=pl.BlockSpec(memory_space=pl.ANY),
        out_shape=jax.ShapeDtypeStruct(xs_init.shape, xs_init.dtype),
        scratch_shapes=[pltpu.SemaphoreType.DMA],
        input_output_aliases={2: 0},
        compiler_params=_params(1),
        name="dispatch",
    )(pos, hnp, xs_init)


def _ffn_kernel(te_ref, nu_ref, xs_ref, w1_ref, w3_ref, w2_ref, ys_ref, w1b_ref, w3b_ref, w2b_ref):
    i = pl.program_id(0)

    @pl.when((i == 0) | (te_ref[i] != te_ref[jnp.maximum(i - 1, 0)]))
    def _():
        w1b_ref[...] = w1_ref[0, 0].astype(w1b_ref.dtype)
        w3b_ref[...] = w3_ref[0, 0].astype(w3b_ref.dtype)
        w2b_ref[...] = w2_ref[0, 0].astype(w2b_ref.dtype)

    @pl.when(i < nu_ref[0])
    def _():
        rpt = w1b_ref.shape[0] // 2 // LANES
        lo, hi = _unpack_bf16_pairs(_load_token_slabs(xs_ref, 0, xs_ref.shape[0] // rpt, rpt))
        _store_token_slabs(ys_ref, _pack_bf16_pairs(_swiglu(lo, hi, w1b_ref, w3b_ref, w2b_ref)))

    @pl.when(i >= nu_ref[0])
    def _():
        ys_ref[...] = jnp.zeros_like(ys_ref)


def _ffn(tile_expert, n_used, xs, w1, w3, w2, *, layer, tg):
    _, _, d, f = w1.shape
    rpt = d // 2 // LANES
    n_tiles = xs.shape[0] // (tg * rpt)
    row = lambda i, te, nu: (jnp.maximum(jnp.minimum(i, nu[0] - 1), 0), 0)
    wblk = lambda i, te, nu: (layer, te[i], 0, 0)
    grid_spec = pltpu.PrefetchScalarGridSpec(
        num_scalar_prefetch=2,
        grid=(n_tiles,),
        in_specs=[
            pl.BlockSpec((tg * rpt, LANES), row),
            pl.BlockSpec((1, 1, d, f), wblk),
            pl.BlockSpec((1, 1, d, f), wblk),
            pl.BlockSpec((1, 1, f, d), wblk),
        ],
        out_specs=pl.BlockSpec((tg * rpt, LANES), lambda i, te, nu: (i, 0)),
        scratch_shapes=[
            pltpu.VMEM((d, f), MXU_DTYPE),
            pltpu.VMEM((d, f), MXU_DTYPE),
            pltpu.VMEM((f, d), MXU_DTYPE),
        ],
    )
    return pl.pallas_call(
        _ffn_kernel,
        grid_spec=grid_spec,
        out_shape=jax.ShapeDtypeStruct(xs.shape, jnp.uint32),
        compiler_params=_params(1),
        name="ffn",
    )(tile_expert, n_used, xs, w1, w3, w2)


def _combine_kernel(pos_ref, hnp_ref, wtok_ref, s1_ref, s3_ref, s2_ref, x_ref, gate_ref, ys_hbm,
                    o_ref, buf_ref, sem):
    n_slots, tm = pos_ref.shape
    rpt = hnp_ref.shape[0] // tm
    slab_rows = tm * rpt

    def slab(k):
        return pl.ds(pl.multiple_of(k * rpt, rpt), rpt)

    def issue(t, carry):
        for j in range(n_slots):
            pltpu.make_async_copy(ys_hbm.at[slab(pos_ref[j, t])],
                                  buf_ref.at[slab(j * tm + t)], sem).start(priority=j % 2)
        return carry

    lax.fori_loop(0, tm, issue, 0, unroll=4)
    lo, hi = _unpack_bf16_pairs(_load_token_slabs(hnp_ref, 0, tm, rpt))
    moe = _swiglu(lo, hi, s1_ref, s3_ref, s2_ref)
    half = lo.shape[1]
    acc_lo = moe[:, :half]
    acc_hi = moe[:, half:]
    wtok = wtok_ref[...]
    for j in range(n_slots):
        pltpu.make_async_copy(ys_hbm.at[pl.ds(0, slab_rows)],
                              buf_ref.at[pl.ds(j * slab_rows, slab_rows)], sem).wait()
    for j in range(n_slots):
        y_lo, y_hi = _unpack_bf16_pairs(_load_token_slabs(buf_ref, j * slab_rows, tm, rpt))
        wj = wtok[:, j:j + 1]
        acc_lo = acc_lo + wj * y_lo
        acc_hi = acc_hi + wj * y_hi
    x = x_ref[...]
    gate = gate_ref[0]
    o_ref[:, :half] = _gated_residual(x[:, :half], gate[:, :half], acc_lo)
    o_ref[:, half:] = _gated_residual(x[:, half:], gate[:, half:], acc_hi)


def _combine(pos, hnp, wtok, s1, s3, s2, x, gate, ys, *, n_sample_rows, tm):
    t, d = x.shape
    n_slots = pos.shape[0]
    f = s1.shape[1]
    ns = n_sample_rows // tm
    rpt = d // 2 // LANES
    full = lambda shape: pl.BlockSpec(shape, lambda i: (0,) * len(shape))
    return pl.pallas_call(
        _combine_kernel,
        grid=(t // tm,),
        in_specs=[
            pl.BlockSpec((n_slots, tm), lambda i: (0, i), memory_space=pltpu.SMEM),
            pl.BlockSpec((tm * rpt, LANES), lambda i: (i, 0)),
            pl.BlockSpec((tm, LANES), lambda i: (i, 0)),
            full((d, f)), full((d, f)), full((f, d)),
            pl.BlockSpec((tm, d), lambda i: (i, 0)),
            pl.BlockSpec((1, SUBLANES, d), lambda i: ((i >= ns).astype(jnp.int32), 0, 0)),
            pl.BlockSpec(memory_space=pl.ANY),
        ],
        out_specs=pl.BlockSpec((tm, d), lambda i: (i, 0)),
        out_shape=jax.ShapeDtypeStruct((t, d), F32),
        scratch_shapes=[
            pltpu.VMEM((n_slots * tm * rpt, LANES), jnp.uint32),
            pltpu.SemaphoreType.DMA,
        ],
        compiler_params=_params(1),
        name="combine",
    )(pos, hnp, wtok, s1, s3, s2, x, gate, ys)


def _slots_kernel(cnt_ref, eidx_ref, rank_ref, pos_ref, te_ref, nu_ref, *, tg):
    counts = cnt_ref[...]
    n_exp = counts.shape[0]
    padded = jnp.floor((counts + (tg - 1)) * (1.0 / tg)) * tg
    incl = (lax.broadcasted_iota(jnp.int32, (n_exp, n_exp), 0)
            >= lax.broadcasted_iota(jnp.int32, (n_exp, n_exp), 1)).astype(F32)
    ends = jnp.dot(incl, padded, precision=lax.Precision.HIGHEST, preferred_element_type=F32)
    ends_col = ends[:, 0:1]
    offs_col = ends_col - padded[:, 0:1]
    eidx = eidx_ref[...]
    rank = rank_ref[...]
    expert = lax.broadcasted_iota(jnp.int32, (n_exp, eidx.shape[1]), 0)
    for j in range(eidx.shape[0]):
        off = jnp.sum(jnp.where(expert == eidx[j:j + 1, :], offs_col, 0.0), axis=0, keepdims=True)
        pos_ref[j:j + 1, :] = off.astype(jnp.int32) + rank[j:j + 1, :]
    n_used = ends[n_exp - 1:n_exp, 0:1] * (1.0 / tg)
    tile = lax.broadcasted_iota(jnp.int32, te_ref.shape, 1).astype(F32)
    tile_row = jnp.minimum(tile, n_used - 1.0) * tg
    te_ref[...] = jnp.sum((ends_col <= tile_row).astype(F32), axis=0, keepdims=True).astype(jnp.int32)
    nu_ref[...] = jnp.broadcast_to(n_used, nu_ref.shape).astype(jnp.int32)


def _slots(counts, eidx, rank, *, tg, n_tiles, tb):
    assert tg & (tg - 1) == 0
    n_slots, t = eidx.shape
    n_tiles_pad = pl.cdiv(n_tiles, LANES) * LANES
    pos, te, nu = pl.pallas_call(
        functools.partial(_slots_kernel, tg=tg),
        grid=(t // tb,),
        in_specs=[
            pl.BlockSpec(counts.shape, lambda i: (0, 0)),
            pl.BlockSpec((n_slots, tb), lambda i: (0, i)),
            pl.BlockSpec((n_slots, tb), lambda i: (0, i)),
        ],
        out_specs=[
            pl.BlockSpec((n_slots, tb), lambda i: (0, i)),
            pl.BlockSpec((1, n_tiles_pad), lambda i: (0, 0)),
            pl.BlockSpec((1, LANES), lambda i: (0, 0)),
        ],
        out_shape=[
            jax.ShapeDtypeStruct((n_slots, t), jnp.int32),
            jax.ShapeDtypeStruct((1, n_tiles_pad), jnp.int32),
            jax.ShapeDtypeStruct((1, LANES), jnp.int32),
        ],
        compiler_params=_params(1),
        name="slots",
    )(counts, eidx, rank)
    return pos, te[0, :n_tiles], nu[0, :1]


def _final_norm_kernel(x_ref, g_ref, os_ref, oc_ref, y_ref, *, n_sample_tiles):
    x = x_ref[...]
    ms = jnp.mean(x * x, axis=-1, keepdims=True)
    y = x * lax.rsqrt(ms + EPS) * g_ref[...]
    steps = x.shape[0] // SUBLANES
    n_lane_blocks = x.shape[1] // LANES
    for k in range(n_lane_blocks):
        y_ref[k] = y[:, k * LANES:(k + 1) * LANES]

    def emit(o_ref):
        for b in range(SUBLANES):
            for k in range(n_lane_blocks):
                o_ref[b, :, k * LANES:(k + 1) * LANES] = y_ref[k, pl.ds(b, steps, stride=SUBLANES), :]

    @pl.when(pl.program_id(0) < n_sample_tiles)
    def _():
        emit(os_ref)

    @pl.when(pl.program_id(0) >= n_sample_tiles)
    def _():
        emit(oc_ref)


def _final_norm(x, g, *, ls, lc, n_ctx_units, tm):
    t, d = x.shape
    steps = tm // SUBLANES
    nst = ls // steps
    ctx_tiles = lc // steps
    n_ctx = n_ctx_units * ctx_tiles

    def ctx_block(i):
        k = jnp.clip(i - nst, 0, n_ctx - 1)
        return (k // ctx_tiles, k % ctx_tiles, 0)

    return pl.pallas_call(
        functools.partial(_final_norm_kernel, n_sample_tiles=nst),
        grid=(t // tm,),
        in_specs=[pl.BlockSpec((tm, d), lambda i: (i, 0)), pl.BlockSpec((1, d), lambda i: (0, 0))],
        out_specs=[
            pl.BlockSpec((SUBLANES, steps, d), lambda i: (0, jnp.minimum(i, nst - 1), 0)),
            pl.BlockSpec((SUBLANES, steps, d), ctx_block),
        ],
        out_shape=[
            jax.ShapeDtypeStruct((SUBLANES, ls, d), F32),
            jax.ShapeDtypeStruct((n_ctx_units * SUBLANES, lc, d), F32),
        ],
        scratch_shapes=[pltpu.VMEM((d // LANES, tm, LANES), F32)],
        compiler_params=_params(1),
        name="final_norm",
    )(x, g)


def _s5_discretise(lam_re, lam_im, log_dt, b_re, b_im, c_re, c_im):
    two, g, p = lam_re.shape
    s = b_re.shape[-1]
    gb = LANES // s
    nb = g // gb
    dt = jnp.exp(log_dt)[..., None]
    mag = jnp.exp(lam_re * dt)
    a_re = mag * jnp.cos(lam_im * dt)
    a_im = mag * jnp.sin(lam_im * dt)
    den = lam_re * lam_re + lam_im * lam_im
    q_re = ((a_re - 1.0) * lam_re + a_im * lam_im) / den
    q_im = (a_im * lam_re - (a_re - 1.0) * lam_im) / den
    bb_re = q_re[..., None] * b_re - q_im[..., None] * b_im
    bb_im = q_re[..., None] * b_im + q_im[..., None] * b_re
    eye = jnp.eye(gb, dtype=F32)

    def bdiag_b(m):
        m = m.reshape(two, nb, gb, p, s)
        return jnp.einsum("dbgps,gh->dbgshp", m, eye).reshape(two, nb, gb * s, gb * p)

    def bdiag_c(m):
        m = m.reshape(two, nb, gb, s, p)
        return jnp.einsum("dbgsp,gh->dbgphs", m, eye).reshape(two, nb, gb * p, gb * s)

    bmat = jnp.concatenate([bdiag_b(bb_re), bdiag_b(bb_im)], axis=-1)
    cmat = jnp.concatenate([bdiag_c(c_re), -bdiag_c(c_im)], axis=-2)
    blk = lambda a: a.reshape(two, nb, 1, gb * p)
    return blk(a_re), blk(a_im), bmat.astype(MXU_DTYPE), cmat.astype(MXU_DTYPE)


def kernel(x_prompt, x_sample, state_rglru, state_s5, c, c_ctx, norm_mix, norm_ffn, ada_w, ada_b, w_in, b_gate, rg_conv_w, rg_conv_b, rg_wa, rg_ba, rg_wx, rg_bx, rg_lambda, rg_out, cv_dw, cv_db, cv_ln_g, cv_ln_b, cv_out, s5_lambda_re, s5_lambda_im, s5_log_dt, s5_b_re, s5_b_im, s5_c_re, s5_c_im, s5_d, s5_glu, w_out, router_w, router_b, exp_w1, exp_w3, exp_w2, sh_w1, sh_w3, sh_w2, norm_final):
    batch, lc, d = x_prompt.shape
    dec_batch, ls, _ = x_sample.shape
    depth = w_in.shape[0]
    rg = rg_out.shape[1]
    cv = cv_out.shape[1]
    s5 = s5_glu.shape[1]
    n_groups_s5, n_state = s5_lambda_re.shape[2], s5_lambda_re.shape[3]
    assert dec_batch == SUBLANES and batch % SUBLANES == 0
    assert ls % GRID_W == 0 and lc % CHUNK_TILES == 0
    assert rg_wa.shape[-1] == LANES
    n_ctx_units = batch // SUBLANES
    grid_rows = ls // GRID_W
    n_sample_rows = ls * SUBLANES
    units = ((0, ls, True, True),) + tuple(
        (ls + k * lc, lc, False, False) for k in range(n_ctx_units))
    tm = 512
    tn = min(1024, rg, cv, s5)
    tk = min(512, d)
    tc = min(128, lc)
    tg = 512
    tmc = 256
    n_tokens = (ls + n_ctx_units * lc) * SUBLANES
    n_exp = router_w.shape[-1]
    assert tc % grid_rows == 0 and (GRID_W * grid_rows) % tc == 0
    assert n_sample_rows % tm == 0 and (lc * SUBLANES) % tm == 0 and n_sample_rows % tmc == 0
    assert all(w % tn == 0 for w in (rg, cv, s5, b_gate.shape[-1])) and d % tk == 0
    n_ffn_tiles = pl.cdiv(n_tokens * TOP_K, tg) + n_exp

    xs = jnp.transpose(x_sample, (1, 0, 2)).reshape(ls * SUBLANES, d)
    xc = jnp.transpose(x_prompt.reshape(n_ctx_units, SUBLANES, lc, d), (0, 2, 1, 3))
    x = jnp.concatenate([xs, xc.reshape(n_ctx_units * lc * SUBLANES, d)], axis=0)

    cond = jnp.concatenate([c, c_ctx[None], jnp.zeros((SUBLANES - 1, d), F32)], axis=0)
    mod = _ada(cond, ada_w, ada_b)
    mod = jnp.stack([mod[:, :SUBLANES],
                     jnp.broadcast_to(mod[:, SUBLANES:SUBLANES + 1], (depth, SUBLANES, 6 * d))], axis=1)
    mod = mod.reshape(depth, 2, SUBLANES, 6, d)

    cast = lambda a: a.astype(MXU_DTYPE)
    sorted_rows = jnp.zeros((n_ffn_tiles * tg * (d // 2 // LANES), LANES), jnp.uint32)
    rg_fin, s5_fin = [], []
    for l in range(depth):
        shift1, scale1, gate1, shift2, scale2, gate2 = (mod[l, :, :, k] for k in range(6))
        u_rg, z_cv, u_s5, gates = _win(
            x, norm_mix[l][None], scale1, shift1, cast(w_in[l]), b_gate[l][None],
            rg=rg, cv=cv, s5=s5, n_sample_rows=n_sample_rows, tm=tm, tn=tn)

        h0_rg = jnp.transpose(state_rglru[:, l], (1, 0, 2))
        heads = rg // LANES
        y_rg, fin_rg = _rglru(
            u_rg, rg_conv_w[l], rg_conv_b[l][None],
            cast(rg_wa[l]), cast(rg_wx[l]), rg_ba[l], rg_bx[l], rg_lambda[l], h0_rg, units=units)
        rg_fin.append(fin_rg)

        dw_tiles = jnp.broadcast_to(cv_dw[l][:, None, :], (cv_dw.shape[1], SUBLANES, cv))
        zc = _conv_branch(z_cv, dw_tiles, cv_db[l][None], cv_ln_g[l][None], cv_ln_b[l][None],
                          n_sample_chunks=ls // CHUNK_TILES, ctx_unit_chunks=lc // CHUNK_TILES)

        a_re, a_im, bmat, cmat = _s5_discretise(
            s5_lambda_re[l], s5_lambda_im[l], s5_log_dt[l], s5_b_re[l], s5_b_im[l],
            s5_c_re[l], s5_c_im[l])
        nb = s5 // LANES
        st = jnp.transpose(state_s5[:, l], (1, 4, 0, 2, 3))
        st = st.reshape(2, 2, SUBLANES, nb, (n_groups_s5 // nb) * n_state)
        h0_s5 = jnp.transpose(st, (0, 3, 2, 1, 4)).reshape(2, nb, SUBLANES, -1)
        y_s5, fin_s5 = _s5(u_s5, s5_d[l][None], a_re, a_im, bmat, cmat, h0_s5,
                           units=units, tc=tc, grid_rows=grid_rows)
        s5_fin.append(fin_s5)

        x = _merge(y_rg, zc, y_s5, gates, cast(rg_out[l]), cast(cv_out[l]), cast(s5_glu[l]),
                   cast(w_out[l]), x, gate1, n_sample_rows=n_sample_rows, tm=tm, tk=tk)

        hnp, eidx, rank, wtok, counts = _router(
            x, norm_ffn[l][None], scale2, shift2, router_w[l].T, router_b[l][:, None],
            n_sample_rows=n_sample_rows, tm=tm)
        pos, tile_expert, n_used = _slots(counts, eidx, rank, tg=tg, n_tiles=n_ffn_tiles, tb=tm)
        sorted_rows = _dispatch(pos, hnp, sorted_rows, tm=tm)
        ys = _ffn(tile_expert, n_used, sorted_rows, exp_w1, exp_w3, exp_w2, layer=l, tg=tg)
        x = _combine(pos, hnp, wtok, cast(sh_w1[l]), cast(sh_w3[l]), cast(sh_w2[l]), x, gate2, ys,
                     n_sample_rows=n_sample_rows, tm=tmc)

    y_sample, y_prompt = _final_norm(x, norm_final[None], ls=ls, lc=lc, n_ctx_units=n_ctx_units, tm=tm)

    fr = jnp.stack(rg_fin, axis=0)[:, 1:]
    new_state_rglru = jnp.transpose(fr, (1, 3, 0, 2, 4)).reshape(batch, depth, 2, rg)
    fs = jnp.stack(s5_fin, axis=0)[:, 1:]
    nb = s5 // LANES
    fs = fs.reshape(depth, n_ctx_units, 2, nb, SUBLANES, 2, n_groups_s5 // nb, n_state)
    new_state_s5 = jnp.transpose(fs, (1, 4, 0, 2, 3, 6, 7, 5)).reshape(
        batch, depth, 2, n_groups_s5, n_state, 2)
    return y_prompt, y_sample, new_state_rglru, new_state_s5
```

```python
import functools
import math

import jax
import jax.numpy as jnp
from jax import lax
from jax.experimental import pallas as pl
from jax.experimental.pallas import tpu as pltpu

F32 = jnp.float32
MXU_DTYPE = jnp.bfloat16

SUBLANES = 8
LANES = 128
VMEM_LIMIT_BYTES = 56 * 1024 * 1024

GRID_W = 64
RG_C = 8.0
S5_GROUP = 16
TOP_K = 8
N_ROUTE_GROUPS = 8
TOPK_GROUPS = 4
ROUTE_SCALE = 2.5
EPS = 1e-6

CHUNK_TILES = 64
CONV_HALO_TILES = 16


def _params(n_grid_dims):
    return pltpu.CompilerParams(
        dimension_semantics=("arbitrary",) * n_grid_dims,
        vmem_limit_bytes=VMEM_LIMIT_BYTES,
    )


def _mm(a, b):
    return jnp.dot(a.astype(MXU_DTYPE), b.astype(MXU_DTYPE), preferred_element_type=F32)


def _sigmoid(x):
    return 0.5 * jnp.tanh(0.5 * x) + 0.5


def _silu(x):
    return x * _sigmoid(x)


def _tile_bcast_mul_add(y, scale, shift):
    rows, d = y.shape
    y3 = y.reshape(rows // SUBLANES, SUBLANES, d)
    return (y3 * scale[None] + shift[None]).reshape(rows, d)


def _gated_residual(x, gate, y):
    rows, d = y.shape
    return (y.reshape(rows // SUBLANES, SUBLANES, d) * gate[None]).reshape(rows, d) + x


def _norm_mod(x, g, scale, shift):
    ms = jnp.mean(x * x, axis=-1, keepdims=True)
    y = x * lax.rsqrt(ms + EPS) * g
    return _tile_bcast_mul_add(y, 1.0 + scale, shift)


def _ada_kernel(c_ref, w_ref, b_ref, o_ref):
    o_ref[0] = _mm(_silu(c_ref[...]), w_ref[0]) + b_ref[0]


def _ada(cond, ada_w, ada_b):
    depth, d, n = ada_w.shape
    tn = math.gcd(n, 1024)
    rows = cond.shape[0]
    return pl.pallas_call(
        _ada_kernel,
        grid=(depth, n // tn),
        in_specs=[
            pl.BlockSpec((rows, d), lambda l, j: (0, 0)),
            pl.BlockSpec((1, d, tn), lambda l, j: (l, 0, j)),
            pl.BlockSpec((1, 1, tn), lambda l, j: (l, 0, j)),
        ],
        out_specs=pl.BlockSpec((1, rows, tn), lambda l, j: (l, 0, j)),
        out_shape=jax.ShapeDtypeStruct((depth, rows, n), F32),
        compiler_params=_params(2),
        name="ada",
    )(cond, ada_w, ada_b.reshape(depth, 1, n))


def _win_kernel(x_ref, g_ref, sc_ref, sh_ref, wa_ref, wb_ref, bg_ref,
                urg_ref, z_ref, us5_ref, gt_ref, hn_ref, *, n_rg, n_cv, n_s5):
    j = pl.program_id(1)

    @pl.when(j == 0)
    def _():
        hn_ref[...] = _norm_mod(x_ref[...], g_ref[...], sc_ref[0], sh_ref[0]).astype(hn_ref.dtype)

    hn = hn_ref[...]
    a = jnp.dot(hn, wa_ref[...], preferred_element_type=F32)

    @pl.when(j < n_rg)
    def _():
        urg_ref[...] = a

    @pl.when((j >= n_rg) & (j < n_rg + n_cv))
    def _():
        b = jnp.dot(hn, wb_ref[...], preferred_element_type=F32)
        z_ref[...] = a * _sigmoid(b)

    @pl.when((j >= n_rg + n_cv) & (j < n_rg + n_cv + n_s5))
    def _():
        us5_ref[...] = a

    @pl.when(j >= n_rg + n_cv + n_s5)
    def _():
        gt_ref[...] = _sigmoid(a + bg_ref[...]).astype(gt_ref.dtype)


def _win(x, g, scale, shift, w_in, b_gate, *, rg, cv, s5, n_sample_rows, tm, tn):
    t, d = x.shape
    n_gate = b_gate.shape[-1]
    n_rg, n_cv, n_s5, n_g = rg // tn, cv // tn, s5 // tn, n_gate // tn
    ns = n_sample_rows // tm
    unit = lambda i, j: ((i >= ns).astype(jnp.int32), 0, 0)
    clip = lambda v, n: jnp.clip(v, 0, n - 1)
    kern = functools.partial(_win_kernel, n_rg=n_rg, n_cv=n_cv, n_s5=n_s5)
    return pl.pallas_call(
        kern,
        grid=(t // tm, n_rg + n_cv + n_s5 + n_g),
        in_specs=[
            pl.BlockSpec((tm, d), lambda i, j: (i, 0)),
            pl.BlockSpec((1, d), lambda i, j: (0, 0)),
            pl.BlockSpec((1, SUBLANES, d), unit),
            pl.BlockSpec((1, SUBLANES, d), unit),
            pl.BlockSpec((d, tn), lambda i, j: (0, jnp.where(j < n_rg + n_cv, j, j + n_cv))),
            pl.BlockSpec((d, tn), lambda i, j: (0, n_rg + n_cv + clip(j - n_rg, n_cv))),
            pl.BlockSpec((1, tn), lambda i, j: (0, clip(j - (n_rg + n_cv + n_s5), n_g))),
        ],
        out_specs=[
            pl.BlockSpec((tm, tn), lambda i, j: (i, clip(j, n_rg))),
            pl.BlockSpec((tm, tn), lambda i, j: (i, clip(j - n_rg, n_cv))),
            pl.BlockSpec((tm, tn), lambda i, j: (i, clip(j - n_rg - n_cv, n_s5))),
            pl.BlockSpec((tm, tn), lambda i, j: (i, clip(j - n_rg - n_cv - n_s5, n_g))),
        ],
        out_shape=[
            jax.ShapeDtypeStruct((t, rg), F32),
            jax.ShapeDtypeStruct((t, cv), F32),
            jax.ShapeDtypeStruct((t, s5), F32),
            jax.ShapeDtypeStruct((t, n_gate), MXU_DTYPE),
        ],
        scratch_shapes=[pltpu.VMEM((tm, d), MXU_DTYPE)],
        compiler_params=_params(2),
        name="win",
    )(x, g, scale, shift, w_in, w_in, b_gate)


def _rg_kernel(u_ref, cw_ref, cb_ref, wa_ref, wx_ref, ba_ref, bx_ref, lam_ref, h0_ref,
               y_ref, fin_ref, ext_ref, a_ref, b_ref, *, units, ch, kconv):
    rows = ch * SUBLANES
    pad_lo = (kconv // 2) * SUBLANES
    pad_hi = (kconv - 1 - kconv // 2) * SUBLANES
    t_rows = u_ref.shape[0]
    for d in (0, 1):
        neg_lam = -lam_ref[d:d + 1, :]
        softplus = jnp.maximum(neg_lam, 0.0) + jnp.log1p(jnp.exp(-jnp.abs(neg_lam)))
        coef = -RG_C * softplus
        wa = wa_ref[d, 0]
        wx = wx_ref[d, 0]
        ba = ba_ref[d:d + 1, :]
        bx = bx_ref[d:d + 1, :]
        for ui, (t0, nt, _, has_h0) in enumerate(units):
            nch = nt // ch

            def chunk_body(ci, h, d=d, t0=t0, nch=nch, coef=coef, wa=wa, wx=wx, ba=ba, bx=bx):
                c = ci if d == 0 else nch - 1 - ci
                r0 = pl.multiple_of((t0 + c * ch) * SUBLANES, SUBLANES)
                lo_start = pl.multiple_of(jnp.maximum(r0 - pad_lo, 0), SUBLANES)
                hi_start = pl.multiple_of(jnp.minimum(r0 + rows, t_rows - pad_hi), SUBLANES)
                ext_ref[0:pad_lo, :] = jnp.where(c > 0, u_ref[pl.ds(lo_start, pad_lo), :], 0.0)
                ext_ref[pad_lo:pad_lo + rows, :] = u_ref[pl.ds(r0, rows), :]
                ext_ref[pad_lo + rows:pad_lo + rows + pad_hi, :] = jnp.where(
                    c < nch - 1, u_ref[pl.ds(hi_start, pad_hi), :], 0.0)
                xc = cb_ref[...] + cw_ref[0:1, :] * ext_ref[0:rows, :]
                for k in range(1, kconv):
                    xc = xc + cw_ref[k:k + 1, :] * ext_ref[k * SUBLANES:k * SUBLANES + rows, :]
                r = _sigmoid(_mm(xc, wa) + ba)
                i = _sigmoid(_mm(xc, wx) + bx)
                a = jnp.exp(coef * r)
                a_ref[...] = a
                b_ref[...] = jnp.sqrt(1.0 - a * a) * i * xc

                group = SUBLANES

                def steps(gi, h):
                    g = gi if d == 0 else ch // group - 1 - gi
                    base = pl.multiple_of(g * group * SUBLANES, group * SUBLANES)
                    for k in range(group):
                        o = pl.ds(base + (k if d == 0 else group - 1 - k) * SUBLANES, SUBLANES)
                        h = a_ref[o, :] * h + b_ref[o, :]
                        b_ref[o, :] = h
                    return h

                h = lax.fori_loop(0, ch // group, steps, h)
                if d == 0:
                    y_ref[pl.ds(r0, rows), :] = b_ref[...]
                else:
                    y_ref[pl.ds(r0, rows), :] = y_ref[pl.ds(r0, rows), :] + b_ref[...]
                return h

            h_init = h0_ref[d] if has_h0 else jnp.zeros((SUBLANES, LANES), F32)
            fin_ref[ui, d] = lax.fori_loop(0, nch, chunk_body, h_init)


def _rglru(u, conv_w, conv_b, wa, wx, ba, bx, lam, h0, *, units):
    t, rg = u.shape
    heads = rg // LANES
    kconv = conv_w.shape[0]
    n_units = len(units)
    rows = CHUNK_TILES * SUBLANES
    kern = functools.partial(_rg_kernel, units=units, ch=CHUNK_TILES, kconv=kconv)
    col = lambda h: (0, h)
    return pl.pallas_call(
        kern,
        grid=(heads,),
        in_specs=[
            pl.BlockSpec((t, LANES), col),
            pl.BlockSpec((kconv, LANES), col),
            pl.BlockSpec((1, LANES), col),
            pl.BlockSpec((2, 1, LANES, LANES), lambda h: (0, h, 0, 0)),
            pl.BlockSpec((2, 1, LANES, LANES), lambda h: (0, h, 0, 0)),
            pl.BlockSpec((2, LANES), col),
            pl.BlockSpec((2, LANES), col),
            pl.BlockSpec((2, LANES), col),
            pl.BlockSpec((2, SUBLANES, LANES), lambda h: (0, 0, h)),
        ],
        out_specs=[
            pl.BlockSpec((t, LANES), col),
            pl.BlockSpec((n_units, 2, SUBLANES, LANES), lambda h: (0, 0, 0, h)),
        ],
        out_shape=[
            jax.ShapeDtypeStruct((t, rg), F32),
            jax.ShapeDtypeStruct((n_units, 2, SUBLANES, rg), F32),
        ],
        scratch_shapes=[
            pltpu.VMEM((rows + (kconv - 1) * SUBLANES, LANES), F32),
            pltpu.VMEM((rows, LANES), F32),
            pltpu.VMEM((rows, LANES), F32),
        ],
        compiler_params=_params(1),
        name="rglru",
    )(u, conv_w, conv_b, wa, wx, ba, bx, lam, h0)


def _cv_kernel(zp_ref, zc_ref, zn_ref, w_ref, db_ref, lg_ref, lb_ref, o_ref, ext_ref, acc_ref,
               *, n_sample_chunks, ctx_unit_chunks, kc):
    i = pl.program_id(0)
    rows, c = zc_ref.shape
    hr = CONV_HALO_TILES * SUBLANES
    group = SUBLANES
    is_ctx = i >= n_sample_chunks
    cpos = (i - n_sample_chunks) % ctx_unit_chunks
    lo_ok = is_ctx & (cpos > 0)
    hi_ok = is_ctx & (cpos < ctx_unit_chunks - 1)
    ext_ref[0:hr, :] = jnp.where(lo_ok, zp_ref[rows - hr:rows, :], 0.0)
    ext_ref[hr:hr + rows, :] = zc_ref[...]
    ext_ref[hr + rows:hr + rows + hr, :] = jnp.where(hi_ok, zn_ref[0:hr, :], 0.0)
    first_tap_tile = CONV_HALO_TILES - kc // 2

    def lane_body(lb, carry):
        lanes = pl.ds(pl.multiple_of(lb * LANES, LANES), LANES)
        taps = [w_ref[k, :, lanes] for k in range(kc)]

        def grp_body(g, carry):
            accs = [None] * group
            for e in range(group + kc - 1):
                src = pl.multiple_of((g * group + first_tap_tile + e) * SUBLANES, SUBLANES)
                tile = ext_ref[pl.ds(src, SUBLANES), lanes]
                for t in range(group):
                    k = e - t
                    if 0 <= k < kc:
                        term = taps[k] * tile
                        accs[t] = term if accs[t] is None else accs[t] + term
            for t in range(group):
                dst = pl.multiple_of((g * group + t) * SUBLANES, SUBLANES)
                acc_ref[pl.ds(dst, SUBLANES), lanes] = accs[t]
            return carry

        return lax.fori_loop(0, rows // (group * SUBLANES), grp_body, carry)

    lax.fori_loop(0, c // LANES, lane_body, 0)
    z = acc_ref[...] + db_ref[...]
    mu = jnp.mean(z, axis=-1, keepdims=True)
    zc = z - mu
    var = jnp.mean(zc * zc, axis=-1, keepdims=True)
    y = zc * lax.rsqrt(var + EPS) * lg_ref[...] + lb_ref[...]
    o_ref[...] = _silu(y).astype(o_ref.dtype)


def _conv_branch(z, dw_tiles, db, ln_g, ln_b, *, n_sample_chunks, ctx_unit_chunks):
    t, c = z.shape
    kc = dw_tiles.shape[0]
    rows = CHUNK_TILES * SUBLANES
    n = t // rows
    kern = functools.partial(_cv_kernel, n_sample_chunks=n_sample_chunks,
                             ctx_unit_chunks=ctx_unit_chunks, kc=kc)
    vec = pl.BlockSpec((1, c), lambda i: (0, 0))
    return pl.pallas_call(
        kern,
        grid=(n,),
        in_specs=[
            pl.BlockSpec((rows, c), lambda i: (jnp.maximum(i - 1, 0), 0)),
            pl.BlockSpec((rows, c), lambda i: (i, 0)),
            pl.BlockSpec((rows, c), lambda i: (jnp.minimum(i + 1, n - 1), 0)),
            pl.BlockSpec((kc, SUBLANES, c), lambda i: (0, 0, 0)),
            vec, vec, vec,
        ],
        out_specs=pl.BlockSpec((rows, c), lambda i: (i, 0)),
        out_shape=jax.ShapeDtypeStruct((t, c), MXU_DTYPE),
        scratch_shapes=[
            pltpu.VMEM((rows + 2 * CONV_HALO_TILES * SUBLANES, c), F32),
            pltpu.VMEM((rows, c), F32),
        ],
        compiler_params=_params(1),
        name="conv",
    )(z, z, z, dw_tiles, db, ln_g, ln_b)


def _s5_kernel(u_ref, dsk_ref, are_ref, aim_ref, bm_ref, cm_ref, h0_ref,
               y_ref, fin_ref, lhs_ref, hs_ref, yc_ref, *, units, tc, grid_rows):
    ns = are_ref.shape[-1]

    def tile_of(unit, c, s):
        t0, _, is_grid, _ = unit
        if is_grid:
            return t0 + (s % grid_rows) * GRID_W + c * (tc // grid_rows) + s // grid_rows
        return t0 + c * tc + s

    for d in (0, 1):
        a_re = jnp.broadcast_to(are_ref[d, 0], (SUBLANES, ns))
        a_im = jnp.broadcast_to(aim_ref[d, 0], (SUBLANES, ns))
        for ui, unit in enumerate(units):
            nch = unit[1] // tc

            def chunk_body(ci, h, d=d, unit=unit, nch=nch, a_re=a_re, a_im=a_im):
                c = ci if d == 0 else nch - 1 - ci
                for s in range(tc):
                    r = pl.multiple_of(tile_of(unit, c, s) * SUBLANES, SUBLANES)
                    lhs_ref[s * SUBLANES:(s + 1) * SUBLANES, :] = u_ref[pl.ds(r, SUBLANES), :]
                hs_ref[...] = _mm(lhs_ref[...], bm_ref[d, 0])

                group = SUBLANES

                def steps(gi, h):
                    g = gi if d == 0 else tc // group - 1 - gi
                    base = pl.multiple_of(g * group * SUBLANES, group * SUBLANES)
                    h_re, h_im = h
                    for k in range(group):
                        o = pl.ds(base + (k if d == 0 else group - 1 - k) * SUBLANES, SUBLANES)
                        n_re = a_re * h_re - a_im * h_im + hs_ref[o, 0:ns]
                        n_im = a_re * h_im + a_im * h_re + hs_ref[o, ns:2 * ns]
                        hs_ref[o, 0:ns] = n_re
                        hs_ref[o, ns:2 * ns] = n_im
                        h_re, h_im = n_re, n_im
                    return h_re, h_im

                h = lax.fori_loop(0, tc // group, steps, h)
                yc_ref[...] = _mm(hs_ref[...], cm_ref[d, 0])
                for s in range(tc):
                    r = pl.multiple_of(tile_of(unit, c, s) * SUBLANES, SUBLANES)
                    sl = slice(s * SUBLANES, (s + 1) * SUBLANES)
                    if d == 0:
                        y_ref[pl.ds(r, SUBLANES), :] = dsk_ref[...] * lhs_ref[sl, :] + yc_ref[sl, :]
                    else:
                        y_ref[pl.ds(r, SUBLANES), :] = y_ref[pl.ds(r, SUBLANES), :] + yc_ref[sl, :]
                return h

            if unit[3]:
                h_init = (h0_ref[d, 0, :, 0:ns], h0_ref[d, 0, :, ns:2 * ns])
            else:
                h_init = (jnp.zeros((SUBLANES, ns), F32), jnp.zeros((SUBLANES, ns), F32))
            f_re, f_im = lax.fori_loop(0, nch, chunk_body, h_init)
            fin_ref[ui, d, 0, :, 0:ns] = f_re
            fin_ref[ui, d, 0, :, ns:2 * ns] = f_im


def _s5(u, dskip, a_re, a_im, bmat, cmat, h0, *, units, tc, grid_rows):
    t, width = u.shape
    nb = width // LANES
    ns = a_re.shape[-1]
    n_units = len(units)
    kern = functools.partial(_s5_kernel, units=units, tc=tc, grid_rows=grid_rows)
    blk4 = lambda *shape: pl.BlockSpec((2, 1) + shape, lambda j: (0, j, 0, 0))
    return pl.pallas_call(
        kern,
        grid=(nb,),
        in_specs=[
            pl.BlockSpec((t, LANES), lambda j: (0, j)),
            pl.BlockSpec((1, LANES), lambda j: (0, j)),
            blk4(1, ns), blk4(1, ns),
            blk4(LANES, 2 * ns), blk4(2 * ns, LANES),
            blk4(SUBLANES, 2 * ns),
        ],
        out_specs=[
            pl.BlockSpec((t, LANES), lambda j: (0, j)),
            pl.BlockSpec((n_units, 2, 1, SUBLANES, 2 * ns), lambda j: (0, 0, j, 0, 0)),
        ],
        out_shape=[
            jax.ShapeDtypeStruct((t, width), F32),
            jax.ShapeDtypeStruct((n_units, 2, nb, SUBLANES, 2 * ns), F32),
        ],
        scratch_shapes=[
            pltpu.VMEM((tc * SUBLANES, LANES), F32),
            pltpu.VMEM((tc * SUBLANES, 2 * ns), F32),
            pltpu.VMEM((tc * SUBLANES, LANES), F32),
        ],
        compiler_params=_params(1),
        name="s5",
    )(u, dskip, a_re, a_im, bmat, cmat, h0)


def _merge_kernel(yrg_ref, zc_ref, ys5_ref, g0_ref, g1_ref, g2_ref, wrg_ref, wcv_ref, wsv_ref,
                  wsg_ref, wo_ref, x_ref, gate_ref, o_ref, acc_ref, rgb_ref, s5b_ref):
    j = pl.program_id(1)

    @pl.when(j == 0)
    def _():
        acc_ref[...] = jnp.zeros_like(acc_ref)
        rgb_ref[...] = yrg_ref[...].astype(rgb_ref.dtype)
        s5b_ref[...] = ys5_ref[...].astype(s5b_ref.dtype)

    dot = lambda a, b: jnp.dot(a, b, preferred_element_type=F32)
    br_rg = dot(rgb_ref[...], wrg_ref[...])
    br_cv = dot(zc_ref[...], wcv_ref[...])
    s5b = s5b_ref[...]
    br_s5 = dot(s5b, wsv_ref[...]) * _sigmoid(dot(s5b, wsg_ref[...]))
    merged = (g0_ref[...].astype(F32) * br_rg + g1_ref[...].astype(F32) * br_cv
              + g2_ref[...].astype(F32) * br_s5)
    acc_ref[...] += dot(merged.astype(MXU_DTYPE), wo_ref[...])

    @pl.when(j == pl.num_programs(1) - 1)
    def _():
        o_ref[...] = _gated_residual(x_ref[...], gate_ref[0], acc_ref[...])


def _merge(y_rg, zc, y_s5, gates, rg_out, cv_out, s5_glu, w_out, x, gate, *, n_sample_rows, tm, tk):
    t, d = x.shape
    rg, cv, s5 = y_rg.shape[1], zc.shape[1], y_s5.shape[1]
    nk = d // tk
    ns = n_sample_rows // tm
    row = lambda w: pl.BlockSpec((tm, w), lambda i, j: (i, 0))
    gspec = lambda k: pl.BlockSpec((tm, tk), lambda i, j: (i, k * nk + j))
    wcol = lambda rows, off: pl.BlockSpec((rows, tk), lambda i, j: (0, off + j))
    return pl.pallas_call(
        _merge_kernel,
        grid=(t // tm, nk),
        in_specs=[
            row(rg), row(cv), row(s5),
            gspec(0), gspec(1), gspec(2),
            wcol(rg, 0), wcol(cv, 0), wcol(s5, 0), wcol(s5, nk),
            pl.BlockSpec((tk, d), lambda i, j: (j, 0)),
            row(d),
            pl.BlockSpec((1, SUBLANES, d), lambda i, j: ((i >= ns).astype(jnp.int32), 0, 0)),
        ],
        out_specs=row(d),
        out_shape=jax.ShapeDtypeStruct((t, d), F32),
        scratch_shapes=[
            pltpu.VMEM((tm, d), F32),
            pltpu.VMEM((tm, rg), MXU_DTYPE),
            pltpu.VMEM((tm, s5), MXU_DTYPE),
        ],
        compiler_params=_params(2),
        name="merge",
    )(y_rg, zc, y_s5, gates, gates, gates, rg_out, cv_out, s5_glu, s5_glu, w_out, x, gate)


def _pack_bf16_pairs(v):
    half = v.shape[1] // 2
    bits = lambda a: lax.bitcast_convert_type(a.astype(jnp.bfloat16).astype(F32), jnp.uint32)
    return (bits(v[:, :half]) >> 16) | (bits(v[:, half:]) & jnp.uint32(0xFFFF0000))


def _store_token_slabs(ref, packed):
    n, width = packed.shape
    r = width // LANES
    for s in range(r):
        ref[pl.ds(s, n, stride=r), :] = packed[:, s * LANES:(s + 1) * LANES]


def _load_token_slabs(ref, first_row, n, r):
    return jnp.concatenate(
        [ref[pl.ds(first_row + s, n, stride=r), :] for s in range(r)], axis=1)


def _unpack_bf16_pairs(p):
    lo = lax.bitcast_convert_type(p << 16, F32)
    hi = lax.bitcast_convert_type(p & jnp.uint32(0xFFFF0000), F32)
    return lo, hi


def _router_kernel(x_ref, g_ref, sc_ref, sh_ref, rw_ref, rb_ref,
                   hnp_ref, eidx_ref, rank_ref, wtok_ref, cnt_ref, run_ref):
    @pl.when(pl.program_id(0) == 0)
    def _():
        run_ref[...] = jnp.zeros_like(run_ref)

    hn = _norm_mod(x_ref[...], g_ref[...], sc_ref[0], sh_ref[0])
    _store_token_slabs(hnp_ref, _pack_bf16_pairs(hn))
    logits = lax.dot_general(rw_ref[...], hn, (((1,), (1,)), ((), ())),
                             precision=lax.Precision.HIGHEST, preferred_element_type=F32)
    s = _sigmoid(logits)
    choice = s + rb_ref[...]
    n_exp, tm = choice.shape
    gsize = n_exp // N_ROUTE_GROUPS
    neg_inf = jnp.float32(-jnp.inf)
    c3 = choice.reshape(N_ROUTE_GROUPS, gsize, tm)
    sub = lax.broadcasted_iota(jnp.int32, c3.shape, 1)
    m1 = jnp.max(c3, axis=1, keepdims=True)
    i1 = jnp.min(jnp.where(c3 == m1, sub, gsize), axis=1, keepdims=True)
    m2 = jnp.max(jnp.where(sub == i1, neg_inf, c3), axis=1, keepdims=True)
    gscore = jnp.broadcast_to(m1 + m2, c3.shape)
    gidx = lax.broadcasted_iota(jnp.int32, c3.shape, 0)
    beaten = jnp.zeros(c3.shape, jnp.int32)
    for gp in range(N_ROUTE_GROUPS):
        other = gscore[gp:gp + 1]
        wins = (other > gscore) | ((other == gscore) & (gidx > gp))
        beaten = beaten + wins.astype(jnp.int32)
    masked = jnp.where(beaten < TOPK_GROUPS, c3, neg_inf).reshape(n_exp, tm)
    eidx = lax.broadcasted_iota(jnp.int32, masked.shape, 0)
    beaten = jnp.zeros(masked.shape, jnp.int32)
    for ep in range(n_exp):
        other = masked[ep:ep + 1, :]
        wins = (other > masked) | ((other == masked) & (eidx > ep))
        beaten = beaten + wins.astype(jnp.int32)
    sel = beaten < TOP_K
    w = jnp.where(sel, s, 0.0)
    comb = ROUTE_SCALE * w / jnp.sum(w, axis=0, keepdims=True)
    self32 = sel.astype(F32)
    before = (lax.broadcasted_iota(jnp.int32, (tm, tm), 0) < lax.broadcasted_iota(jnp.int32, (tm, tm), 1))
    rank = run_ref[...] + _mm(self32, before.astype(F32))
    run_ref[...] = run_ref[...] + jnp.sum(self32, axis=1, keepdims=True)
    cnt_ref[...] = jnp.broadcast_to(run_ref[...], cnt_ref.shape)
    eidx_f = eidx.astype(F32)
    w_rows = []
    for j in range(TOP_K):
        m = beaten == j
        pick = lambda v: jnp.sum(jnp.where(m, v, 0.0), axis=0, keepdims=True)
        eidx_ref[j:j + 1, :] = pick(eidx_f).astype(jnp.int32)
        rank_ref[j:j + 1, :] = pick(rank).astype(jnp.int32)
        w_rows.append(pick(comb))
    w_rows.append(jnp.zeros((wtok_ref.shape[1] - TOP_K, tm), F32))
    wtok_ref[...] = jnp.concatenate(w_rows, axis=0).T


def _router(x, g, scale, shift, router_wt, router_b, *, n_sample_rows, tm):
    t, d = x.shape
    n_exp = router_wt.shape[0]
    ns = n_sample_rows // tm
    unit = lambda i: ((i >= ns).astype(jnp.int32), 0, 0)
    rpt = d // 2 // LANES
    return pl.pallas_call(
        _router_kernel,
        grid=(t // tm,),
        in_specs=[
            pl.BlockSpec((tm, d), lambda i: (i, 0)),
            pl.BlockSpec((1, d), lambda i: (0, 0)),
            pl.BlockSpec((1, SUBLANES, d), unit),
            pl.BlockSpec((1, SUBLANES, d), unit),
            pl.BlockSpec((n_exp, d), lambda i: (0, 0)),
            pl.BlockSpec((n_exp, 1), lambda i: (0, 0)),
        ],
        out_specs=[
            pl.BlockSpec((tm * rpt, LANES), lambda i: (i, 0)),
            pl.BlockSpec((TOP_K, tm), lambda i: (0, i)),
            pl.BlockSpec((TOP_K, tm), lambda i: (0, i)),
            pl.BlockSpec((tm, LANES), lambda i: (i, 0)),
            pl.BlockSpec((n_exp, LANES), lambda i: (0, 0)),
        ],
        out_shape=[
            jax.ShapeDtypeStruct((t * rpt, LANES), jnp.uint32),
            jax.ShapeDtypeStruct((TOP_K, t), jnp.int32),
            jax.ShapeDtypeStruct((TOP_K, t), jnp.int32),
            jax.ShapeDtypeStruct((t, LANES), F32),
            jax.ShapeDtypeStruct((n_exp, LANES), F32),
        ],
        scratch_shapes=[pltpu.VMEM((n_exp, 1), F32)],
        compiler_params=_params(1),
        name="router",
    )(x, g, scale, shift, router_wt, router_b)


def _swiglu(lo, hi, w1, w3, w2):
    half = lo.shape[1]
    lo = lo.astype(MXU_DTYPE)
    hi = hi.astype(MXU_DTYPE)
    dot = lambda a, b: jnp.dot(a, b, preferred_element_type=F32)
    h1 = dot(lo, w1[0:half, :]) + dot(hi, w1[half:, :])
    h3 = dot(lo, w3[0:half, :]) + dot(hi, w3[half:, :])
    return dot((_silu(h1) * h3).astype(MXU_DTYPE), w2[...])


def _dispatch_kernel(pos_ref, hn_ref, xs_in_hbm, xs_hbm, sem):
    del xs_in_hbm
    n_slots, tm = pos_ref.shape
    rpt = hn_ref.shape[0] // tm

    def slab(k):
        return pl.ds(pl.multiple_of(k * rpt, rpt), rpt)

    def issue(t, carry):
        src = hn_ref.at[slab(t)]
        for j in range(n_slots):
            pltpu.make_async_copy(src, xs_hbm.at[slab(pos_ref[j, t])], sem).start(priority=j % 2)
        return carry

    lax.fori_loop(0, tm, issue, 0, unroll=4)
    for j in range(n_slots):
        pltpu.make_async_copy(hn_ref, xs_hbm.at[pl.ds(0, tm * rpt)], sem).wait()


def _dispatch(pos, hnp, xs_init, *, tm):
    n_slots, t = pos.shape
    rpt = hnp.shape[0] // t
    return pl.pallas_call(
        _dispatch_kernel,
        grid=(t // tm,),
        in_specs=[
            pl.BlockSpec((n_slots, tm), lambda i: (0, i), memory_space=pltpu.SMEM),
            pl.BlockSpec((tm * rpt, LANES), lambda i: (i, 0)),
            pl.BlockSpec(memory_space=pl.ANY),
        ],
        out_specs=pl.BlockSpec(memory_space=pl.ANY),
        out_shape=jax.ShapeDtypeStruct(xs_init.shape, xs_init.dtype),
        scratch_shapes=[pltpu.SemaphoreType.DMA],
        input_output_aliases={2: 0},
        compiler_params=_params(1),
        name="dispatch",
    )(pos, hnp, xs_init)


def _ffn_kernel(te_ref, nu_ref, xs_ref, w1_ref, w3_ref, w2_ref, ys_ref, w1b_ref, w3b_ref, w2b_ref):
    i = pl.program_id(0)

    @pl.when((i == 0) | (te_ref[i] != te_ref[jnp.maximum(i - 1, 0)]))
    def _():
        w1b_ref[...] = w1_ref[0, 0].astype(w1b_ref.dtype)
        w3b_ref[...] = w3_ref[0, 0].astype(w3b_ref.dtype)
        w2b_ref[...] = w2_ref[0, 0].astype(w2b_ref.dtype)

    @pl.when(i < nu_ref[0])
    def _():
        rpt = w1b_ref.shape[0] // 2 // LANES
        lo, hi = _unpack_bf16_pairs(_load_token_slabs(xs_ref, 0, xs_ref.shape[0] // rpt, rpt))
        _store_token_slabs(ys_ref, _pack_bf16_pairs(_swiglu(lo, hi, w1b_ref, w3b_ref, w2b_ref)))

    @pl.when(i >= nu_ref[0])
    def _():
        ys_ref[...] = jnp.zeros_like(ys_ref)


def _ffn(tile_expert, n_used, xs, w1, w3, w2, *, layer, tg):
    _, _, d, f = w1.shape
    rpt = d // 2 // LANES
    n_tiles = xs.shape[0] // (tg * rpt)
    row = lambda i, te, nu: (jnp.maximum(jnp.minimum(i, nu[0] - 1), 0), 0)
    wblk = lambda i, te, nu: (layer, te[i], 0, 0)
    grid_spec = pltpu.PrefetchScalarGridSpec(
        num_scalar_prefetch=2,
        grid=(n_tiles,),
        in_specs=[
            pl.BlockSpec((tg * rpt, LANES), row),
            pl.BlockSpec((1, 1, d, f), wblk),
            pl.BlockSpec((1, 1, d, f), wblk),
            pl.BlockSpec((1, 1, f, d), wblk),
        ],
        out_specs=pl.BlockSpec((tg * rpt, LANES), lambda i, te, nu: (i, 0)),
        scratch_shapes=[
            pltpu.VMEM((d, f), MXU_DTYPE),
            pltpu.VMEM((d, f), MXU_DTYPE),
            pltpu.VMEM((f, d), MXU_DTYPE),
        ],
    )
    return pl.pallas_call(
        _ffn_kernel,
        grid_spec=grid_spec,
        out_shape=jax.ShapeDtypeStruct(xs.shape, jnp.uint32),
        compiler_params=_params(1),
        name="ffn",
    )(tile_expert, n_used, xs, w1, w3, w2)


def _combine_kernel(pos_ref, hnp_ref, wtok_ref, s1_ref, s3_ref, s2_ref, x_ref, gate_ref, ys_hbm,
                    o_ref, buf_ref, sem):
    n_slots, tm = pos_ref.shape
    rpt = hnp_ref.shape[0] // tm
    slab_rows = tm * rpt

    def slab(k):
        return pl.ds(pl.multiple_of(k * rpt, rpt), rpt)

    def issue(t, carry):
        for j in range(n_slots):
            pltpu.make_async_copy(ys_hbm.at[slab(pos_ref[j, t])],
                                  buf_ref.at[slab(j * tm + t)], sem).start(priority=j % 2)
        return carry

    lax.fori_loop(0, tm, issue, 0, unroll=4)
    lo, hi = _unpack_bf16_pairs(_load_token_slabs(hnp_ref, 0, tm, rpt))
    moe = _swiglu(lo, hi, s1_ref, s3_ref, s2_ref)
    half = lo.shape[1]
    acc_lo = moe[:, :half]
    acc_hi = moe[:, half:]
    wtok = wtok_ref[...]
    for j in range(n_slots):
        pltpu.make_async_copy(ys_hbm.at[pl.ds(0, slab_rows)],
                              buf_ref.at[pl.ds(j * slab_rows, slab_rows)], sem).wait()
    for j in range(n_slots):
        y_lo, y_hi = _unpack_bf16_pairs(_load_token_slabs(buf_ref, j * slab_rows, tm, rpt))
        wj = wtok[:, j:j + 1]
        acc_lo = acc_lo + wj * y_lo
        acc_hi = acc_hi + wj * y_hi
    x = x_ref[...]
    gate = gate_ref[0]
    o_ref[:, :half] = _gated_residual(x[:, :half], gate[:, :half], acc_lo)
    o_ref[:, half:] = _gated_residual(x[:, half:], gate[:, half:], acc_hi)


def _combine(pos, hnp, wtok, s1, s3, s2, x, gate, ys, *, n_sample_rows, tm):
    t, d = x.shape
    n_slots = pos.shape[0]
    f = s1.shape[1]
    ns = n_sample_rows // tm
    rpt = d // 2 // LANES
    full = lambda shape: pl.BlockSpec(shape, lambda i: (0,) * len(shape), pipeline_mode=pl.Buffered(1))
    return pl.pallas_call(
        _combine_kernel,
        grid=(t // tm,),
        in_specs=[
            pl.BlockSpec((n_slots, tm), lambda i: (0, i), memory_space=pltpu.SMEM),
            pl.BlockSpec((tm * rpt, LANES), lambda i: (i, 0)),
            pl.BlockSpec((tm, LANES), lambda i: (i, 0)),
            full((d, f)), full((d, f)), full((f, d)),
            pl.BlockSpec((tm, d), lambda i: (i, 0)),
            pl.BlockSpec((1, SUBLANES, d), lambda i: ((i >= ns).astype(jnp.int32), 0, 0)),
            pl.BlockSpec(memory_space=pl.ANY),
        ],
        out_specs=pl.BlockSpec((tm, d), lambda i: (i, 0)),
        out_shape=jax.ShapeDtypeStruct((t, d), F32),
        scratch_shapes=[
            pltpu.VMEM((n_slots * tm * rpt, LANES), jnp.uint32),
            pltpu.SemaphoreType.DMA,
        ],
        compiler_params=_params(1),
        name="combine",
    )(pos, hnp, wtok, s1, s3, s2, x, gate, ys)


def _slots_kernel(cnt_ref, eidx_ref, rank_ref, pos_ref, te_ref, nu_ref, *, tg):
    counts = cnt_ref[...]
    n_exp = counts.shape[0]
    padded = jnp.floor((counts + (tg - 1)) * (1.0 / tg)) * tg
    incl = (lax.broadcasted_iota(jnp.int32, (n_exp, n_exp), 0)
            >= lax.broadcasted_iota(jnp.int32, (n_exp, n_exp), 1)).astype(F32)
    ends = jnp.dot(incl, padded, precision=lax.Precision.HIGHEST, preferred_element_type=F32)
    ends_col = ends[:, 0:1]
    offs_col = ends_col - padded[:, 0:1]
    eidx = eidx_ref[...]
    rank = rank_ref[...]
    expert = lax.broadcasted_iota(jnp.int32, (n_exp, eidx.shape[1]), 0)
    for j in range(eidx.shape[0]):
        off = jnp.sum(jnp.where(expert == eidx[j:j + 1, :], offs_col, 0.0), axis=0, keepdims=True)
        pos_ref[j:j + 1, :] = off.astype(jnp.int32) + rank[j:j + 1, :]
    n_used = ends[n_exp - 1:n_exp, 0:1] * (1.0 / tg)
    tile = lax.broadcasted_iota(jnp.int32, te_ref.shape, 1).astype(F32)
    tile_row = jnp.minimum(tile, n_used - 1.0) * tg
    te_ref[...] = jnp.sum((ends_col <= tile_row).astype(F32), axis=0, keepdims=True).astype(jnp.int32)
    nu_ref[...] = jnp.broadcast_to(n_used, nu_ref.shape).astype(jnp.int32)


def _slots(counts, eidx, rank, *, tg, n_tiles, tb):
    assert tg & (tg - 1) == 0
    n_slots, t = eidx.shape
    n_tiles_pad = pl.cdiv(n_tiles, LANES) * LANES
    pos, te, nu = pl.pallas_call(
        functools.partial(_slots_kernel, tg=tg),
        grid=(t // tb,),
        in_specs=[
            pl.BlockSpec(counts.shape, lambda i: (0, 0)),
            pl.BlockSpec((n_slots, tb), lambda i: (0, i)),
            pl.BlockSpec((n_slots, tb), lambda i: (0, i)),
        ],
        out_specs=[
            pl.BlockSpec((n_slots, tb), lambda i: (0, i)),
            pl.BlockSpec((1, n_tiles_pad), lambda i: (0, 0)),
            pl.BlockSpec((1, LANES), lambda i: (0, 0)),
        ],
        out_shape=[
            jax.ShapeDtypeStruct((n_slots, t), jnp.int32),
            jax.ShapeDtypeStruct((1, n_tiles_pad), jnp.int32),
            jax.ShapeDtypeStruct((1, LANES), jnp.int32),
        ],
        compiler_params=_params(1),
        name="slots",
    )(counts, eidx, rank)
    return pos, te[0, :n_tiles], nu[0, :1]


def _final_norm_kernel(x_ref, g_ref, os_ref, oc_ref, y_ref, *, n_sample_tiles):
    x = x_ref[...]
    ms = jnp.mean(x * x, axis=-1, keepdims=True)
    y = x * lax.rsqrt(ms + EPS) * g_ref[...]
    steps = x.shape[0] // SUBLANES
    n_lane_blocks = x.shape[1] // LANES
    for k in range(n_lane_blocks):
        y_ref[k] = y[:, k * LANES:(k + 1) * LANES]

    def emit(o_ref):
        for b in range(SUBLANES):
            for k in range(n_lane_blocks):
                o_ref[b, :, k * LANES:(k + 1) * LANES] = y_ref[k, pl.ds(b, steps, stride=SUBLANES), :]

    @pl.when(pl.program_id(0) < n_sample_tiles)
    def _():
        emit(os_ref)

    @pl.when(pl.program_id(0) >= n_sample_tiles)
    def _():
        emit(oc_ref)


def _final_norm(x, g, *, ls, lc, n_ctx_units, tm):
    t, d = x.shape
    steps = tm // SUBLANES
    nst = ls // steps
    ctx_tiles = lc // steps
    n_ctx = n_ctx_units * ctx_tiles

    def ctx_block(i):
        k = jnp.clip(i - nst, 0, n_ctx - 1)
        return (k // ctx_tiles, k % ctx_tiles, 0)

    return pl.pallas_call(
        functools.partial(_final_norm_kernel, n_sample_tiles=nst),
        grid=(t // tm,),
        in_specs=[pl.BlockSpec((tm, d), lambda i: (i, 0)), pl.BlockSpec((1, d), lambda i: (0, 0))],
        out_specs=[
            pl.BlockSpec((SUBLANES, steps, d), lambda i: (0, jnp.minimum(i, nst - 1), 0)),
            pl.BlockSpec((SUBLANES, steps, d), ctx_block),
        ],
        out_shape=[
            jax.ShapeDtypeStruct((SUBLANES, ls, d), F32),
            jax.ShapeDtypeStruct((n_ctx_units * SUBLANES, lc, d), F32),
        ],
        scratch_shapes=[pltpu.VMEM((d // LANES, tm, LANES), F32)],
        compiler_params=_params(1),
        name="final_norm",
    )(x, g)


def _s5_discretise(lam_re, lam_im, log_dt, b_re, b_im, c_re, c_im):
    two, g, p = lam_re.shape
    s = b_re.shape[-1]
    gb = LANES // s
    nb = g // gb
    dt = jnp.exp(log_dt)[..., None]
    mag = jnp.exp(lam_re * dt)
    a_re = mag * jnp.cos(lam_im * dt)
    a_im = mag * jnp.sin(lam_im * dt)
    den = lam_re * lam_re + lam_im * lam_im
    q_re = ((a_re - 1.0) * lam_re + a_im * lam_im) / den
    q_im = (a_im * lam_re - (a_re - 1.0) * lam_im) / den
    bb_re = q_re[..., None] * b_re - q_im[..., None] * b_im
    bb_im = q_re[..., None] * b_im + q_im[..., None] * b_re
    eye = jnp.eye(gb, dtype=F32)

    def bdiag_b(m):
        m = m.reshape(two, nb, gb, p, s)
        return jnp.einsum("dbgps,gh->dbgshp", m, eye).reshape(two, nb, gb * s, gb * p)

    def bdiag_c(m):
        m = m.reshape(two, nb, gb, s, p)
        return jnp.einsum("dbgsp,gh->dbgphs", m, eye).reshape(two, nb, gb * p, gb * s)

    bmat = jnp.concatenate([bdiag_b(bb_re), bdiag_b(bb_im)], axis=-1)
    cmat = jnp.concatenate([bdiag_c(c_re), -bdiag_c(c_im)], axis=-2)
    blk = lambda a: a.reshape(two, nb, 1, gb * p)
    return blk(a_re), blk(a_im), bmat.astype(MXU_DTYPE), cmat.astype(MXU_DTYPE)


def kernel(x_prompt, x_sample, state_rglru, state_s5, c, c_ctx, norm_mix, norm_ffn, ada_w, ada_b, w_in, b_gate, rg_conv_w, rg_conv_b, rg_wa, rg_ba, rg_wx, rg_bx, rg_lambda, rg_out, cv_dw, cv_db, cv_ln_g, cv_ln_b, cv_out, s5_lambda_re, s5_lambda_im, s5_log_dt, s5_b_re, s5_b_im, s5_c_re, s5_c_im, s5_d, s5_glu, w_out, router_w, router_b, exp_w1, exp_w3, exp_w2, sh_w1, sh_w3, sh_w2, norm_final):
    batch, lc, d = x_prompt.shape
    dec_batch, ls, _ = x_sample.shape
    depth = w_in.shape[0]
    rg = rg_out.shape[1]
    cv = cv_out.shape[1]
    s5 = s5_glu.shape[1]
    n_groups_s5, n_state = s5_lambda_re.shape[2], s5_lambda_re.shape[3]
    assert dec_batch == SUBLANES and batch % SUBLANES == 0
    assert ls % GRID_W == 0 and lc % CHUNK_TILES == 0
    assert rg_wa.shape[-1] == LANES
    n_ctx_units = batch // SUBLANES
    grid_rows = ls // GRID_W
    n_sample_rows = ls * SUBLANES
    units = ((0, ls, True, True),) + tuple(
        (ls + k * lc, lc, False, False) for k in range(n_ctx_units))
    tm = 512
    tn = min(1024, rg, cv, s5)
    tk = min(512, d)
    tc = min(128, lc)
    tg = 512
    tmc = 512
    n_tokens = (ls + n_ctx_units * lc) * SUBLANES
    n_exp = router_w.shape[-1]
    assert tc % grid_rows == 0 and (GRID_W * grid_rows) % tc == 0
    assert n_sample_rows % tm == 0 and (lc * SUBLANES) % tm == 0 and n_sample_rows % tmc == 0
    assert all(w % tn == 0 for w in (rg, cv, s5, b_gate.shape[-1])) and d % tk == 0
    n_ffn_tiles = pl.cdiv(n_tokens * TOP_K, tg) + n_exp

    xs = jnp.transpose(x_sample, (1, 0, 2)).reshape(ls * SUBLANES, d)
    xc = jnp.transpose(x_prompt.reshape(n_ctx_units, SUBLANES, lc, d), (0, 2, 1, 3))
    x = jnp.concatenate([xs, xc.reshape(n_ctx_units * lc * SUBLANES, d)], axis=0)

    cond = jnp.concatenate([c, c_ctx[None], jnp.zeros((SUBLANES - 1, d), F32)], axis=0)
    mod = _ada(cond, ada_w, ada_b)
    mod = jnp.stack([mod[:, :SUBLANES],
                     jnp.broadcast_to(mod[:, SUBLANES:SUBLANES + 1], (depth, SUBLANES, 6 * d))], axis=1)
    mod = mod.reshape(depth, 2, SUBLANES, 6, d)

    cast = lambda a: a.astype(MXU_DTYPE)
    sorted_rows = jnp.zeros((n_ffn_tiles * tg * (d // 2 // LANES), LANES), jnp.uint32)
    rg_fin, s5_fin = [], []
    for l in range(depth):
        shift1, scale1, gate1, shift2, scale2, gate2 = (mod[l, :, :, k] for k in range(6))
        u_rg, z_cv, u_s5, gates = _win(
            x, norm_mix[l][None], scale1, shift1, cast(w_in[l]), b_gate[l][None],
            rg=rg, cv=cv, s5=s5, n_sample_rows=n_sample_rows, tm=tm, tn=tn)

        h0_rg = jnp.transpose(state_rglru[:, l], (1, 0, 2))
        heads = rg // LANES
        y_rg, fin_rg = _rglru(
            u_rg, rg_conv_w[l], rg_conv_b[l][None],
            cast(rg_wa[l]), cast(rg_wx[l]), rg_ba[l], rg_bx[l], rg_lambda[l], h0_rg, units=units)
        rg_fin.append(fin_rg)

        dw_tiles = jnp.broadcast_to(cv_dw[l][:, None, :], (cv_dw.shape[1], SUBLANES, cv))
        zc = _conv_branch(z_cv, dw_tiles, cv_db[l][None], cv_ln_g[l][None], cv_ln_b[l][None],
                          n_sample_chunks=ls // CHUNK_TILES, ctx_unit_chunks=lc // CHUNK_TILES)

        a_re, a_im, bmat, cmat = _s5_discretise(
            s5_lambda_re[l], s5_lambda_im[l], s5_log_dt[l], s5_b_re[l], s5_b_im[l],
            s5_c_re[l], s5_c_im[l])
        nb = s5 // LANES
        st = jnp.transpose(state_s5[:, l], (1, 4, 0, 2, 3))
        st = st.reshape(2, 2, SUBLANES, nb, (n_groups_s5 // nb) * n_state)
        h0_s5 = jnp.transpose(st, (0, 3, 2, 1, 4)).reshape(2, nb, SUBLANES, -1)
        y_s5, fin_s5 = _s5(u_s5, s5_d[l][None], a_re, a_im, bmat, cmat, h0_s5,
                           units=units, tc=tc, grid_rows=grid_rows)
        s5_fin.append(fin_s5)

        x = _merge(y_rg, zc, y_s5, gates, cast(rg_out[l]), cast(cv_out[l]), cast(s5_glu[l]),
                   cast(w_out[l]), x, gate1, n_sample_rows=n_sample_rows, tm=tm, tk=tk)

        hnp, eidx, rank, wtok, counts = _router(
            x, norm_ffn[l][None], scale2, shift2, router_w[l].T, router_b[l][:, None],
            n_sample_rows=n_sample_rows, tm=tm)
        pos, tile_expert, n_used = _slots(counts, eidx, rank, tg=tg, n_tiles=n_ffn_tiles, tb=tm)
        sorted_rows = _dispatch(pos, hnp, sorted_rows, tm=tm)
        ys = _ffn(tile_expert, n_used, sorted_rows, exp_w1, exp_w3, exp_w2, layer=l, tg=tg)
        x = _combine(pos, hnp, wtok, cast(sh_w1[l]), cast(sh_w3[l]), cast(sh_w2[l]), x, gate2, ys,
                     n_sample_rows=n_sample_rows, tm=tmc)

    y_sample, y_prompt = _final_norm(x, norm_final[None], ls=ls, lc=lc, n_ctx_units=n_ctx_units, tm=tm)

    fr = jnp.stack(rg_fin, axis=0)[:, 1:]
    new_state_rglru = jnp.transpose(fr, (1, 3, 0, 2, 4)).reshape(batch, depth, 2, rg)
    fs = jnp.stack(s5_fin, axis=0)[:, 1:]
    nb = s5 // LANES
    fs = fs.reshape(depth, n_ctx_units, 2, nb, SUBLANES, 2, n_groups_s5 // nb, n_state)
    new_state_s5 = jnp.transpose(fs, (1, 4, 0, 2, 3, 6, 7, 5)).reshape(
        batch, depth, 2, n_groups_s5, n_state, 2)
    return y_prompt, y_sample, new_state_rglru, new_state_s5
```
